```python
import jax, jax.numpy as jnp
from jax import lax
import numpy as np


D_MODEL = 1024
BATCH = 8
SEQ = 4096
DEPTH = 4

N_META = 16
CHUNK = 128
PAD = CHUNK - N_META
EPS = 1e-6
SSD_HEADS = 16
SSD_HEAD_DIM = 64
SSD_D_INNER = SSD_HEADS * SSD_HEAD_DIM
SSD_GROUPS = 2
SSD_HEADS_PER_GROUP = SSD_HEADS // SSD_GROUPS
SSD_STATE = 128
SSD_CONV = 4
SSD_CONV_CH = SSD_D_INNER + 2 * SSD_GROUPS * SSD_STATE
MLA_HEADS = 16
MLA_NOPE = 64
MLA_ROPE = 32
MLA_V = 64
MLA_Q_RANK = 384
MLA_KV_RANK = 256
ROPE_BASE = 10000.0
IN_SPLITS = (SSD_D_INNER, SSD_CONV_CH, SSD_HEADS, MLA_Q_RANK, MLA_KV_RANK, MLA_ROPE)
D_IN = sum(IN_SPLITS)
AB_WIDTH = SSD_D_INNER + MLA_HEADS * MLA_V
LRU_WIDTH = 1280
LRU_BLOCKS = 10
LRU_BLOCK = LRU_WIDTH // LRU_BLOCKS
LRU_CONV = 4
LRU_C = 8.0
D_FF = 4 * D_MODEL
N_EVEN = (DEPTH + 1) // 2
N_ODD = DEPTH // 2

kernel_name = 'hybrid_ssd_mla_rglru_sandwich_meta'


def rmsnorm(x, g):
    xf = x.astype(jnp.float32)
    y = xf * lax.rsqrt(jnp.mean(xf * xf, axis=-1, keepdims=True) + EPS)
    return (y * g.astype(jnp.float32)).astype(x.dtype)


def _split(x, sizes):
    offs = np.cumsum(sizes)[:-1].tolist()
    return jnp.split(x, offs, axis=-1)


def pad_front(x, n):
    return jnp.pad(x, [(0, 0), (n, 0)] + [(0, 0)] * (x.ndim - 2))


def causal_dwconv(x, w, b):
    k, t = w.shape[0], x.shape[1]
    xp = jnp.pad(x, ((0, 0), (k - 1, 0), (0, 0)))
    out = b
    for i in range(k):
        out = out + xp[:, i:i + t] * w[i]
    return out


def rope_tables(t, dim):
    inv = ROPE_BASE ** (-jnp.arange(0, dim, 2, dtype=jnp.float32) / dim)
    ang = jnp.arange(t, dtype=jnp.float32)[:, None] * inv[None, :]
    return jnp.cos(ang), jnp.sin(ang)


def apply_rope(x, cos, sin):
    half = x.shape[-1] // 2
    x1, x2 = x[..., :half], x[..., half:]
    return jnp.concatenate([x1 * cos - x2 * sin, x2 * cos + x1 * sin], axis=-1).astype(x.dtype)


def ssd_chunked_scan(xdt, da, bm, cm):
    b, tp = xdt.shape[:2]
    nc = tp // CHUNK
    g, k = SSD_GROUPS, SSD_HEADS_PER_GROUP
    x = xdt.reshape(b, nc, CHUNK, g, k, SSD_HEAD_DIM)
    a = da.reshape(b, nc, CHUNK, g, k).transpose(0, 1, 3, 4, 2)
    bc = bm.reshape(b, nc, CHUNK, g, SSD_STATE)
    cc = cm.reshape(b, nc, CHUNK, g, SSD_STATE)
    a_cs = jnp.cumsum(a, axis=-1)
    causal = jnp.tril(jnp.ones((CHUNK, CHUNK), dtype=bool))
    seg = a_cs[..., :, None] - a_cs[..., None, :]
    decay_in = jnp.exp(jnp.where(causal, seg, -jnp.inf))
    cb = jnp.einsum('bclgn,bcsgn->bcgls', cc, bc)
    y_diag = jnp.einsum('bcgkls,bcsgkp->bclgkp', cb[:, :, :, None] * decay_in, x)
    decay_to_end = jnp.exp(a_cs[..., -1:] - a_cs)
    states = jnp.einsum('bclgn,bcgkl,bclgkp->bcgkpn', bc, decay_to_end, x).astype(jnp.float32)
    chunk_decay = jnp.exp(a_cs[..., -1])

    def step(h, inp):
        dec, st = inp
        return dec[..., None, None] * h + st, h

    h0 = jnp.zeros((b, g, k, SSD_HEAD_DIM, SSD_STATE), jnp.float32)
    _, prev = lax.scan(step, h0, (jnp.moveaxis(chunk_decay, 1, 0), jnp.moveaxis(states, 1, 0)))
    prev = jnp.moveaxis(prev, 0, 1)
    y_off = jnp.einsum('bclgn,bcgkpn,bcgkl->bclgkp', cc.astype(jnp.float32), prev, jnp.exp(a_cs))
    y = y_diag.astype(jnp.float32) + y_off
    return y.reshape(b, tp, SSD_HEADS, SSD_HEAD_DIM).astype(xdt.dtype)


def ssd_branch(z, xbc, dt_raw, conv_w, conv_b, dt_bias, a_log, d_skip, norm_g):
    b, t, _ = z.shape
    xbc = jax.nn.silu(causal_dwconv(xbc, conv_w, conv_b))
    xs, bm, cm = _split(xbc, (SSD_D_INNER, SSD_GROUPS * SSD_STATE, SSD_GROUPS * SSD_STATE))
    dt = jax.nn.softplus(dt_raw.astype(jnp.float32) + dt_bias.astype(jnp.float32))
    a = -jnp.exp(a_log.astype(jnp.float32))
    xh = xs.reshape(b, t, SSD_HEADS, SSD_HEAD_DIM)
    xdt = pad_front(xh * dt[..., None].astype(xh.dtype), PAD)
    da = pad_front(dt * a, PAD)
    bm = pad_front(bm.reshape(b, t, SSD_GROUPS, SSD_STATE), PAD)
    cm = pad_front(cm.reshape(b, t, SSD_GROUPS, SSD_STATE), PAD)
    y = ssd_chunked_scan(xdt, da, bm, cm)[:, PAD:]
    y = y + d_skip[:, None] * xh
    y = y.reshape(b, t, SSD_D_INNER)
    return rmsnorm(y * jax.nn.silu(z), norm_g)


def mla_branch(cq, ckv, krope, q_norm_g, w_q_up, kv_norm_g, w_kv_up, cos, sin):
    b, t, _ = cq.shape
    q = (rmsnorm(cq, q_norm_g) @ w_q_up).reshape(b, t, MLA_HEADS, MLA_NOPE + MLA_ROPE)
    q = jnp.concatenate([q[..., :MLA_NOPE], apply_rope(q[..., MLA_NOPE:], cos[:, None], sin[:, None])], axis=-1)
    kv = (rmsnorm(ckv, kv_norm_g) @ w_kv_up).reshape(b, t, MLA_HEADS, MLA_NOPE + MLA_V)
    k_r = apply_rope(krope, cos, sin)
    k = jnp.concatenate([kv[..., :MLA_NOPE],
                         jnp.broadcast_to(k_r[:, :, None], (b, t, MLA_HEADS, MLA_ROPE))], axis=-1)
    v = kv[..., MLA_NOPE:]
    qp, kp, vp = pad_front(q, PAD), pad_front(k, PAD), pad_front(v, PAD)
    tp = t + PAD
    nb = tp // CHUNK
    scale = (MLA_NOPE + MLA_ROPE) ** -0.5
    kidx = jnp.arange(tp)

    def block(j):
        q_blk = lax.dynamic_slice_in_dim(qp, j * CHUNK, CHUNK, axis=1)
        s = jnp.einsum('bqhd,bkhd->bhqk', q_blk, kp).astype(jnp.float32) * scale
        qidx = j * CHUNK + jnp.arange(CHUNK)
        mask = (kidx[None, :] <= qidx[:, None]) & (kidx[None, :] >= PAD)
        s = jnp.where(mask, s, -1e30)
        p = jax.nn.softmax(s, axis=-1).astype(vp.dtype)
        return jnp.einsum('bhqk,bkhd->bqhd', p, vp)

    o = lax.map(block, jnp.arange(nb))
    o = o.transpose(1, 0, 2, 3, 4).reshape(b, tp, MLA_HEADS * MLA_V)
    return o[:, PAD:]


def mixer_ssd_mla(h, w_in, conv_w, conv_b, dt_bias, a_log, d_skip, ssd_norm_g,
                  q_norm_g, w_q_up, kv_norm_g, w_kv_up, w_out, cos, sin):
    z, xbc, dt_raw, cq, ckv, krope = _split(h @ w_in, IN_SPLITS)
    y_ssd = ssd_branch(z, xbc, dt_raw, conv_w, conv_b, dt_bias, a_log, d_skip, ssd_norm_g)
    y_att = mla_branch(cq, ckv, krope, q_norm_g, w_q_up, kv_norm_g, w_kv_up, cos, sin)
    return jnp.concatenate([y_ssd, y_att], axis=-1) @ w_out


def _lru_combine(c1, c2):
    a1, b1 = c1
    a2, b2 = c2
    return a1 * a2, a2 * b1 + b2


def mixer_rglru(h, w_x, w_y, conv_w, conv_b, w_a, b_a, w_i, b_i, lam, w_out):
    b, t, _ = h.shape
    gate = jax.nn.gelu(h @ w_y)
    xr = causal_dwconv(h @ w_x, conv_w, conv_b)
    xb = xr.reshape(b, t, LRU_BLOCKS, LRU_BLOCK)
    r = jax.nn.sigmoid(jnp.einsum('btni,nij->btnj', xb, w_a).reshape(b, t, LRU_WIDTH) + b_a)
    i = jax.nn.sigmoid(jnp.einsum('btni,nij->btnj', xb, w_i).reshape(b, t, LRU_WIDTH) + b_i)
    log_a = -LRU_C * r.astype(jnp.float32) * jax.nn.softplus(-lam.astype(jnp.float32))
    a = jnp.exp(log_a)
    u = jnp.sqrt(-jnp.expm1(2.0 * log_a)) * (i * xr).astype(jnp.float32)
    _, hs = lax.associative_scan(_lru_combine, (a, u), axis=1)
    return (hs.astype(h.dtype) * gate) @ w_out


def _normal(k, shape, fan_in):
    return jax.random.normal(k, shape, jnp.float32) * (fan_in ** -0.5)


def setup_inputs(seed: int = 0) -> dict:
    key = jax.random.key(seed)
    ks = iter(jax.random.split(key, 40))
    gain = lambda k, shape: 1.0 + 0.02 * jax.random.normal(k, shape, jnp.float32)
    small = lambda k, shape: 0.02 * jax.random.normal(k, shape, jnp.float32)
    x = jax.random.normal(next(ks), (BATCH, SEQ, D_MODEL), jnp.float32)
    meta_tokens = jax.random.normal(next(ks), (N_META, D_MODEL), jnp.float32)
    mix_pre_g = gain(next(ks), (DEPTH, D_MODEL))
    mix_post_g = gain(next(ks), (DEPTH, D_MODEL))
    mlp_pre_g = gain(next(ks), (DEPTH, D_MODEL))
    mlp_post_g = gain(next(ks), (DEPTH, D_MODEL))
    w_up = _normal(next(ks), (DEPTH, D_MODEL, D_FF), D_MODEL)
    w_down = _normal(next(ks), (DEPTH, D_FF, D_MODEL), D_FF)
    w_in = _normal(next(ks), (N_EVEN, D_MODEL, D_IN), D_MODEL)
    ssd_conv_w = _normal(next(ks), (N_EVEN, SSD_CONV, SSD_CONV_CH), SSD_CONV)
    ssd_conv_b = small(next(ks), (N_EVEN, SSD_CONV_CH))
    dt = jnp.exp(jax.random.uniform(next(ks), (N_EVEN, SSD_HEADS), jnp.float32,
                                    minval=np.log(1e-3), maxval=np.log(1e-1)))
    ssd_dt_bias = dt + jnp.log(-jnp.expm1(-dt))
    ssd_a_log = jnp.log(jax.random.uniform(next(ks), (N_EVEN, SSD_HEADS), jnp.float32, minval=1.0, maxval=16.0))
    ssd_d = 1.0 + 0.1 * jax.random.normal(next(ks), (N_EVEN, SSD_HEADS), jnp.float32)
    ssd_norm_g = gain(next(ks), (N_EVEN, SSD_D_INNER))
    mla_q_norm_g = gain(next(ks), (N_EVEN, MLA_Q_RANK))
    mla_w_q_up = _normal(next(ks), (N_EVEN, MLA_Q_RANK, MLA_HEADS * (MLA_NOPE + MLA_ROPE)), MLA_Q_RANK)
    mla_kv_norm_g = gain(next(ks), (N_EVEN, MLA_KV_RANK))
    mla_w_kv_up = _normal(next(ks), (N_EVEN, MLA_KV_RANK, MLA_HEADS * (MLA_NOPE + MLA_V)), MLA_KV_RANK)
    w_out_ab = _normal(next(ks), (N_EVEN, AB_WIDTH, D_MODEL), AB_WIDTH)
    rg_w_x = _normal(next(ks), (N_ODD, D_MODEL, LRU_WIDTH), D_MODEL)
    rg_w_y = _normal(next(ks), (N_ODD, D_MODEL, LRU_WIDTH), D_MODEL)
    rg_conv_w = _normal(next(ks), (N_ODD, LRU_CONV, LRU_WIDTH), LRU_CONV)
    rg_conv_b = small(next(ks), (N_ODD, LRU_WIDTH))
    rg_w_a = _normal(next(ks), (N_ODD, LRU_BLOCKS, LRU_BLOCK, LRU_BLOCK), LRU_BLOCK)
    rg_b_a = small(next(ks), (N_ODD, LRU_WIDTH))
    rg_w_i = _normal(next(ks), (N_ODD, LRU_BLOCKS, LRU_BLOCK, LRU_BLOCK), LRU_BLOCK)
    rg_b_i = small(next(ks), (N_ODD, LRU_WIDTH))
    a8 = jax.random.uniform(next(ks), (N_ODD, LRU_WIDTH), jnp.float32, minval=0.9, maxval=0.999)
    base = a8 ** (1.0 / LRU_C)
    rg_lambda = jnp.log(base) - jnp.log1p(-base)
    rg_w_out = _normal(next(ks), (N_ODD, LRU_WIDTH, D_MODEL), LRU_WIDTH)
    return {'x': x, 'meta_tokens': meta_tokens, 'mix_pre_g': mix_pre_g, 'mix_post_g': mix_post_g,
            'mlp_pre_g': mlp_pre_g, 'mlp_post_g': mlp_post_g, 'w_up': w_up, 'w_down': w_down,
            'w_in': w_in, 'ssd_conv_w': ssd_conv_w, 'ssd_conv_b': ssd_conv_b, 'ssd_dt_bias': ssd_dt_bias,
            'ssd_a_log': ssd_a_log, 'ssd_d': ssd_d, 'ssd_norm_g': ssd_norm_g,
            'mla_q_norm_g': mla_q_norm_g, 'mla_w_q_up': mla_w_q_up, 'mla_kv_norm_g': mla_kv_norm_g,
            'mla_w_kv_up': mla_w_kv_up, 'w_out_ab': w_out_ab, 'rg_w_x': rg_w_x, 'rg_w_y': rg_w_y,
            'rg_conv_w': rg_conv_w, 'rg_conv_b': rg_conv_b, 'rg_w_a': rg_w_a, 'rg_b_a': rg_b_a,
            'rg_w_i': rg_w_i, 'rg_b_i': rg_b_i, 'rg_lambda': rg_lambda, 'rg_w_out': rg_w_out}


def reference(x, meta_tokens, mix_pre_g, mix_post_g, mlp_pre_g, mlp_post_g, w_up, w_down,
              w_in, ssd_conv_w, ssd_conv_b, ssd_dt_bias, ssd_a_log, ssd_d, ssd_norm_g,
              mla_q_norm_g, mla_w_q_up, mla_kv_norm_g, mla_w_kv_up, w_out_ab,
              rg_w_x, rg_w_y, rg_conv_w, rg_conv_b, rg_w_a, rg_b_a, rg_w_i, rg_b_i, rg_lambda, rg_w_out):
    b = x.shape[0]
    meta = jnp.broadcast_to(meta_tokens[None].astype(x.dtype), (b, N_META, D_MODEL))
    h = jnp.concatenate([meta, x], axis=1)
    cos, sin = rope_tables(h.shape[1], MLA_ROPE)
    for layer in range(DEPTH):
        hn = rmsnorm(h, mix_pre_g[layer])
        if layer % 2 == 0:
            e = layer // 2
            m = mixer_ssd_mla(hn, w_in[e], ssd_conv_w[e], ssd_conv_b[e], ssd_dt_bias[e], ssd_a_log[e],
                              ssd_d[e], ssd_norm_g[e], mla_q_norm_g[e], mla_w_q_up[e], mla_kv_norm_g[e],
                              mla_w_kv_up[e], w_out_ab[e], cos, sin)
        else:
            o = layer // 2
            m = mixer_rglru(hn, rg_w_x[o], rg_w_y[o], rg_conv_w[o], rg_conv_b[o], rg_w_a[o], rg_b_a[o],
                            rg_w_i[o], rg_b_i[o], rg_lambda[o], rg_w_out[o])
        h = h + rmsnorm(m, mix_post_g[layer])
        hn = rmsnorm(h, mlp_pre_g[layer])
        u = jnp.square(jax.nn.relu(hn @ w_up[layer]))
        h = h + rmsnorm(u @ w_down[layer], mlp_post_g[layer])
    return h[:, N_META:]
```

```python
import functools

import jax
import jax.numpy as jnp
import numpy as np
from jax import lax
from jax.experimental import pallas as pl
from jax.experimental.pallas import tpu as pltpu

F32 = jnp.float32
BF16 = jnp.bfloat16

N_META = 16
CHUNK = 128
PAD = CHUNK - N_META
EPS = 1e-6
SSD_HEADS = 16
SSD_HEAD_DIM = 64
SSD_D_INNER = SSD_HEADS * SSD_HEAD_DIM
SSD_GROUPS = 2
SSD_STATE = 128
SSD_CONV = 4
SSD_BC = SSD_GROUPS * SSD_STATE
SSD_CONV_CH = SSD_D_INNER + 2 * SSD_BC
MLA_HEADS = 16
MLA_NOPE = 64
MLA_ROPE = 32
MLA_V = 64
MLA_Q_RANK = 384
MLA_KV_RANK = 256
ROPE_BASE = 10000.0
LRU_BLOCKS = 10
LRU_BLOCK = 128
LRU_WIDTH = LRU_BLOCKS * LRU_BLOCK
LRU_C = 8.0
LANES = 128
SUBLANES = 8
VMEM_LIMIT_BYTES = 56 * 1024 * 1024
NEG_BIG = -1e30


def _cparams(*sem):
    return pltpu.CompilerParams(dimension_semantics=sem, vmem_limit_bytes=VMEM_LIMIT_BYTES)


def _pick_tile(n, candidates):
    for c in candidates:
        if n % c == 0:
            return c
    raise ValueError(f"no tile in {candidates} divides {n}")


def _const_spec(shape):
    nd = len(shape)
    return pl.BlockSpec(shape, lambda *_: (0,) * nd)


def _rms(x, g):
    ms = jnp.mean(x * x, axis=-1, keepdims=True)
    return x * lax.rsqrt(ms + EPS) * g


def _sigmoid(x):
    return 1.0 / (1.0 + jnp.exp(-x))


def _softplus(x):
    return jnp.maximum(x, 0.0) + jnp.log(1.0 + jnp.exp(-jnp.abs(x)))


def _dot(a, b):
    return jnp.dot(a, b, preferred_element_type=F32)


def _dot_nt(a, b):
    return lax.dot_general(a, b, (((1,), (1,)), ((), ())), preferred_element_type=F32)


def _dot_tn(a, b):
    return lax.dot_general(a, b, (((0,), (0,)), ((), ())), preferred_element_type=F32)


def _shift_rows(cur, tail, k):
    if k == 0:
        return cur
    rolled = pltpu.roll(cur, k, 0)
    row = lax.broadcasted_iota(jnp.int32, tail.shape, 0)
    top = jnp.where(row < k, pltpu.roll(tail, k, 0), rolled[0:SUBLANES])
    return jnp.concatenate([top, rolled[SUBLANES:]], axis=0)


def _causal_conv(cur, tail, w_ref, b_ref):
    k = w_ref.shape[0]
    out = b_ref[...]
    for i in range(k):
        out = out + _shift_rows(cur, tail, k - 1 - i) * w_ref[i:i + 1, :]
    return out


def _proj_even_kernel(h_ref, g_ref, wz_ref, wxbc_ref, wdt_ref, wkrp_ref, wkrs_ref, wcq_ref, qg_ref,
                      wckv_ref, kvg_ref, wqp_ref, wqs_ref, wkp_ref, wv_ref, cq_ref, sq_ref, ck_ref, sk_ref,
                      z_ref, xbc_ref, dt_ref, q_ref, k_ref, v_ref):
    hn = _rms(h_ref[...], g_ref[...]).astype(BF16)
    z_ref[...] = _dot(hn, wz_ref[...]).astype(BF16)
    xbc_ref[...] = _dot(hn, wxbc_ref[...]).astype(BF16)
    dt_ref[...] = _dot(hn, wdt_ref[...])
    kr = _dot(hn, wkrp_ref[...]) * ck_ref[...] + _dot(hn, wkrs_ref[...]) * sk_ref[...]
    cqn = _rms(_dot(hn, wcq_ref[...]), qg_ref[...]).astype(BF16)
    ckvn = _rms(_dot(hn, wckv_ref[...]), kvg_ref[...]).astype(BF16)
    v_ref[...] = _dot(ckvn, wv_ref[...]).astype(BF16)
    cq_t = cq_ref[...]
    sq_t = sq_ref[...]
    for hd in range(MLA_HEADS):
        sl = slice(hd * LANES, (hd + 1) * LANES)
        qh = _dot(cqn, wqp_ref[:, sl]) * cq_t + _dot(cqn, wqs_ref[:, sl]) * sq_t
        q_ref[hd] = qh.astype(BF16)
        k_ref[hd] = (_dot(ckvn, wkp_ref[:, sl]) + kr).astype(BF16)


def _proj_even(h3, g, wts, tables):
    b, tp, d = h3.shape
    tm = _pick_tile(tp, (384, 128))
    nt = tp // tm
    n = b * tp
    row3 = lambda bi, i: (bi, i, 0)
    flat = lambda bi, i: (bi * nt + i, 0)
    tab = lambda bi, i: (i, 0)
    in_specs = [pl.BlockSpec((None, tm, d), row3), _const_spec(g.shape)]
    in_specs += [_const_spec(w.shape) for w in wts]
    in_specs += [pl.BlockSpec((tm, LANES), tab) for _ in tables]
    out_shape = (
        jax.ShapeDtypeStruct((n, SSD_D_INNER), BF16),
        jax.ShapeDtypeStruct((n, SSD_CONV_CH), BF16),
        jax.ShapeDtypeStruct((n, LANES), F32),
        jax.ShapeDtypeStruct((MLA_HEADS, n, LANES), BF16),
        jax.ShapeDtypeStruct((MLA_HEADS, n, LANES), BF16),
        jax.ShapeDtypeStruct((n, MLA_HEADS * MLA_V), BF16),
    )
    head3 = lambda bi, i: (0, bi * nt + i, 0)
    out_specs = (
        pl.BlockSpec((tm, SSD_D_INNER), flat),
        pl.BlockSpec((tm, SSD_CONV_CH), flat),
        pl.BlockSpec((tm, LANES), flat),
        pl.BlockSpec((MLA_HEADS, tm, LANES), head3),
        pl.BlockSpec((MLA_HEADS, tm, LANES), head3),
        pl.BlockSpec((tm, MLA_HEADS * MLA_V), flat),
    )
    return pl.pallas_call(
        _proj_even_kernel, grid=(b, nt), in_specs=in_specs, out_specs=out_specs, out_shape=out_shape,
        compiler_params=_cparams("parallel", "parallel"), name="proj_even",
    )(h3, g, *wts, *tables)


def _cumsum_rows(x):
    n = x.shape[0]
    row = lax.broadcasted_iota(jnp.int32, x.shape, 0)
    d = 1
    while d < n:
        x = x + jnp.where(row >= d, pltpu.roll(x, d, 0), 0.0)
        d *= 2
    return x


def _ssd_kernel(xbc_ref, dt_ref, z_ref, cw_ref, cb_ref, dtb_ref, alog_ref, dsk_ref, ng_ref, y_ref,
                tail_scr, state_scr, y_scr):
    c = pl.program_id(1)
    L = CHUNK
    heads_per_group = SSD_HEADS // SSD_GROUPS

    @pl.when(c == 0)
    def _():
        tail_scr[...] = jnp.zeros_like(tail_scr)
        state_scr[...] = jnp.zeros_like(state_scr)

    cur = xbc_ref[...].astype(F32)
    conv = _causal_conv(cur, tail_scr[...], cw_ref, cb_ref)
    tail_scr[...] = cur[L - SUBLANES:L]
    row = lax.broadcasted_iota(jnp.int32, (L, 1), 0)
    valid = (c * L + row) >= PAD
    act = jnp.where(valid, conv * _sigmoid(conv), 0.0)
    xs = act[:, :SSD_D_INNER]
    bm = act[:, SSD_D_INNER:SSD_D_INNER + SSD_BC]
    cm = act[:, SSD_D_INNER + SSD_BC:]

    dtv = jnp.where(valid, _softplus(dt_ref[...] + dtb_ref[...]), 0.0)
    a_cs = _cumsum_rows(dtv * (-jnp.exp(alog_ref[...])))
    a_cs_t = a_cs.T
    dt_t = dtv.T
    tot = a_cs[L - 1:L, :]
    w_l = jnp.exp(tot - a_cs) * dtv
    e_l = jnp.exp(a_cs)
    cd = jnp.exp(tot)
    causal = lax.broadcasted_iota(jnp.int32, (L, L), 0) >= lax.broadcasted_iota(jnp.int32, (L, L), 1)
    lo = lax.broadcasted_iota(jnp.int32, (L, LANES), 1) < SSD_HEAD_DIM

    for g in range(SSD_GROUPS):
        cc = cm[:, g * SSD_STATE:(g + 1) * SSD_STATE]
        bc = bm[:, g * SSD_STATE:(g + 1) * SSD_STATE]
        cb = _dot_nt(cc.astype(BF16), bc.astype(BF16))
        for hp in range(heads_per_group // 2):
            pair = g * (heads_per_group // 2) + hp
            sl = slice(pair * LANES, (pair + 1) * LANES)
            xs_pair = xs[:, sl]
            st_pair = state_scr[:, sl]
            xs_b = xs_pair.astype(BF16)
            rhs = jnp.concatenate([xs_b, st_pair.astype(BF16)], axis=0)
            ys, sts, cds = [], [], []
            for hh in range(2):
                h = 2 * pair + hh
                seg = a_cs[:, h:h + 1] - a_cs_t[h:h + 1, :]
                dec = jnp.exp(jnp.where(causal, seg, -jnp.inf))
                m = cb * dec * dt_t[h:h + 1, :]
                lhs = jnp.concatenate([m, cc * e_l[:, h:h + 1]], axis=1).astype(BF16)
                ys.append(_dot(lhs, rhs))
                bcw = (bc * w_l[:, h:h + 1]).astype(BF16)
                sts.append(_dot_tn(bcw, xs_b))
                cds.append(cd[:, h:h + 1])
            y_scr[:, sl] = jnp.where(lo, ys[0], ys[1]) + dsk_ref[:, sl] * xs_pair
            state_scr[:, sl] = jnp.where(lo, cds[0], cds[1]) * st_pair + jnp.where(lo, sts[0], sts[1])

    zf = z_ref[...].astype(F32)
    y_ref[...] = _rms(y_scr[...] * (zf * _sigmoid(zf)), ng_ref[...]).astype(BF16)


def _ssd(xbc3, dt3, z3, cw, cb, dtb, alog, dsk, ng):
    b, tp, _ = xbc3.shape
    nc = tp // CHUNK
    row3 = lambda bi, ci: (bi, ci, 0)
    in_specs = [
        pl.BlockSpec((None, CHUNK, SSD_CONV_CH), row3),
        pl.BlockSpec((None, CHUNK, LANES), row3),
        pl.BlockSpec((None, CHUNK, SSD_D_INNER), row3),
    ] + [_const_spec(a.shape) for a in (cw, cb, dtb, alog, dsk, ng)]
    return pl.pallas_call(
        _ssd_kernel, grid=(b, nc), in_specs=in_specs,
        out_specs=pl.BlockSpec((None, CHUNK, SSD_D_INNER), row3),
        out_shape=jax.ShapeDtypeStruct((b, tp, SSD_D_INNER), BF16),
        scratch_shapes=[pltpu.VMEM((SUBLANES, SSD_CONV_CH), F32),
                        pltpu.VMEM((SSD_STATE, SSD_D_INNER), F32),
                        pltpu.VMEM((CHUNK, SSD_D_INNER), F32)],
        compiler_params=_cparams("parallel", "arbitrary"), name="ssd",
    )(xbc3, dt3, z3, cw, cb, dtb, alog, dsk, ng)


ATTN_KV_BLOCK = 256


def _attn_kernel(q_ref, k_ref, v_ref, o_ref, m_scr, l_scr, acc_scr, *, tq):
    qi = pl.program_id(2)
    q0 = qi * tq
    m_scr[...] = jnp.full_like(m_scr, NEG_BIG)
    l_scr[...] = jnp.zeros_like(l_scr)
    acc_scr[...] = jnp.zeros_like(acc_scr)

    def block(off, size):
        kv_pos = off + lax.broadcasted_iota(jnp.int32, (tq, size), 1)
        q_pos = q0 + lax.broadcasted_iota(jnp.int32, (tq, size), 0)
        mask = (kv_pos <= q_pos) & (kv_pos >= PAD)
        vb = v_ref[pl.ds(off, size), :]
        for hh in range(2):
            kb = k_ref[hh, pl.ds(off, size), :]
            s = jnp.where(mask, _dot_nt(q_ref[hh], kb), NEG_BIG)
            m_prev = m_scr[hh]
            m_new = jnp.maximum(m_prev, jnp.max(s, axis=-1, keepdims=True))
            alpha = jnp.exp(m_prev - m_new)
            p = jnp.exp(s - m_new)
            l_scr[hh] = alpha * l_scr[hh] + jnp.sum(p, axis=-1, keepdims=True)
            acc_scr[hh] = alpha * acc_scr[hh] + _dot(p.astype(BF16), vb)
            m_scr[hh] = m_new

    end = q0 + tq
    n_full = end // ATTN_KV_BLOCK

    def body(j, carry):
        block(pl.multiple_of(j * ATTN_KV_BLOCK, ATTN_KV_BLOCK), ATTN_KV_BLOCK)
        return carry

    lax.fori_loop(0, n_full, body, 0)

    @pl.when(end % ATTN_KV_BLOCK != 0)
    def _():
        block(pl.multiple_of(n_full * ATTN_KV_BLOCK, CHUNK), CHUNK)

    lo = lax.broadcasted_iota(jnp.int32, (tq, LANES), 1) < MLA_V
    out = jnp.where(lo, acc_scr[0] / l_scr[0], acc_scr[1] / l_scr[1])
    row = q0 + lax.broadcasted_iota(jnp.int32, (tq, 1), 0)
    o_ref[...] = jnp.where(row >= PAD, out, 0.0).astype(BF16)


def _attention(q, k, v, b, tp):
    tq = _pick_tile(tp, (384, 128))
    nq = tp // tq
    n = b * tp
    in_specs = [
        pl.BlockSpec((2, tq, LANES), lambda bi, hp, qi: (hp, bi * nq + qi, 0)),
        pl.BlockSpec((2, tp, LANES), lambda bi, hp, qi: (hp, bi, 0)),
        pl.BlockSpec((tp, LANES), lambda bi, hp, qi: (bi, hp)),
    ]
    return pl.pallas_call(
        functools.partial(_attn_kernel, tq=tq), grid=(b, MLA_HEADS // 2, nq), in_specs=in_specs,
        out_specs=pl.BlockSpec((tq, LANES), lambda bi, hp, qi: (bi * nq + qi, hp)),
        out_shape=jax.ShapeDtypeStruct((n, MLA_HEADS * MLA_V), BF16),
        scratch_shapes=[pltpu.VMEM((2, tq, 1), F32), pltpu.VMEM((2, tq, 1), F32),
                        pltpu.VMEM((2, tq, LANES), F32)],
        compiler_params=_cparams("parallel", "parallel", "arbitrary"), name="attention",
    )(q, k, v)


def _post_kernel(*refs, n_in):
    h_ref = refs[0]
    a_refs = refs[1:1 + n_in]
    w_refs = refs[1 + n_in:1 + 2 * n_in]
    g_ref = refs[1 + 2 * n_in]
    o_ref = refs[2 + 2 * n_in]
    m = _dot(a_refs[0][...], w_refs[0][...])
    for a_ref, w_ref in zip(a_refs[1:], w_refs[1:]):
        m = m + _dot(a_ref[...], w_ref[...])
    o_ref[...] = h_ref[...] + _rms(m, g_ref[...])


def _post(h2, acts, wts, g):
    n, d = h2.shape
    tm = _pick_tile(n, (512, 384, 256, 128))
    rows = lambda i: (i, 0)
    in_specs = [pl.BlockSpec((tm, d), rows)]
    in_specs += [pl.BlockSpec((tm, a.shape[1]), rows) for a in acts]
    in_specs += [_const_spec(w.shape) for w in wts] + [_const_spec(g.shape)]
    return pl.pallas_call(
        functools.partial(_post_kernel, n_in=len(acts)), grid=(n // tm,), in_specs=in_specs,
        out_specs=pl.BlockSpec((tm, d), rows), out_shape=jax.ShapeDtypeStruct((n, d), F32),
        compiler_params=_cparams("parallel"), name="mixer_out",
    )(h2, *acts, *wts, g)


def _mlp_kernel(h_ref, g1_ref, wup_ref, wdn_ref, g2_ref, o_ref, *, ff_chunk):
    x = h_ref[...]
    hn = _rms(x, g1_ref[...]).astype(BF16)
    d_ff = wup_ref.shape[1]
    y = None
    for c0 in range(0, d_ff, ff_chunk):
        u = jnp.maximum(_dot(hn, wup_ref[:, c0:c0 + ff_chunk]), 0.0)
        part = _dot((u * u).astype(BF16), wdn_ref[c0:c0 + ff_chunk, :])
        y = part if y is None else y + part
    o_ref[...] = x + _rms(y, g2_ref[...])


def _mlp(h2, g1, wup, wdn, g2):
    n, d = h2.shape
    tm = _pick_tile(n, (256, 128))
    rows = lambda i: (i, 0)
    in_specs = [pl.BlockSpec((tm, d), rows), _const_spec(g1.shape), _const_spec(wup.shape),
                _const_spec(wdn.shape), _const_spec(g2.shape)]
    return pl.pallas_call(
        functools.partial(_mlp_kernel, ff_chunk=1024), grid=(n // tm,), in_specs=in_specs,
        out_specs=pl.BlockSpec((tm, d), rows), out_shape=jax.ShapeDtypeStruct((n, d), F32),
        compiler_params=_cparams("parallel"), name="mlp",
    )(h2, g1, wup, wdn, g2)


def _proj_odd_kernel(h_ref, g_ref, wx_ref, wy_ref, x_ref, gate_ref):
    hn = _rms(h_ref[...], g_ref[...]).astype(BF16)
    x_ref[...] = _dot(hn, wx_ref[...]).astype(BF16)
    y = _dot(hn, wy_ref[...])
    gelu = 0.5 * y * (1.0 + jnp.tanh(np.sqrt(2.0 / np.pi).astype(np.float32) * (y + 0.044715 * (y * y * y))))
    gate_ref[...] = gelu.astype(BF16)


def _proj_odd(h2, g, wx, wy):
    n, d = h2.shape
    tm = _pick_tile(n, (512, 384, 256, 128))
    rows = lambda i: (i, 0)
    w = wx.shape[1]
    return pl.pallas_call(
        _proj_odd_kernel, grid=(n // tm,),
        in_specs=[pl.BlockSpec((tm, d), rows), _const_spec(g.shape), _const_spec(wx.shape), _const_spec(wy.shape)],
        out_specs=(pl.BlockSpec((tm, w), rows), pl.BlockSpec((tm, w), rows)),
        out_shape=(jax.ShapeDtypeStruct((n, w), BF16), jax.ShapeDtypeStruct((n, w), BF16)),
        compiler_params=_cparams("parallel"), name="proj_odd",
    )(h2, g, wx, wy)


def _rglru_kernel(x_ref, gate_ref, cw_ref, cb_ref, wa_ref, ba_ref, wi_ref, bi_ref, lam_ref, o_ref,
                  tail_scr, h_scr):
    t = pl.program_id(1)
    L = x_ref.shape[0]

    @pl.when(t == 0)
    def _():
        tail_scr[...] = jnp.zeros_like(tail_scr)
        h_scr[...] = jnp.zeros_like(h_scr)

    cur = x_ref[...].astype(F32)
    xr = _causal_conv(cur, tail_scr[...], cw_ref, cb_ref)
    tail_scr[...] = cur[L - SUBLANES:L]
    rs, gs = [], []
    for nb in range(LRU_BLOCKS):
        sl = slice(nb * LRU_BLOCK, (nb + 1) * LRU_BLOCK)
        xb = xr[:, sl].astype(BF16)
        rs.append(_dot(xb, wa_ref[nb]))
        gs.append(_dot(xb, wi_ref[nb]))
    r = _sigmoid(jnp.concatenate(rs, axis=1) + ba_ref[...])
    ig = _sigmoid(jnp.concatenate(gs, axis=1) + bi_ref[...])
    log_a = (-LRU_C) * r * _softplus(-lam_ref[...])
    a = jnp.exp(log_a)
    row = lax.broadcasted_iota(jnp.int32, (L, 1), 0)
    valid = (t * L + row) >= PAD
    u = jnp.where(valid, jnp.sqrt(1.0 - a * a) * (ig * xr), 0.0)

    rowf = lax.broadcasted_iota(jnp.int32, (L, LRU_WIDTH), 0)
    d = 1
    while d < L:
        keep = rowf >= d
        a_s = jnp.where(keep, pltpu.roll(a, d, 0), 1.0)
        u_s = jnp.where(keep, pltpu.roll(u, d, 0), 0.0)
        u = a * u_s + u
        a = a * a_s
        d *= 2
    hs = u + a * h_scr[0:1, :]
    h_scr[...] = jnp.broadcast_to(hs[L - 1:L, :], h_scr.shape)
    o_ref[...] = (hs * gate_ref[...].astype(F32)).astype(BF16)


def _rglru(x3, gate3, cw, cb, wa, ba, wi, bi, lam):
    b, tp, w = x3.shape
    tt = CHUNK
    row3 = lambda bi_, ti: (bi_, ti, 0)
    in_specs = [pl.BlockSpec((None, tt, w), row3), pl.BlockSpec((None, tt, w), row3)]
    in_specs += [_const_spec(a.shape) for a in (cw, cb, wa, ba, wi, bi, lam)]
    return pl.pallas_call(
        _rglru_kernel, grid=(b, tp // tt), in_specs=in_specs,
        out_specs=pl.BlockSpec((None, tt, w), row3),
        out_shape=jax.ShapeDtypeStruct((b, tp, w), BF16),
        scratch_shapes=[pltpu.VMEM((SUBLANES, w), F32), pltpu.VMEM((SUBLANES, w), F32)],
        compiler_params=_cparams("parallel", "arbitrary"), name="rglru",
    )(x3, gate3, cw, cb, wa, ba, wi, bi, lam)


def _rope_tables(tp):
    half = MLA_ROPE // 2
    inv = ROPE_BASE ** (-jnp.arange(0, MLA_ROPE, 2, dtype=F32) / MLA_ROPE)
    pos = jnp.maximum(jnp.arange(tp, dtype=F32) - PAD, 0.0)
    ang = pos[:, None] * inv[None, :]
    cos, sin = jnp.cos(ang), jnp.sin(ang)
    zeros_lo = jnp.zeros((tp, MLA_NOPE), F32)
    zeros_hi = jnp.zeros((tp, LANES - MLA_NOPE - MLA_ROPE), F32)
    ck = jnp.concatenate([zeros_lo, cos, cos, zeros_hi], axis=1)
    sk = jnp.concatenate([zeros_lo, sin, sin, zeros_hi], axis=1)
    scale = (MLA_NOPE + MLA_ROPE) ** -0.5
    cq = jnp.concatenate([jnp.ones((tp, MLA_NOPE), F32), cos, cos, zeros_hi], axis=1) * scale
    sq = sk * scale
    del half
    return cq, sq, ck, sk


def _even_weights(w_in, w_q_up, w_kv_up):
    d = w_in.shape[0]
    offs = np.cumsum((SSD_D_INNER, SSD_CONV_CH, SSD_HEADS, MLA_Q_RANK, MLA_KV_RANK, MLA_ROPE))
    wz = w_in[:, :offs[0]]
    wxbc = w_in[:, offs[0]:offs[1]]
    wdt = w_in[:, offs[1]:offs[2]]
    wcq = w_in[:, offs[2]:offs[3]]
    wckv = w_in[:, offs[3]:offs[4]]
    wkr = w_in[:, offs[4]:offs[5]]
    half = MLA_ROPE // 2
    hi_pad = LANES - MLA_NOPE - MLA_ROPE
    wdt_p = jnp.pad(wdt, ((0, 0), (0, LANES - SSD_HEADS)))
    wkr_p = jnp.pad(wkr, ((0, 0), (MLA_NOPE, hi_pad)))
    wkr_s = jnp.pad(jnp.concatenate([-wkr[:, half:], wkr[:, :half]], axis=1), ((0, 0), (MLA_NOPE, hi_pad)))
    wq = w_q_up.reshape(MLA_Q_RANK, MLA_HEADS, MLA_NOPE + MLA_ROPE)
    q_nope, q_rope = wq[..., :MLA_NOPE], wq[..., MLA_NOPE:]
    wq_p = jnp.pad(wq, ((0, 0), (0, 0), (0, hi_pad))).reshape(MLA_Q_RANK, MLA_HEADS * LANES)
    q_swap = jnp.concatenate([jnp.zeros_like(q_nope), -q_rope[..., half:], q_rope[..., :half]], axis=-1)
    wq_s = jnp.pad(q_swap, ((0, 0), (0, 0), (0, hi_pad))).reshape(MLA_Q_RANK, MLA_HEADS * LANES)
    wkv = w_kv_up.reshape(MLA_KV_RANK, MLA_HEADS, MLA_NOPE + MLA_V)
    wk_p = jnp.pad(wkv[..., :MLA_NOPE], ((0, 0), (0, 0), (0, LANES - MLA_NOPE))).reshape(
        MLA_KV_RANK, MLA_HEADS * LANES)
    wv = wkv[..., MLA_NOPE:].reshape(MLA_KV_RANK, MLA_HEADS * MLA_V)
    del d
    bf = lambda a: a.astype(BF16)
    return dict(wz=bf(wz), wxbc=bf(wxbc), wdt=bf(wdt_p), wkr_p=bf(wkr_p), wkr_s=bf(wkr_s), wcq=bf(wcq),
                wckv=bf(wckv), wq_p=bf(wq_p), wq_s=bf(wq_s), wk_p=bf(wk_p), wv=bf(wv))


def _row(a):
    return a.reshape(1, -1).astype(F32)


def kernel(x, meta_tokens, mix_pre_g, mix_post_g, mlp_pre_g, mlp_post_g, w_up, w_down, w_in, ssd_conv_w,
           ssd_conv_b, ssd_dt_bias, ssd_a_log, ssd_d, ssd_norm_g, mla_q_norm_g, mla_w_q_up, mla_kv_norm_g,
           mla_w_kv_up, w_out_ab, rg_w_x, rg_w_y, rg_conv_w, rg_conv_b, rg_w_a, rg_b_a, rg_w_i, rg_b_i,
           rg_lambda, rg_w_out):
    b, seq, d = x.shape
    depth = mix_pre_g.shape[0]
    tp = PAD + N_META + seq
    n = b * tp
    meta = jnp.broadcast_to(meta_tokens[None].astype(x.dtype), (b, N_META, d))
    h = jnp.concatenate([jnp.zeros((b, PAD, d), x.dtype), meta, x], axis=1).reshape(n, d)
    tables = _rope_tables(tp)

    for layer in range(depth):
        if layer % 2 == 0:
            e = layer // 2
            w = _even_weights(w_in[e], mla_w_q_up[e], mla_w_kv_up[e])
            wts = (w["wz"], w["wxbc"], w["wdt"], w["wkr_p"], w["wkr_s"], w["wcq"], _row(mla_q_norm_g[e]),
                   w["wckv"], _row(mla_kv_norm_g[e]), w["wq_p"], w["wq_s"], w["wk_p"], w["wv"])
            z, xbc, dt, q, k, v = _proj_even(h.reshape(b, tp, d), _row(mix_pre_g[layer]), wts, tables)
            lane_pad = lambda a: jnp.pad(_row(a), ((0, 0), (0, LANES - SSD_HEADS)))
            y_ssd = _ssd(xbc.reshape(b, tp, SSD_CONV_CH), dt.reshape(b, tp, LANES), z.reshape(b, tp, SSD_D_INNER),
                         ssd_conv_w[e].astype(F32), _row(ssd_conv_b[e]), lane_pad(ssd_dt_bias[e]),
                         lane_pad(ssd_a_log[e]), _row(jnp.repeat(ssd_d[e], SSD_HEAD_DIM)), _row(ssd_norm_g[e]))
            y_att = _attention(q, k, v, b, tp)
            wo = w_out_ab[e].astype(BF16)
            h = _post(h, (y_ssd.reshape(n, SSD_D_INNER), y_att), (wo[:SSD_D_INNER], wo[SSD_D_INNER:]),
                      _row(mix_post_g[layer]))
        else:
            o = layer // 2
            xr, gate = _proj_odd(h, _row(mix_pre_g[layer]), rg_w_x[o].astype(BF16), rg_w_y[o].astype(BF16))
            y = _rglru(xr.reshape(b, tp, LRU_WIDTH), gate.reshape(b, tp, LRU_WIDTH), rg_conv_w[o].astype(F32),
                       _row(rg_conv_b[o]), rg_w_a[o].astype(BF16), _row(rg_b_a[o]), rg_w_i[o].astype(BF16),
                       _row(rg_b_i[o]), _row(rg_lambda[o]))
            h = _post(h, (y.reshape(n, LRU_WIDTH),), (rg_w_out[o].astype(BF16),), _row(mix_post_g[layer]))
        h = _mlp(h, _row(mlp_pre_g[layer]), w_up[layer].astype(BF16), w_down[layer].astype(BF16),
                 _row(mlp_post_g[layer]))
    return h.reshape(b, tp, d)[:, PAD + N_META:]
```

```python
import functools

import jax
import jax.numpy as jnp
import numpy as np
from jax import lax
from jax.experimental import pallas as pl
from jax.experimental.pallas import tpu as pltpu

F32 = jnp.float32
BF16 = jnp.bfloat16

N_META = 16
CHUNK = 128
PAD = CHUNK - N_META
EPS = 1e-6
SSD_HEADS = 16
SSD_HEAD_DIM = 64
SSD_D_INNER = SSD_HEADS * SSD_HEAD_DIM
SSD_GROUPS = 2
SSD_STATE = 128
SSD_CONV = 4
SSD_BC = SSD_GROUPS * SSD_STATE
SSD_CONV_CH = SSD_D_INNER + 2 * SSD_BC
MLA_HEADS = 16
MLA_NOPE = 64
MLA_ROPE = 32
MLA_V = 64
MLA_Q_RANK = 384
MLA_KV_RANK = 256
ROPE_BASE = 10000.0
LRU_BLOCKS = 10
LRU_BLOCK = 128
LRU_WIDTH = LRU_BLOCKS * LRU_BLOCK
LRU_C = 8.0
LANES = 128
SUBLANES = 8
VMEM_LIMIT_BYTES = 56 * 1024 * 1024
NEG_BIG = -1e30


def _cparams(*sem):
    return pltpu.CompilerParams(dimension_semantics=sem, vmem_limit_bytes=VMEM_LIMIT_BYTES)


def _pick_tile(n, candidates):
    for c in candidates:
        if n % c == 0:
            return c
    raise ValueError(f"no tile in {candidates} divides {n}")


def _const_spec(shape):
    nd = len(shape)
    return pl.BlockSpec(shape, lambda *_: (0,) * nd)


def _rms(x, g):
    ms = jnp.mean(x * x, axis=-1, keepdims=True)
    return x * lax.rsqrt(ms + EPS) * g


def _sigmoid(x):
    return 1.0 / (1.0 + jnp.exp(-x))


def _softplus(x):
    return jnp.maximum(x, 0.0) + jnp.log(1.0 + jnp.exp(-jnp.abs(x)))


def _dot(a, b):
    return jnp.dot(a, b, preferred_element_type=F32)


def _dot_nt(a, b):
    return lax.dot_general(a, b, (((1,), (1,)), ((), ())), preferred_element_type=F32)


def _dot_tn(a, b):
    return lax.dot_general(a, b, (((0,), (0,)), ((), ())), preferred_element_type=F32)


def _shift_rows(cur, tail, k):
    if k == 0:
        return cur
    rolled = pltpu.roll(cur, k, 0)
    row = lax.broadcasted_iota(jnp.int32, tail.shape, 0)
    top = jnp.where(row < k, pltpu.roll(tail, k, 0), rolled[0:SUBLANES])
    return jnp.concatenate([top, rolled[SUBLANES:]], axis=0)


def _causal_conv(cur, tail, w_ref, b_ref):
    k = w_ref.shape[0]
    out = b_ref[...]
    for i in range(k):
        out = out + _shift_rows(cur, tail, k - 1 - i) * w_ref[i:i + 1, :]
    return out


def _proj_even_kernel(h_ref, g_ref, wz_ref, wxbc_ref, wdt_ref, wkrp_ref, wkrs_ref, wcq_ref, qg_ref,
                      wckv_ref, kvg_ref, wqp_ref, wqs_ref, wkp_ref, wv_ref, cq_ref, sq_ref, ck_ref, sk_ref,
                      z_ref, xbc_ref, dt_ref, q_ref, k_ref, v_ref):
    hn = _rms(h_ref[...], g_ref[...]).astype(BF16)
    z_ref[...] = _dot(hn, wz_ref[...]).astype(BF16)
    xbc_ref[...] = _dot(hn, wxbc_ref[...]).astype(BF16)
    dt_ref[...] = _dot(hn, wdt_ref[...])
    kr = _dot(hn, wkrp_ref[...]) * ck_ref[...] + _dot(hn, wkrs_ref[...]) * sk_ref[...]
    cqn = _rms(_dot(hn, wcq_ref[...]), qg_ref[...]).astype(BF16)
    ckvn = _rms(_dot(hn, wckv_ref[...]), kvg_ref[...]).astype(BF16)
    v_ref[...] = _dot(ckvn, wv_ref[...]).astype(BF16)
    cq_t = cq_ref[...]
    sq_t = sq_ref[...]
    for hd in range(MLA_HEADS):
        sl = slice(hd * LANES, (hd + 1) * LANES)
        qh = _dot(cqn, wqp_ref[:, sl]) * cq_t + _dot(cqn, wqs_ref[:, sl]) * sq_t
        q_ref[hd] = qh.astype(BF16)
        k_ref[hd] = (_dot(ckvn, wkp_ref[:, sl]) + kr).astype(BF16)


def _proj_even(h3, g, wts, tables):
    b, tp, d = h3.shape
    tm = _pick_tile(tp, (384, 128))
    nt = tp // tm
    n = b * tp
    row3 = lambda bi, i: (bi, i, 0)
    flat = lambda bi, i: (bi * nt + i, 0)
    tab = lambda bi, i: (i, 0)
    in_specs = [pl.BlockSpec((None, tm, d), row3), _const_spec(g.shape)]
    in_specs += [_const_spec(w.shape) for w in wts]
    in_specs += [pl.BlockSpec((tm, LANES), tab) for _ in tables]
    out_shape = (
        jax.ShapeDtypeStruct((n, SSD_D_INNER), BF16),
        jax.ShapeDtypeStruct((n, SSD_CONV_CH), BF16),
        jax.ShapeDtypeStruct((n, LANES), F32),
        jax.ShapeDtypeStruct((MLA_HEADS, n, LANES), BF16),
        jax.ShapeDtypeStruct((MLA_HEADS, n, LANES), BF16),
        jax.ShapeDtypeStruct((n, MLA_HEADS * MLA_V), BF16),
    )
    head3 = lambda bi, i: (0, bi * nt + i, 0)
    out_specs = (
        pl.BlockSpec((tm, SSD_D_INNER), flat),
        pl.BlockSpec((tm, SSD_CONV_CH), flat),
        pl.BlockSpec((tm, LANES), flat),
        pl.BlockSpec((MLA_HEADS, tm, LANES), head3),
        pl.BlockSpec((MLA_HEADS, tm, LANES), head3),
        pl.BlockSpec((tm, MLA_HEADS * MLA_V), flat),
    )
    return pl.pallas_call(
        _proj_even_kernel, grid=(b, nt), in_specs=in_specs, out_specs=out_specs, out_shape=out_shape,
        compiler_params=_cparams("parallel", "parallel"), name="proj_even",
    )(h3, g, *wts, *tables)


def _cumsum_rows(x):
    n = x.shape[0]
    row = lax.broadcasted_iota(jnp.int32, x.shape, 0)
    d = 1
    while d < n:
        x = x + jnp.where(row >= d, pltpu.roll(x, d, 0), 0.0)
        d *= 2
    return x


def _ssd_kernel(xbc_ref, dt_ref, z_ref, cw_ref, cb_ref, dtb_ref, alog_ref, dsk_ref, ng_ref, y_ref,
                tail_scr, state_scr, y_scr):
    c = pl.program_id(1)
    L = CHUNK
    heads_per_group = SSD_HEADS // SSD_GROUPS

    @pl.when(c == 0)
    def _():
        tail_scr[...] = jnp.zeros_like(tail_scr)
        state_scr[...] = jnp.zeros_like(state_scr)

    cur = xbc_ref[...].astype(F32)
    conv = _causal_conv(cur, tail_scr[...], cw_ref, cb_ref)
    tail_scr[...] = cur[L - SUBLANES:L]
    row = lax.broadcasted_iota(jnp.int32, (L, 1), 0)
    valid = (c * L + row) >= PAD
    act = jnp.where(valid, conv * _sigmoid(conv), 0.0)
    xs = act[:, :SSD_D_INNER]
    bm = act[:, SSD_D_INNER:SSD_D_INNER + SSD_BC]
    cm = act[:, SSD_D_INNER + SSD_BC:]

    dtv = jnp.where(valid, _softplus(dt_ref[...] + dtb_ref[...]), 0.0)
    a_cs = _cumsum_rows(dtv * (-jnp.exp(alog_ref[...])))
    a_cs_t = a_cs.T
    dt_t = dtv.T
    tot = a_cs[L - 1:L, :]
    w_l = jnp.exp(tot - a_cs) * dtv
    e_l = jnp.exp(a_cs)
    cd = jnp.exp(tot)
    causal = lax.broadcasted_iota(jnp.int32, (L, L), 0) >= lax.broadcasted_iota(jnp.int32, (L, L), 1)
    lo = lax.broadcasted_iota(jnp.int32, (L, LANES), 1) < SSD_HEAD_DIM

    for g in range(SSD_GROUPS):
        cc = cm[:, g * SSD_STATE:(g + 1) * SSD_STATE]
        bc = bm[:, g * SSD_STATE:(g + 1) * SSD_STATE]
        cb = _dot_nt(cc.astype(BF16), bc.astype(BF16))
        for hp in range(heads_per_group // 2):
            pair = g * (heads_per_group // 2) + hp
            sl = slice(pair * LANES, (pair + 1) * LANES)
            xs_pair = xs[:, sl]
            st_pair = state_scr[:, sl]
            xs_b = xs_pair.astype(BF16)
            rhs = jnp.concatenate([xs_b, st_pair.astype(BF16)], axis=0)
            ys, sts, cds = [], [], []
            for hh in range(2):
                h = 2 * pair + hh
                seg = a_cs[:, h:h + 1] - a_cs_t[h:h + 1, :]
                dec = jnp.exp(jnp.where(causal, seg, -jnp.inf))
                m = cb * dec * dt_t[h:h + 1, :]
                lhs = jnp.concatenate([m, cc * e_l[:, h:h + 1]], axis=1).astype(BF16)
                ys.append(_dot(lhs, rhs))
                bcw = (bc * w_l[:, h:h + 1]).astype(BF16)
                sts.append(_dot_tn(bcw, xs_b))
                cds.append(cd[:, h:h + 1])
            y_scr[:, sl] = jnp.where(lo, ys[0], ys[1]) + dsk_ref[:, sl] * xs_pair
            state_scr[:, sl] = jnp.where(lo, cds[0], cds[1]) * st_pair + jnp.where(lo, sts[0], sts[1])

    zf = z_ref[...].astype(F32)
    y_ref[...] = _rms(y_scr[...] * (zf * _sigmoid(zf)), ng_ref[...]).astype(BF16)


def _ssd(xbc3, dt3, z3, cw, cb, dtb, alog, dsk, ng):
    b, tp, _ = xbc3.shape
    nc = tp // CHUNK
    row3 = lambda bi, ci: (bi, ci, 0)
    in_specs = [
        pl.BlockSpec((None, CHUNK, SSD_CONV_CH), row3),
        pl.BlockSpec((None, CHUNK, LANES), row3),
        pl.BlockSpec((None, CHUNK, SSD_D_INNER), row3),
    ] + [_const_spec(a.shape) for a in (cw, cb, dtb, alog, dsk, ng)]
    return pl.pallas_call(
        _ssd_kernel, grid=(b, nc), in_specs=in_specs,
        out_specs=pl.BlockSpec((None, CHUNK, SSD_D_INNER), row3),
        out_shape=jax.ShapeDtypeStruct((b, tp, SSD_D_INNER), BF16),
        scratch_shapes=[pltpu.VMEM((SUBLANES, SSD_CONV_CH), F32),
                        pltpu.VMEM((SSD_STATE, SSD_D_INNER), F32),
                        pltpu.VMEM((CHUNK, SSD_D_INNER), F32)],
        compiler_params=_cparams("parallel", "arbitrary"), name="ssd",
    )(xbc3, dt3, z3, cw, cb, dtb, alog, dsk, ng)


ATTN_KV_BLOCK = 256


def _attn_kernel(q_ref, k_ref, v_ref, o_ref, vext_scr, p_scr, alpha_scr, m_scr, acc_scr, *, tq):
    kb = ATTN_KV_BLOCK
    qi = pl.program_id(2)
    q0 = qi * tq
    end = q0 + tq
    n_int = q0 // kb
    n_blocks = (end + kb - 1) // kb
    last_off = end - kb

    @pl.when(qi == 0)
    def _():
        vext_scr[:, :LANES] = v_ref[...]
        vext_scr[:, LANES:] = jnp.ones((vext_scr.shape[0], LANES), BF16)

    m_scr[...] = jnp.full_like(m_scr, NEG_BIG)
    acc_scr[...] = jnp.zeros_like(acc_scr)

    def offset(bk):
        return pl.multiple_of(jnp.minimum(bk * kb, last_off), CHUNK)

    def apply_v(bk):
        vb = vext_scr[pl.ds(offset(bk), kb), :]
        for hh in range(2):
            acc_scr[hh] = pltpu.repeat(alpha_scr[hh], 2, axis=1) * acc_scr[hh] + _dot(p_scr[hh], vb)

    def scores(bk, masked):
        off = offset(bk)
        if masked:
            col = lax.broadcasted_iota(jnp.int32, (tq, kb), 1)
            keep = (col - lax.broadcasted_iota(jnp.int32, (tq, kb), 0)) <= (q0 - off)
            first_col = jnp.where(bk == 0, PAD, bk * kb - off)
        for hh in range(2):
            s = _dot_nt(q_ref[hh], k_ref[hh, pl.ds(off, kb), :])
            if masked:
                s = jnp.where(keep, s, NEG_BIG)
                s = jnp.where(col >= first_col, s, NEG_BIG)
            m_prev = m_scr[hh]
            m_new = jnp.maximum(m_prev, jnp.max(s, axis=-1, keepdims=True))
            alpha_scr[hh] = jnp.exp(m_prev - m_new)
            p_scr[hh] = jnp.exp(s - pltpu.repeat(m_new, kb // LANES, axis=1)).astype(BF16)
            m_scr[hh] = m_new

    def plain_step(bk, carry):
        apply_v(bk - 1)
        scores(bk, False)
        return carry

    def masked_step(bk, carry):
        apply_v(bk - 1)
        scores(bk, True)
        return carry

    scores(0, True)
    lax.fori_loop(1, n_int, plain_step, 0)
    lax.fori_loop(jnp.maximum(n_int, 1), n_blocks, masked_step, 0)
    apply_v(n_blocks - 1)

    lo = lax.broadcasted_iota(jnp.int32, (tq, LANES), 1) < MLA_V
    a0 = acc_scr[0]
    a1 = acc_scr[1]
    out = jnp.where(lo, a0[:, :LANES] / a0[:, LANES:], a1[:, :LANES] / a1[:, LANES:])
    row = q0 + lax.broadcasted_iota(jnp.int32, (tq, 1), 0)
    o_ref[...] = jnp.where(row >= PAD, out, 0.0).astype(BF16)


def _attention(q, k, v, b, tp):
    tq = _pick_tile(tp, (384,))
    nq = tp // tq
    n = b * tp
    in_specs = [
        pl.BlockSpec((2, tq, LANES), lambda bi, hp, qi: (hp, bi * nq + qi, 0)),
        pl.BlockSpec((2, tp, LANES), lambda bi, hp, qi: (hp, bi, 0)),
        pl.BlockSpec((tp, LANES), lambda bi, hp, qi: (bi, hp)),
    ]
    return pl.pallas_call(
        functools.partial(_attn_kernel, tq=tq), grid=(b, MLA_HEADS // 2, nq), in_specs=in_specs,
        out_specs=pl.BlockSpec((tq, LANES), lambda bi, hp, qi: (bi * nq + qi, hp)),
        out_shape=jax.ShapeDtypeStruct((n, MLA_HEADS * MLA_V), BF16),
        scratch_shapes=[pltpu.VMEM((tp, 2 * LANES), BF16), pltpu.VMEM((2, tq, ATTN_KV_BLOCK), BF16),
                        pltpu.VMEM((2, tq, LANES), F32), pltpu.VMEM((2, tq, LANES), F32),
                        pltpu.VMEM((2, tq, 2 * LANES), F32)],
        compiler_params=_cparams("parallel", "parallel", "arbitrary"), name="attention",
    )(q, k, v)


def _post_kernel(*refs, n_in):
    h_ref = refs[0]
    a_refs = refs[1:1 + n_in]
    w_refs = refs[1 + n_in:1 + 2 * n_in]
    g_ref = refs[1 + 2 * n_in]
    o_ref = refs[2 + 2 * n_in]
    m = _dot(a_refs[0][...], w_refs[0][...])
    for a_ref, w_ref in zip(a_refs[1:], w_refs[1:]):
        m = m + _dot(a_ref[...], w_ref[...])
    o_ref[...] = h_ref[...] + _rms(m, g_ref[...])


def _post(h2, acts, wts, g):
    n, d = h2.shape
    tm = _pick_tile(n, (512, 384, 256, 128))
    rows = lambda i: (i, 0)
    in_specs = [pl.BlockSpec((tm, d), rows)]
    in_specs += [pl.BlockSpec((tm, a.shape[1]), rows) for a in acts]
    in_specs += [_const_spec(w.shape) for w in wts] + [_const_spec(g.shape)]
    return pl.pallas_call(
        functools.partial(_post_kernel, n_in=len(acts)), grid=(n // tm,), in_specs=in_specs,
        out_specs=pl.BlockSpec((tm, d), rows), out_shape=jax.ShapeDtypeStruct((n, d), F32),
        compiler_params=_cparams("parallel"), name="mixer_out",
    )(h2, *acts, *wts, g)


def _mlp_kernel(h_ref, g1_ref, wup_ref, wdn_ref, g2_ref, o_ref, *, ff_chunk):
    x = h_ref[...]
    hn = _rms(x, g1_ref[...]).astype(BF16)
    d_ff = wup_ref.shape[1]
    y = None
    for c0 in range(0, d_ff, ff_chunk):
        u = jnp.maximum(_dot(hn, wup_ref[:, c0:c0 + ff_chunk]), 0.0)
        part = _dot((u * u).astype(BF16), wdn_ref[c0:c0 + ff_chunk, :])
        y = part if y is None else y + part
    o_ref[...] = x + _rms(y, g2_ref[...])


def _mlp(h2, g1, wup, wdn, g2):
    n, d = h2.shape
    tm = _pick_tile(n, (256, 128))
    rows = lambda i: (i, 0)
    in_specs = [pl.BlockSpec((tm, d), rows), _const_spec(g1.shape), _const_spec(wup.shape),
                _const_spec(wdn.shape), _const_spec(g2.shape)]
    return pl.pallas_call(
        functools.partial(_mlp_kernel, ff_chunk=1024), grid=(n // tm,), in_specs=in_specs,
        out_specs=pl.BlockSpec((tm, d), rows), out_shape=jax.ShapeDtypeStruct((n, d), F32),
        compiler_params=_cparams("parallel"), name="mlp",
    )(h2, g1, wup, wdn, g2)


def _proj_odd_kernel(h_ref, g_ref, wx_ref, wy_ref, x_ref, gate_ref):
    hn = _rms(h_ref[...], g_ref[...]).astype(BF16)
    x_ref[...] = _dot(hn, wx_ref[...]).astype(BF16)
    y = _dot(hn, wy_ref[...])
    gelu = 0.5 * y * (1.0 + jnp.tanh(np.sqrt(2.0 / np.pi).astype(np.float32) * (y + 0.044715 * (y * y * y))))
    gate_ref[...] = gelu.astype(BF16)


def _proj_odd(h2, g, wx, wy):
    n, d = h2.shape
    tm = _pick_tile(n, (512, 384, 256, 128))
    rows = lambda i: (i, 0)
    w = wx.shape[1]
    return pl.pallas_call(
        _proj_odd_kernel, grid=(n // tm,),
        in_specs=[pl.BlockSpec((tm, d), rows), _const_spec(g.shape), _const_spec(wx.shape), _const_spec(wy.shape)],
        out_specs=(pl.BlockSpec((tm, w), rows), pl.BlockSpec((tm, w), rows)),
        out_shape=(jax.ShapeDtypeStruct((n, w), BF16), jax.ShapeDtypeStruct((n, w), BF16)),
        compiler_params=_cparams("parallel"), name="proj_odd",
    )(h2, g, wx, wy)


def _rglru_kernel(x_ref, gate_ref, cw_ref, cb_ref, wa_ref, ba_ref, wi_ref, bi_ref, lam_ref, o_ref,
                  tail_scr, h_scr):
    t = pl.program_id(1)
    L = x_ref.shape[0]

    @pl.when(t == 0)
    def _():
        tail_scr[...] = jnp.zeros_like(tail_scr)
        h_scr[...] = jnp.zeros_like(h_scr)

    cur = x_ref[...].astype(F32)
    xr = _causal_conv(cur, tail_scr[...], cw_ref, cb_ref)
    tail_scr[...] = cur[L - SUBLANES:L]
    rs, gs = [], []
    for nb in range(LRU_BLOCKS):
        sl = slice(nb * LRU_BLOCK, (nb + 1) * LRU_BLOCK)
        xb = xr[:, sl].astype(BF16)
        rs.append(_dot(xb, wa_ref[nb]))
        gs.append(_dot(xb, wi_ref[nb]))
    r = _sigmoid(jnp.concatenate(rs, axis=1) + ba_ref[...])
    ig = _sigmoid(jnp.concatenate(gs, axis=1) + bi_ref[...])
    log_a = (-LRU_C) * r * _softplus(-lam_ref[...])
    a = jnp.exp(log_a)
    row = lax.broadcasted_iota(jnp.int32, (L, 1), 0)
    valid = (t * L + row) >= PAD
    u = jnp.where(valid, jnp.sqrt(1.0 - a * a) * (ig * xr), 0.0)

    rowf = lax.broadcasted_iota(jnp.int32, (L, LRU_WIDTH), 0)
    d = 1
    while d < L:
        keep = rowf >= d
        a_s = jnp.where(keep, pltpu.roll(a, d, 0), 1.0)
        u_s = jnp.where(keep, pltpu.roll(u, d, 0), 0.0)
        u = a * u_s + u
        a = a * a_s
        d *= 2
    hs = u + a * h_scr[0:1, :]
    h_scr[...] = jnp.broadcast_to(hs[L - 1:L, :], h_scr.shape)
    o_ref[...] = (hs * gate_ref[...].astype(F32)).astype(BF16)


def _rglru(x3, gate3, cw, cb, wa, ba, wi, bi, lam):
    b, tp, w = x3.shape
    tt = CHUNK
    row3 = lambda bi_, ti: (bi_, ti, 0)
    in_specs = [pl.BlockSpec((None, tt, w), row3), pl.BlockSpec((None, tt, w), row3)]
    in_specs += [_const_spec(a.shape) for a in (cw, cb, wa, ba, wi, bi, lam)]
    return pl.pallas_call(
        _rglru_kernel, grid=(b, tp // tt), in_specs=in_specs,
        out_specs=pl.BlockSpec((None, tt, w), row3),
        out_shape=jax.ShapeDtypeStruct((b, tp, w), BF16),
        scratch_shapes=[pltpu.VMEM((SUBLANES, w), F32), pltpu.VMEM((SUBLANES, w), F32)],
        compiler_params=_cparams("parallel", "arbitrary"), name="rglru",
    )(x3, gate3, cw, cb, wa, ba, wi, bi, lam)


def _rope_tables(tp):
    half = MLA_ROPE // 2
    inv = ROPE_BASE ** (-jnp.arange(0, MLA_ROPE, 2, dtype=F32) / MLA_ROPE)
    pos = jnp.maximum(jnp.arange(tp, dtype=F32) - PAD, 0.0)
    ang = pos[:, None] * inv[None, :]
    cos, sin = jnp.cos(ang), jnp.sin(ang)
    zeros_lo = jnp.zeros((tp, MLA_NOPE), F32)
    zeros_hi = jnp.zeros((tp, LANES - MLA_NOPE - MLA_ROPE), F32)
    ck = jnp.concatenate([zeros_lo, cos, cos, zeros_hi], axis=1)
    sk = jnp.concatenate([zeros_lo, sin, sin, zeros_hi], axis=1)
    scale = (MLA_NOPE + MLA_ROPE) ** -0.5
    cq = jnp.concatenate([jnp.ones((tp, MLA_NOPE), F32), cos, cos, zeros_hi], axis=1) * scale
    sq = sk * scale
    del half
    return cq, sq, ck, sk


def _even_weights(w_in, w_q_up, w_kv_up):
    d = w_in.shape[0]
    offs = np.cumsum((SSD_D_INNER, SSD_CONV_CH, SSD_HEADS, MLA_Q_RANK, MLA_KV_RANK, MLA_ROPE))
    wz = w_in[:, :offs[0]]
    wxbc = w_in[:, offs[0]:offs[1]]
    wdt = w_in[:, offs[1]:offs[2]]
    wcq = w_in[:, offs[2]:offs[3]]
    wckv = w_in[:, offs[3]:offs[4]]
    wkr = w_in[:, offs[4]:offs[5]]
    half = MLA_ROPE // 2
    hi_pad = LANES - MLA_NOPE - MLA_ROPE
    wdt_p = jnp.pad(wdt, ((0, 0), (0, LANES - SSD_HEADS)))
    wkr_p = jnp.pad(wkr, ((0, 0), (MLA_NOPE, hi_pad)))
    wkr_s = jnp.pad(jnp.concatenate([-wkr[:, half:], wkr[:, :half]], axis=1), ((0, 0), (MLA_NOPE, hi_pad)))
    wq = w_q_up.reshape(MLA_Q_RANK, MLA_HEADS, MLA_NOPE + MLA_ROPE)
    q_nope, q_rope = wq[..., :MLA_NOPE], wq[..., MLA_NOPE:]
    wq_p = jnp.pad(wq, ((0, 0), (0, 0), (0, hi_pad))).reshape(MLA_Q_RANK, MLA_HEADS * LANES)
    q_swap = jnp.concatenate([jnp.zeros_like(q_nope), -q_rope[..., half:], q_rope[..., :half]], axis=-1)
    wq_s = jnp.pad(q_swap, ((0, 0), (0, 0), (0, hi_pad))).reshape(MLA_Q_RANK, MLA_HEADS * LANES)
    wkv = w_kv_up.reshape(MLA_KV_RANK, MLA_HEADS, MLA_NOPE + MLA_V)
    wk_p = jnp.pad(wkv[..., :MLA_NOPE], ((0, 0), (0, 0), (0, LANES - MLA_NOPE))).reshape(
        MLA_KV_RANK, MLA_HEADS * LANES)
    wv = wkv[..., MLA_NOPE:].reshape(MLA_KV_RANK, MLA_HEADS * MLA_V)
    del d
    bf = lambda a: a.astype(BF16)
    return dict(wz=bf(wz), wxbc=bf(wxbc), wdt=bf(wdt_p), wkr_p=bf(wkr_p), wkr_s=bf(wkr_s), wcq=bf(wcq),
                wckv=bf(wckv), wq_p=bf(wq_p), wq_s=bf(wq_s), wk_p=bf(wk_p), wv=bf(wv))


def _row(a):
    return a.reshape(1, -1).astype(F32)


def kernel(x, meta_tokens, mix_pre_g, mix_post_g, mlp_pre_g, mlp_post_g, w_up, w_down, w_in, ssd_conv_w,
           ssd_conv_b, ssd_dt_bias, ssd_a_log, ssd_d, ssd_norm_g, mla_q_norm_g, mla_w_q_up, mla_kv_norm_g,
           mla_w_kv_up, w_out_ab, rg_w_x, rg_w_y, rg_conv_w, rg_conv_b, rg_w_a, rg_b_a, rg_w_i, rg_b_i,
           rg_lambda, rg_w_out):
    b, seq, d = x.shape
    depth = mix_pre_g.shape[0]
    tp = PAD + N_META + seq
    n = b * tp
    meta = jnp.broadcast_to(meta_tokens[None].astype(x.dtype), (b, N_META, d))
    h = jnp.concatenate([jnp.zeros((b, PAD, d), x.dtype), meta, x], axis=1).reshape(n, d)
    tables = _rope_tables(tp)

    for layer in range(depth):
        if layer % 2 == 0:
            e = layer // 2
            w = _even_weights(w_in[e], mla_w_q_up[e], mla_w_kv_up[e])
            wts = (w["wz"], w["wxbc"], w["wdt"], w["wkr_p"], w["wkr_s"], w["wcq"], _row(mla_q_norm_g[e]),
                   w["wckv"], _row(mla_kv_norm_g[e]), w["wq_p"], w["wq_s"], w["wk_p"], w["wv"])
            z, xbc, dt, q, k, v = _proj_even(h.reshape(b, tp, d), _row(mix_pre_g[layer]), wts, tables)
            lane_pad = lambda a: jnp.pad(_row(a), ((0, 0), (0, LANES - SSD_HEADS)))
            y_ssd = _ssd(xbc.reshape(b, tp, SSD_CONV_CH), dt.reshape(b, tp, LANES), z.reshape(b, tp, SSD_D_INNER),
                         ssd_conv_w[e].astype(F32), _row(ssd_conv_b[e]), lane_pad(ssd_dt_bias[e]),
                         lane_pad(ssd_a_log[e]), _row(jnp.repeat(ssd_d[e], SSD_HEAD_DIM)), _row(ssd_norm_g[e]))
            y_att = _attention(q, k, v, b, tp)
            wo = w_out_ab[e].astype(BF16)
            h = _post(h, (y_ssd.reshape(n, SSD_D_INNER), y_att), (wo[:SSD_D_INNER], wo[SSD_D_INNER:]),
                      _row(mix_post_g[layer]))
        else:
            o = layer // 2
            xr, gate = _proj_odd(h, _row(mix_pre_g[layer]), rg_w_x[o].astype(BF16), rg_w_y[o].astype(BF16))
            y = _rglru(xr.reshape(b, tp, LRU_WIDTH), gate.reshape(b, tp, LRU_WIDTH), rg_conv_w[o].astype(F32),
                       _row(rg_conv_b[o]), rg_w_a[o].astype(BF16), _row(rg_b_a[o]), rg_w_i[o].astype(BF16),
                       _row(rg_b_i[o]), _row(rg_lambda[o]))
            h = _post(h, (y.reshape(n, LRU_WIDTH),), (rg_w_out[o].astype(BF16),), _row(mix_post_g[layer]))
        h = _mlp(h, _row(mlp_pre_g[layer]), w_up[layer].astype(BF16), w_down[layer].astype(BF16),
                 _row(mlp_post_g[layer]))
    return h.reshape(b, tp, d)[:, PAD + N_META:]
```

```python
import functools

import jax
import jax.numpy as jnp
import numpy as np
from jax import lax
from jax.experimental import pallas as pl
from jax.experimental.pallas import tpu as pltpu

F32 = jnp.float32
BF16 = jnp.bfloat16

N_META = 16
CHUNK = 128
PAD = CHUNK - N_META
EPS = 1e-6
SSD_HEADS = 16
SSD_HEAD_DIM = 64
SSD_D_INNER = SSD_HEADS * SSD_HEAD_DIM
SSD_GROUPS = 2
SSD_STATE = 128
SSD_CONV = 4
SSD_BC = SSD_GROUPS * SSD_STATE
SSD_CONV_CH = SSD_D_INNER + 2 * SSD_BC
MLA_HEADS = 16
MLA_NOPE = 64
MLA_ROPE = 32
MLA_V = 64
MLA_Q_RANK = 384
MLA_KV_RANK = 256
ROPE_BASE = 10000.0
LRU_BLOCKS = 10
LRU_BLOCK = 128
LRU_WIDTH = LRU_BLOCKS * LRU_BLOCK
LRU_C = 8.0
LANES = 128
SUBLANES = 8
VMEM_LIMIT_BYTES = 56 * 1024 * 1024
NEG_BIG = -1e30


def _cparams(*sem):
    return pltpu.CompilerParams(dimension_semantics=sem, vmem_limit_bytes=VMEM_LIMIT_BYTES)


def _pick_tile(n, candidates):
    for c in candidates:
        if n % c == 0:
            return c
    raise ValueError(f"no tile in {candidates} divides {n}")


def _const_spec(shape):
    nd = len(shape)
    return pl.BlockSpec(shape, lambda *_: (0,) * nd)


def _rms(x, g):
    ms = jnp.mean(x * x, axis=-1, keepdims=True)
    return x * lax.rsqrt(ms + EPS) * g


def _sigmoid(x):
    return 1.0 / (1.0 + jnp.exp(-x))


def _softplus(x):
    return jnp.maximum(x, 0.0) + jnp.log(1.0 + jnp.exp(-jnp.abs(x)))


def _dot(a, b):
    return jnp.dot(a, b, preferred_element_type=F32)


def _dot_nt(a, b):
    return lax.dot_general(a, b, (((1,), (1,)), ((), ())), preferred_element_type=F32)


def _dot_tn(a, b):
    return lax.dot_general(a, b, (((0,), (0,)), ((), ())), preferred_element_type=F32)


def _shift_rows(cur, tail, k):
    if k == 0:
        return cur
    rolled = pltpu.roll(cur, k, 0)
    row = lax.broadcasted_iota(jnp.int32, tail.shape, 0)
    top = jnp.where(row < k, pltpu.roll(tail, k, 0), rolled[0:SUBLANES])
    return jnp.concatenate([top, rolled[SUBLANES:]], axis=0)


def _causal_conv(cur, tail, w_ref, b_ref):
    k = w_ref.shape[0]
    out = b_ref[...]
    for i in range(k):
        out = out + _shift_rows(cur, tail, k - 1 - i) * w_ref[i:i + 1, :]
    return out


def _proj_even_kernel(h_ref, g_ref, wz_ref, wxbc_ref, wdt_ref, wkrp_ref, wkrs_ref, wcq_ref, qg_ref,
                      wckv_ref, kvg_ref, wqp_ref, wqs_ref, wkp_ref, wv_ref, cq_ref, sq_ref, ck_ref, sk_ref,
                      z_ref, xbc_ref, dt_ref, q_ref, k_ref, v_ref):
    hn = _rms(h_ref[...], g_ref[...]).astype(BF16)
    z_ref[...] = _dot(hn, wz_ref[...]).astype(BF16)
    xbc_ref[...] = _dot(hn, wxbc_ref[...]).astype(BF16)
    dt_ref[...] = _dot(hn, wdt_ref[...])
    kr = _dot(hn, wkrp_ref[...]) * ck_ref[...] + _dot(hn, wkrs_ref[...]) * sk_ref[...]
    cqn = _rms(_dot(hn, wcq_ref[...]), qg_ref[...]).astype(BF16)
    ckvn = _rms(_dot(hn, wckv_ref[...]), kvg_ref[...]).astype(BF16)
    v_ref[...] = _dot(ckvn, wv_ref[...]).astype(BF16)
    cq_t = cq_ref[...]
    sq_t = sq_ref[...]
    for hd in range(MLA_HEADS):
        sl = slice(hd * LANES, (hd + 1) * LANES)
        qh = _dot(cqn, wqp_ref[:, sl]) * cq_t + _dot(cqn, wqs_ref[:, sl]) * sq_t
        q_ref[hd] = qh.astype(BF16)
        k_ref[hd] = (_dot(ckvn, wkp_ref[:, sl]) + kr).astype(BF16)


def _proj_even(h3, g, wts, tables):
    b, tp, d = h3.shape
    tm = _pick_tile(tp, (384, 128))
    nt = tp // tm
    n = b * tp
    row3 = lambda bi, i: (bi, i, 0)
    flat = lambda bi, i: (bi * nt + i, 0)
    tab = lambda bi, i: (i, 0)
    in_specs = [pl.BlockSpec((None, tm, d), row3), _const_spec(g.shape)]
    in_specs += [_const_spec(w.shape) for w in wts]
    in_specs += [pl.BlockSpec((tm, LANES), tab) for _ in tables]
    out_shape = (
        jax.ShapeDtypeStruct((n, SSD_D_INNER), BF16),
        jax.ShapeDtypeStruct((n, SSD_CONV_CH), BF16),
        jax.ShapeDtypeStruct((n, LANES), F32),
        jax.ShapeDtypeStruct((MLA_HEADS, n, LANES), BF16),
        jax.ShapeDtypeStruct((MLA_HEADS, n, LANES), BF16),
        jax.ShapeDtypeStruct((n, MLA_HEADS * MLA_V), BF16),
    )
    head3 = lambda bi, i: (0, bi * nt + i, 0)
    out_specs = (
        pl.BlockSpec((tm, SSD_D_INNER), flat),
        pl.BlockSpec((tm, SSD_CONV_CH), flat),
        pl.BlockSpec((tm, LANES), flat),
        pl.BlockSpec((MLA_HEADS, tm, LANES), head3),
        pl.BlockSpec((MLA_HEADS, tm, LANES), head3),
        pl.BlockSpec((tm, MLA_HEADS * MLA_V), flat),
    )
    return pl.pallas_call(
        _proj_even_kernel, grid=(b, nt), in_specs=in_specs, out_specs=out_specs, out_shape=out_shape,
        compiler_params=_cparams("parallel", "parallel"), name="proj_even",
    )(h3, g, *wts, *tables)


def _cumsum_rows(x):
    n = x.shape[0]
    row = lax.broadcasted_iota(jnp.int32, x.shape, 0)
    d = 1
    while d < n:
        x = x + jnp.where(row >= d, pltpu.roll(x, d, 0), 0.0)
        d *= 2
    return x


def _ssd_kernel(xbc_ref, dt_ref, z_ref, cw_ref, cb_ref, dtb_ref, alog_ref, dsk_ref, ng_ref, y_ref,
                tail_scr, state_scr, y_scr):
    c = pl.program_id(1)
    L = CHUNK
    heads_per_group = SSD_HEADS // SSD_GROUPS

    @pl.when(c == 0)
    def _():
        tail_scr[...] = jnp.zeros_like(tail_scr)
        state_scr[...] = jnp.zeros_like(state_scr)

    cur = xbc_ref[...].astype(F32)
    conv = _causal_conv(cur, tail_scr[...], cw_ref, cb_ref)
    tail_scr[...] = cur[L - SUBLANES:L]
    row = lax.broadcasted_iota(jnp.int32, (L, 1), 0)
    valid = (c * L + row) >= PAD
    act = jnp.where(valid, conv * _sigmoid(conv), 0.0)
    xs = act[:, :SSD_D_INNER]
    bm = act[:, SSD_D_INNER:SSD_D_INNER + SSD_BC]
    cm = act[:, SSD_D_INNER + SSD_BC:]

    dtv = jnp.where(valid, _softplus(dt_ref[...] + dtb_ref[...]), 0.0)
    a_cs = _cumsum_rows(dtv * (-jnp.exp(alog_ref[...])))
    a_cs_t = a_cs.T
    dt_t = dtv.T
    tot = a_cs[L - 1:L, :]
    w_l = jnp.exp(tot - a_cs) * dtv
    e_l = jnp.exp(a_cs)
    cd = jnp.exp(tot)
    causal = lax.broadcasted_iota(jnp.int32, (L, L), 0) >= lax.broadcasted_iota(jnp.int32, (L, L), 1)
    lo = lax.broadcasted_iota(jnp.int32, (L, LANES), 1) < SSD_HEAD_DIM

    for g in range(SSD_GROUPS):
        cc = cm[:, g * SSD_STATE:(g + 1) * SSD_STATE]
        bc = bm[:, g * SSD_STATE:(g + 1) * SSD_STATE]
        cb = _dot_nt(cc.astype(BF16), bc.astype(BF16))
        for hp in range(heads_per_group // 2):
            pair = g * (heads_per_group // 2) + hp
            sl = slice(pair * LANES, (pair + 1) * LANES)
            xs_pair = xs[:, sl]
            st_pair = state_scr[:, sl]
            xs_b = xs_pair.astype(BF16)
            rhs = jnp.concatenate([xs_b, st_pair.astype(BF16)], axis=0)
            ys, sts, cds = [], [], []
            for hh in range(2):
                h = 2 * pair + hh
                seg = a_cs[:, h:h + 1] - a_cs_t[h:h + 1, :]
                dec = jnp.exp(jnp.where(causal, seg, -jnp.inf))
                m = cb * dec * dt_t[h:h + 1, :]
                lhs = jnp.concatenate([m, cc * e_l[:, h:h + 1]], axis=1).astype(BF16)
                ys.append(_dot(lhs, rhs))
                bcw = (bc * w_l[:, h:h + 1]).astype(BF16)
                sts.append(_dot_tn(bcw, xs_b))
                cds.append(cd[:, h:h + 1])
            y_scr[:, sl] = jnp.where(lo, ys[0], ys[1]) + dsk_ref[:, sl] * xs_pair
            state_scr[:, sl] = jnp.where(lo, cds[0], cds[1]) * st_pair + jnp.where(lo, sts[0], sts[1])

    zf = z_ref[...].astype(F32)
    y_ref[...] = _rms(y_scr[...] * (zf * _sigmoid(zf)), ng_ref[...]).astype(BF16)


def _ssd(xbc3, dt3, z3, cw, cb, dtb, alog, dsk, ng):
    b, tp, _ = xbc3.shape
    nc = tp // CHUNK
    row3 = lambda bi, ci: (bi, ci, 0)
    in_specs = [
        pl.BlockSpec((None, CHUNK, SSD_CONV_CH), row3),
        pl.BlockSpec((None, CHUNK, LANES), row3),
        pl.BlockSpec((None, CHUNK, SSD_D_INNER), row3),
    ] + [_const_spec(a.shape) for a in (cw, cb, dtb, alog, dsk, ng)]
    return pl.pallas_call(
        _ssd_kernel, grid=(b, nc), in_specs=in_specs,
        out_specs=pl.BlockSpec((None, CHUNK, SSD_D_INNER), row3),
        out_shape=jax.ShapeDtypeStruct((b, tp, SSD_D_INNER), BF16),
        scratch_shapes=[pltpu.VMEM((SUBLANES, SSD_CONV_CH), F32),
                        pltpu.VMEM((SSD_STATE, SSD_D_INNER), F32),
                        pltpu.VMEM((CHUNK, SSD_D_INNER), F32)],
        compiler_params=_cparams("parallel", "arbitrary"), name="ssd",
    )(xbc3, dt3, z3, cw, cb, dtb, alog, dsk, ng)


ATTN_KV_BLOCK = 256
ATTN_V_ROWS = MLA_V + 16


def _attn_kernel(q_ref, k_ref, v_ref, o_ref, vt_scr, s_scr, bmax_scr, p_scr, alpha_scr, m_scr, acc_scr, *, tq):
    kb = ATTN_KV_BLOCK
    qi = pl.program_id(2)
    q0 = qi * tq
    end = q0 + tq
    n_int = q0 // kb
    n_blocks = (end + kb - 1) // kb
    last_off = end - kb

    @pl.when(qi == 0)
    def _():
        ones = jnp.ones((ATTN_V_ROWS - MLA_V, CHUNK), BF16)
        for c in range(vt_scr.shape[0]):
            vt = v_ref[c * CHUNK:(c + 1) * CHUNK, :].astype(F32).T.astype(BF16)
            for hh in range(2):
                vt_scr[c, hh, :MLA_V, :] = vt[hh * MLA_V:(hh + 1) * MLA_V, :]
                vt_scr[c, hh, MLA_V:, :] = ones

    m_scr[...] = jnp.full_like(m_scr, NEG_BIG)
    acc_scr[...] = jnp.zeros_like(acc_scr)

    def offset(bk):
        return pl.multiple_of(jnp.minimum(bk * kb, last_off), CHUNK)

    def qk(hh, bk, masked):
        off = offset(bk)
        s = _dot_nt(k_ref[hh, pl.ds(off, kb), :], q_ref[hh])
        if masked:
            key = lax.broadcasted_iota(jnp.int32, (kb, tq), 0)
            keep = (key - lax.broadcasted_iota(jnp.int32, (kb, tq), 1)) <= (q0 - off)
            first_key = jnp.where(bk == 0, PAD, bk * kb - off)
            s = jnp.where(keep, s, NEG_BIG)
            s = jnp.where(key >= first_key, s, NEG_BIG)
        s_scr[hh] = s
        bmax_scr[hh] = jnp.broadcast_to(jnp.max(s, axis=0, keepdims=True), (SUBLANES, tq))

    def softmax(hh):
        m_prev = m_scr[hh, 0:1, :]
        m_new = jnp.maximum(m_prev, bmax_scr[hh, 0:1, :])
        alpha_scr[hh] = jnp.broadcast_to(jnp.exp2(m_prev - m_new), (SUBLANES, tq))
        m_scr[hh] = jnp.broadcast_to(m_new, (SUBLANES, tq))
        p_scr[hh] = jnp.exp2(s_scr[hh] - m_new).astype(BF16)

    def pv(hh, bk):
        c0 = offset(bk) // CHUNK
        vt = jnp.concatenate([vt_scr[c0 + c, hh] for c in range(kb // CHUNK)], axis=1)
        acc_scr[hh] = alpha_scr[hh, 0:1, :] * acc_scr[hh] + _dot(vt, p_scr[hh])

    def step(bk, masked):
        softmax(1)
        qk(0, bk, masked)
        pv(0, bk - 1)
        softmax(0)
        qk(1, bk, masked)
        pv(1, bk - 1)

    def plain_step(bk, carry):
        step(bk, False)
        return carry

    def masked_step(bk, carry):
        step(bk, True)
        return carry

    qk(0, 0, True)
    softmax(0)
    qk(1, 0, True)
    lax.fori_loop(1, n_int, plain_step, 0)
    lax.fori_loop(jnp.maximum(n_int, 1), n_blocks, masked_step, 0)
    softmax(1)
    pv(0, n_blocks - 1)
    pv(1, n_blocks - 1)

    out_t = jnp.concatenate([acc_scr[hh, :MLA_V, :] / acc_scr[hh, MLA_V:MLA_V + 1, :] for hh in range(2)],
                            axis=0)
    qpos = q0 + lax.broadcasted_iota(jnp.int32, (1, tq), 1)
    out_t = jnp.where(qpos >= PAD, out_t, 0.0)
    for c in range(tq // CHUNK):
        o_ref[c * CHUNK:(c + 1) * CHUNK, :] = out_t[:, c * CHUNK:(c + 1) * CHUNK].T.astype(BF16)


def _attention(q, k, v, b, tp):
    tq = _pick_tile(tp, (1408, 384, 256))
    nq = tp // tq
    n = b * tp
    in_specs = [
        pl.BlockSpec((2, tq, LANES), lambda bi, hp, qi: (hp, bi * nq + qi, 0)),
        pl.BlockSpec((2, tp, LANES), lambda bi, hp, qi: (hp, bi, 0)),
        pl.BlockSpec((tp, LANES), lambda bi, hp, qi: (bi, hp)),
    ]
    return pl.pallas_call(
        functools.partial(_attn_kernel, tq=tq), grid=(b, MLA_HEADS // 2, nq), in_specs=in_specs,
        out_specs=pl.BlockSpec((tq, LANES), lambda bi, hp, qi: (bi * nq + qi, hp)),
        out_shape=jax.ShapeDtypeStruct((n, MLA_HEADS * MLA_V), BF16),
        scratch_shapes=[pltpu.VMEM((tp // CHUNK, 2, ATTN_V_ROWS, CHUNK), BF16),
                        pltpu.VMEM((2, ATTN_KV_BLOCK, tq), F32), pltpu.VMEM((2, SUBLANES, tq), F32),
                        pltpu.VMEM((2, ATTN_KV_BLOCK, tq), BF16),
                        pltpu.VMEM((2, SUBLANES, tq), F32), pltpu.VMEM((2, SUBLANES, tq), F32),
                        pltpu.VMEM((2, ATTN_V_ROWS, tq), F32)],
        compiler_params=_cparams("parallel", "parallel", "arbitrary"), name="attention",
    )(q, k, v)


def _post_kernel(*refs, n_in):
    h_ref = refs[0]
    a_refs = refs[1:1 + n_in]
    w_refs = refs[1 + n_in:1 + 2 * n_in]
    g_ref = refs[1 + 2 * n_in]
    o_ref = refs[2 + 2 * n_in]
    m = _dot(a_refs[0][...], w_refs[0][...])
    for a_ref, w_ref in zip(a_refs[1:], w_refs[1:]):
        m = m + _dot(a_ref[...], w_ref[...])
    o_ref[...] = h_ref[...] + _rms(m, g_ref[...])


def _post(h2, acts, wts, g):
    n, d = h2.shape
    tm = _pick_tile(n, (512, 384, 256, 128))
    rows = lambda i: (i, 0)
    in_specs = [pl.BlockSpec((tm, d), rows)]
    in_specs += [pl.BlockSpec((tm, a.shape[1]), rows) for a in acts]
    in_specs += [_const_spec(w.shape) for w in wts] + [_const_spec(g.shape)]
    return pl.pallas_call(
        functools.partial(_post_kernel, n_in=len(acts)), grid=(n // tm,), in_specs=in_specs,
        out_specs=pl.BlockSpec((tm, d), rows), out_shape=jax.ShapeDtypeStruct((n, d), F32),
        compiler_params=_cparams("parallel"), name="mixer_out",
    )(h2, *acts, *wts, g)


def _mlp_kernel(h_ref, g1_ref, wup_ref, wdn_ref, g2_ref, o_ref, *, ff_chunk):
    x = h_ref[...]
    hn = _rms(x, g1_ref[...]).astype(BF16)
    d_ff = wup_ref.shape[1]
    y = None
    for c0 in range(0, d_ff, ff_chunk):
        u = jnp.maximum(_dot(hn, wup_ref[:, c0:c0 + ff_chunk]), 0.0)
        part = _dot((u * u).astype(BF16), wdn_ref[c0:c0 + ff_chunk, :])
        y = part if y is None else y + part
    o_ref[...] = x + _rms(y, g2_ref[...])


def _mlp(h2, g1, wup, wdn, g2):
    n, d = h2.shape
    tm = _pick_tile(n, (256, 128))
    rows = lambda i: (i, 0)
    in_specs = [pl.BlockSpec((tm, d), rows), _const_spec(g1.shape), _const_spec(wup.shape),
                _const_spec(wdn.shape), _const_spec(g2.shape)]
    return pl.pallas_call(
        functools.partial(_mlp_kernel, ff_chunk=1024), grid=(n // tm,), in_specs=in_specs,
        out_specs=pl.BlockSpec((tm, d), rows), out_shape=jax.ShapeDtypeStruct((n, d), F32),
        compiler_params=_cparams("parallel"), name="mlp",
    )(h2, g1, wup, wdn, g2)


def _proj_odd_kernel(h_ref, g_ref, wx_ref, wy_ref, x_ref, gate_ref):
    hn = _rms(h_ref[...], g_ref[...]).astype(BF16)
    x_ref[...] = _dot(hn, wx_ref[...]).astype(BF16)
    y = _dot(hn, wy_ref[...])
    gelu = 0.5 * y * (1.0 + jnp.tanh(np.sqrt(2.0 / np.pi).astype(np.float32) * (y + 0.044715 * (y * y * y))))
    gate_ref[...] = gelu.astype(BF16)


def _proj_odd(h2, g, wx, wy):
    n, d = h2.shape
    tm = _pick_tile(n, (512, 384, 256, 128))
    rows = lambda i: (i, 0)
    w = wx.shape[1]
    return pl.pallas_call(
        _proj_odd_kernel, grid=(n // tm,),
        in_specs=[pl.BlockSpec((tm, d), rows), _const_spec(g.shape), _const_spec(wx.shape), _const_spec(wy.shape)],
        out_specs=(pl.BlockSpec((tm, w), rows), pl.BlockSpec((tm, w), rows)),
        out_shape=(jax.ShapeDtypeStruct((n, w), BF16), jax.ShapeDtypeStruct((n, w), BF16)),
        compiler_params=_cparams("parallel"), name="proj_odd",
    )(h2, g, wx, wy)


def _rglru_kernel(x_ref, gate_ref, cw_ref, cb_ref, wa_ref, ba_ref, wi_ref, bi_ref, lam_ref, o_ref,
                  tail_scr, h_scr):
    t = pl.program_id(1)
    L = x_ref.shape[0]

    @pl.when(t == 0)
    def _():
        tail_scr[...] = jnp.zeros_like(tail_scr)
        h_scr[...] = jnp.zeros_like(h_scr)

    cur = x_ref[...].astype(F32)
    xr = _causal_conv(cur, tail_scr[...], cw_ref, cb_ref)
    tail_scr[...] = cur[L - SUBLANES:L]
    rs, gs = [], []
    for nb in range(LRU_BLOCKS):
        sl = slice(nb * LRU_BLOCK, (nb + 1) * LRU_BLOCK)
        xb = xr[:, sl].astype(BF16)
        rs.append(_dot(xb, wa_ref[nb]))
        gs.append(_dot(xb, wi_ref[nb]))
    r = _sigmoid(jnp.concatenate(rs, axis=1) + ba_ref[...])
    ig = _sigmoid(jnp.concatenate(gs, axis=1) + bi_ref[...])
    log_a = (-LRU_C) * r * _softplus(-lam_ref[...])
    a = jnp.exp(log_a)
    row = lax.broadcasted_iota(jnp.int32, (L, 1), 0)
    valid = (t * L + row) >= PAD
    u = jnp.where(valid, jnp.sqrt(1.0 - a * a) * (ig * xr), 0.0)

    rowf = lax.broadcasted_iota(jnp.int32, (L, LRU_WIDTH), 0)
    d = 1
    while d < L:
        keep = rowf >= d
        a_s = jnp.where(keep, pltpu.roll(a, d, 0), 1.0)
        u_s = jnp.where(keep, pltpu.roll(u, d, 0), 0.0)
        u = a * u_s + u
        a = a * a_s
        d *= 2
    hs = u + a * h_scr[0:1, :]
    h_scr[...] = jnp.broadcast_to(hs[L - 1:L, :], h_scr.shape)
    o_ref[...] = (hs * gate_ref[...].astype(F32)).astype(BF16)


def _rglru(x3, gate3, cw, cb, wa, ba, wi, bi, lam):
    b, tp, w = x3.shape
    tt = CHUNK
    row3 = lambda bi_, ti: (bi_, ti, 0)
    in_specs = [pl.BlockSpec((None, tt, w), row3), pl.BlockSpec((None, tt, w), row3)]
    in_specs += [_const_spec(a.shape) for a in (cw, cb, wa, ba, wi, bi, lam)]
    return pl.pallas_call(
        _rglru_kernel, grid=(b, tp // tt), in_specs=in_specs,
        out_specs=pl.BlockSpec((None, tt, w), row3),
        out_shape=jax.ShapeDtypeStruct((b, tp, w), BF16),
        scratch_shapes=[pltpu.VMEM((SUBLANES, w), F32), pltpu.VMEM((SUBLANES, w), F32)],
        compiler_params=_cparams("parallel", "arbitrary"), name="rglru",
    )(x3, gate3, cw, cb, wa, ba, wi, bi, lam)


def _rope_tables(tp):
    inv = ROPE_BASE ** (-jnp.arange(0, MLA_ROPE, 2, dtype=F32) / MLA_ROPE)
    pos = jnp.maximum(jnp.arange(tp, dtype=F32) - PAD, 0.0)
    ang = pos[:, None] * inv[None, :]
    cos, sin = jnp.cos(ang), jnp.sin(ang)
    zeros_lo = jnp.zeros((tp, MLA_NOPE), F32)
    zeros_hi = jnp.zeros((tp, LANES - MLA_NOPE - MLA_ROPE), F32)
    ck = jnp.concatenate([zeros_lo, cos, cos, zeros_hi], axis=1)
    sk = jnp.concatenate([zeros_lo, sin, sin, zeros_hi], axis=1)
    scale = (MLA_NOPE + MLA_ROPE) ** -0.5 * np.log2(np.e)
    cq = jnp.concatenate([jnp.ones((tp, MLA_NOPE), F32), cos, cos, zeros_hi], axis=1) * scale
    sq = sk * scale
    return cq, sq, ck, sk


def _even_weights(w_in, w_q_up, w_kv_up):
    offs = np.cumsum((SSD_D_INNER, SSD_CONV_CH, SSD_HEADS, MLA_Q_RANK, MLA_KV_RANK, MLA_ROPE))
    wz = w_in[:, :offs[0]]
    wxbc = w_in[:, offs[0]:offs[1]]
    wdt = w_in[:, offs[1]:offs[2]]
    wcq = w_in[:, offs[2]:offs[3]]
    wckv = w_in[:, offs[3]:offs[4]]
    wkr = w_in[:, offs[4]:offs[5]]
    half = MLA_ROPE // 2
    hi_pad = LANES - MLA_NOPE - MLA_ROPE
    wdt_p = jnp.pad(wdt, ((0, 0), (0, LANES - SSD_HEADS)))
    wkr_p = jnp.pad(wkr, ((0, 0), (MLA_NOPE, hi_pad)))
    wkr_s = jnp.pad(jnp.concatenate([-wkr[:, half:], wkr[:, :half]], axis=1), ((0, 0), (MLA_NOPE, hi_pad)))
    wq = w_q_up.reshape(MLA_Q_RANK, MLA_HEADS, MLA_NOPE + MLA_ROPE)
    q_nope, q_rope = wq[..., :MLA_NOPE], wq[..., MLA_NOPE:]
    wq_p = jnp.pad(wq, ((0, 0), (0, 0), (0, hi_pad))).reshape(MLA_Q_RANK, MLA_HEADS * LANES)
    q_swap = jnp.concatenate([jnp.zeros_like(q_nope), -q_rope[..., half:], q_rope[..., :half]], axis=-1)
    wq_s = jnp.pad(q_swap, ((0, 0), (0, 0), (0, hi_pad))).reshape(MLA_Q_RANK, MLA_HEADS * LANES)
    wkv = w_kv_up.reshape(MLA_KV_RANK, MLA_HEADS, MLA_NOPE + MLA_V)
    wk_p = jnp.pad(wkv[..., :MLA_NOPE], ((0, 0), (0, 0), (0, LANES - MLA_NOPE))).reshape(
        MLA_KV_RANK, MLA_HEADS * LANES)
    wv = wkv[..., MLA_NOPE:].reshape(MLA_KV_RANK, MLA_HEADS * MLA_V)
    bf = lambda a: a.astype(BF16)
    return dict(wz=bf(wz), wxbc=bf(wxbc), wdt=bf(wdt_p), wkr_p=bf(wkr_p), wkr_s=bf(wkr_s), wcq=bf(wcq),
                wckv=bf(wckv), wq_p=bf(wq_p), wq_s=bf(wq_s), wk_p=bf(wk_p), wv=bf(wv))


def _row(a):
    return a.reshape(1, -1).astype(F32)


def kernel(x, meta_tokens, mix_pre_g, mix_post_g, mlp_pre_g, mlp_post_g, w_up, w_down, w_in, ssd_conv_w,
           ssd_conv_b, ssd_dt_bias, ssd_a_log, ssd_d, ssd_norm_g, mla_q_norm_g, mla_w_q_up, mla_kv_norm_g,
           mla_w_kv_up, w_out_ab, rg_w_x, rg_w_y, rg_conv_w, rg_conv_b, rg_w_a, rg_b_a, rg_w_i, rg_b_i,
           rg_lambda, rg_w_out):
    b, seq, d = x.shape
    depth = mix_pre_g.shape[0]
    tp = PAD + N_META + seq
    n = b * tp
    meta = jnp.broadcast_to(meta_tokens[None].astype(x.dtype), (b, N_META, d))
    h = jnp.concatenate([jnp.zeros((b, PAD, d), x.dtype), meta, x], axis=1).reshape(n, d)
    tables = _rope_tables(tp)

    for layer in range(depth):
        if layer % 2 == 0:
            e = layer // 2
            w = _even_weights(w_in[e], mla_w_q_up[e], mla_w_kv_up[e])
            wts = (w["wz"], w["wxbc"], w["wdt"], w["wkr_p"], w["wkr_s"], w["wcq"], _row(mla_q_norm_g[e]),
                   w["wckv"], _row(mla_kv_norm_g[e]), w["wq_p"], w["wq_s"], w["wk_p"], w["wv"])
            z, xbc, dt, q, k, v = _proj_even(h.reshape(b, tp, d), _row(mix_pre_g[layer]), wts, tables)
            lane_pad = lambda a: jnp.pad(_row(a), ((0, 0), (0, LANES - SSD_HEADS)))
            y_ssd = _ssd(xbc.reshape(b, tp, SSD_CONV_CH), dt.reshape(b, tp, LANES), z.reshape(b, tp, SSD_D_INNER),
                         ssd_conv_w[e].astype(F32), _row(ssd_conv_b[e]), lane_pad(ssd_dt_bias[e]),
                         lane_pad(ssd_a_log[e]), _row(jnp.repeat(ssd_d[e], SSD_HEAD_DIM)), _row(ssd_norm_g[e]))
            y_att = _attention(q, k, v, b, tp)
            wo = w_out_ab[e].astype(BF16)
            h = _post(h, (y_ssd.reshape(n, SSD_D_INNER), y_att), (wo[:SSD_D_INNER], wo[SSD_D_INNER:]),
                      _row(mix_post_g[layer]))
        else:
            o = layer // 2
            xr, gate = _proj_odd(h, _row(mix_pre_g[layer]), rg_w_x[o].astype(BF16), rg_w_y[o].astype(BF16))
            y = _rglru(xr.reshape(b, tp, LRU_WIDTH), gate.reshape(b, tp, LRU_WIDTH), rg_conv_w[o].astype(F32),
                       _row(rg_conv_b[o]), rg_w_a[o].astype(BF16), _row(rg_b_a[o]), rg_w_i[o].astype(BF16),
                       _row(rg_b_i[o]), _row(rg_lambda[o]))
            h = _post(h, (y.reshape(n, LRU_WIDTH),), (rg_w_out[o].astype(BF16),), _row(mix_post_g[layer]))
        h = _mlp(h, _row(mlp_pre_g[layer]), w_up[layer].astype(BF16), w_down[layer].astype(BF16),
                 _row(mlp_post_g[layer]))
    return h.reshape(b, tp, d)[:, PAD + N_META:]
```

```python
import functools

import jax
import jax.numpy as jnp
import numpy as np
from jax import lax
from jax.experimental import pallas as pl
from jax.experimental.pallas import tpu as pltpu

F32 = jnp.float32
BF16 = jnp.bfloat16

N_META = 16
CHUNK = 128
PAD = CHUNK - N_META
EPS = 1e-6
SSD_HEADS = 16
SSD_HEAD_DIM = 64
SSD_D_INNER = SSD_HEADS * SSD_HEAD_DIM
SSD_GROUPS = 2
SSD_STATE = 128
SSD_CONV = 4
SSD_BC = SSD_GROUPS * SSD_STATE
SSD_CONV_CH = SSD_D_INNER + 2 * SSD_BC
MLA_HEADS = 16
MLA_NOPE = 64
MLA_ROPE = 32
MLA_V = 64
MLA_Q_RANK = 384
MLA_KV_RANK = 256
ROPE_BASE = 10000.0
LRU_BLOCKS = 10
LRU_BLOCK = 128
LRU_WIDTH = LRU_BLOCKS * LRU_BLOCK
LRU_C = 8.0
LRU_CONV_COLS = 640
LANES = 128
SUBLANES = 8
VMEM_LIMIT_BYTES = 56 * 1024 * 1024
NEG_BIG = -1e30


def _cparams(*sem):
    return pltpu.CompilerParams(dimension_semantics=sem, vmem_limit_bytes=VMEM_LIMIT_BYTES)


def _pick_tile(n, candidates):
    for c in candidates:
        if n % c == 0:
            return c
    raise ValueError(f"no tile in {candidates} divides {n}")


def _const_spec(shape):
    nd = len(shape)
    return pl.BlockSpec(shape, lambda *_: (0,) * nd)


def _rms(x, g):
    ms = jnp.mean(x * x, axis=-1, keepdims=True)
    return x * lax.rsqrt(ms + EPS) * g


def _sigmoid(x):
    return 1.0 / (1.0 + jnp.exp(-x))


def _softplus(x):
    return jnp.maximum(x, 0.0) + jnp.log(1.0 + jnp.exp(-jnp.abs(x)))


def _dot(a, b):
    return jnp.dot(a, b, preferred_element_type=F32)


def _dot_nt(a, b):
    return lax.dot_general(a, b, (((1,), (1,)), ((), ())), preferred_element_type=F32)


def _dot_tn(a, b):
    return lax.dot_general(a, b, (((0,), (0,)), ((), ())), preferred_element_type=F32)


def _shift_rows(cur, tail, k):
    if k == 0:
        return cur
    rolled = pltpu.roll(cur, k, 0)
    row = lax.broadcasted_iota(jnp.int32, tail.shape, 0)
    top = jnp.where(row < k, pltpu.roll(tail, k, 0), rolled[0:SUBLANES])
    return jnp.concatenate([top, rolled[SUBLANES:]], axis=0)


def _causal_conv(cur, tail_scr, w_ref, b_ref, cols):
    k = w_ref.shape[0]
    tail = tail_scr[:, cols]
    out = b_ref[:, cols]
    for i in range(k):
        out = out + _shift_rows(cur, tail, k - 1 - i) * w_ref[i:i + 1, cols]
    tail_scr[:, cols] = cur[cur.shape[0] - SUBLANES:]
    return out


CONV_COLS = 512


def _proj_even_kernel(h_ref, g_ref, wz_ref, wxbc_ref, wdt_ref, wkrp_ref, wkrs_ref, wcq_ref, qg_ref,
                      wckv_ref, kvg_ref, wqp_ref, wqs_ref, wkp_ref, wv_ref, cw_ref, cb_ref,
                      cq_ref, sq_ref, ck_ref, sk_ref,
                      z_ref, xbc_ref, dt_ref, q_ref, k_ref, v_ref, tail_scr):
    @pl.when(pl.program_id(1) == 0)
    def _():
        tail_scr[...] = jnp.zeros_like(tail_scr)

    hn = _rms(h_ref[...], g_ref[...]).astype(BF16)
    z_ref[...] = _dot(hn, wz_ref[...]).astype(BF16)
    for c0 in range(0, SSD_CONV_CH, CONV_COLS):
        cols = slice(c0, c0 + CONV_COLS)
        conv = _causal_conv(_dot(hn, wxbc_ref[:, cols]), tail_scr, cw_ref, cb_ref, cols)
        xbc_ref[:, cols] = (conv * _sigmoid(conv)).astype(BF16)
    dt_ref[...] = _dot(hn, wdt_ref[...])
    kr = _dot(hn, wkrp_ref[...]) * ck_ref[...] + _dot(hn, wkrs_ref[...]) * sk_ref[...]
    cqn = _rms(_dot(hn, wcq_ref[...]), qg_ref[...]).astype(BF16)
    ckvn = _rms(_dot(hn, wckv_ref[...]), kvg_ref[...]).astype(BF16)
    v_ref[...] = _dot(ckvn, wv_ref[...]).astype(BF16)
    cq_t = cq_ref[...]
    sq_t = sq_ref[...]
    for hd in range(MLA_HEADS):
        sl = slice(hd * LANES, (hd + 1) * LANES)
        qh = _dot(cqn, wqp_ref[:, sl]) * cq_t + _dot(cqn, wqs_ref[:, sl]) * sq_t
        q_ref[hd] = qh.astype(BF16)
        k_ref[hd] = (_dot(ckvn, wkp_ref[:, sl]) + kr).astype(BF16)


def _proj_even(h3, g, wts, tables):
    b, tp, d = h3.shape
    tm = _pick_tile(tp, (384, 128))
    nt = tp // tm
    n = b * tp
    row3 = lambda bi, i: (bi, i, 0)
    flat = lambda bi, i: (bi * nt + i, 0)
    tab = lambda bi, i: (i, 0)
    in_specs = [pl.BlockSpec((None, tm, d), row3), _const_spec(g.shape)]
    in_specs += [_const_spec(w.shape) for w in wts]
    in_specs += [pl.BlockSpec((tm, LANES), tab) for _ in tables]
    out_shape = (
        jax.ShapeDtypeStruct((n, SSD_D_INNER), BF16),
        jax.ShapeDtypeStruct((n, SSD_CONV_CH), BF16),
        jax.ShapeDtypeStruct((n, LANES), F32),
        jax.ShapeDtypeStruct((MLA_HEADS, n, LANES), BF16),
        jax.ShapeDtypeStruct((MLA_HEADS, n, LANES), BF16),
        jax.ShapeDtypeStruct((n, MLA_HEADS * MLA_V), BF16),
    )
    head3 = lambda bi, i: (0, bi * nt + i, 0)
    out_specs = (
        pl.BlockSpec((tm, SSD_D_INNER), flat),
        pl.BlockSpec((tm, SSD_CONV_CH), flat),
        pl.BlockSpec((tm, LANES), flat),
        pl.BlockSpec((MLA_HEADS, tm, LANES), head3),
        pl.BlockSpec((MLA_HEADS, tm, LANES), head3),
        pl.BlockSpec((tm, MLA_HEADS * MLA_V), flat),
    )
    return pl.pallas_call(
        _proj_even_kernel, grid=(b, nt), in_specs=in_specs, out_specs=out_specs, out_shape=out_shape,
        scratch_shapes=[pltpu.VMEM((SUBLANES, SSD_CONV_CH), F32)],
        compiler_params=_cparams("parallel", "arbitrary"), name="proj_even",
    )(h3, g, *wts, *tables)


def _cumsum_rows(x):
    n = x.shape[0]
    row = lax.broadcasted_iota(jnp.int32, x.shape, 0)
    d = 1
    while d < n:
        x = x + jnp.where(row >= d, pltpu.roll(x, d, 0), 0.0)
        d *= 2
    return x


SSD_BATCH_BLOCK = 2


def _ssd_chunk(c, xbc, dt_raw, zf, dtb, alog, dsk, ng, state_ref, y_scr):
    L = CHUNK
    heads_per_group = SSD_HEADS // SSD_GROUPS
    row = lax.broadcasted_iota(jnp.int32, (L, 1), 0)
    valid = (c * L + row) >= PAD
    act = jnp.where(valid, xbc.astype(F32), 0.0)
    xs = act[:, :SSD_D_INNER]
    bm = act[:, SSD_D_INNER:SSD_D_INNER + SSD_BC]
    cm = act[:, SSD_D_INNER + SSD_BC:]

    dtv = jnp.where(valid, _softplus(dt_raw + dtb), 0.0)
    a_cs = _cumsum_rows(dtv * (-jnp.exp(alog)))
    a_cs_t = a_cs.T
    dt_t = dtv.T
    tot = a_cs[L - 1:L, :]
    w_t = jnp.exp(a_cs_t[:, L - 1:L] - a_cs_t) * dt_t
    cd = jnp.exp(tot)
    causal = lax.broadcasted_iota(jnp.int32, (L, L), 0) >= lax.broadcasted_iota(jnp.int32, (L, L), 1)
    lo = lax.broadcasted_iota(jnp.int32, (L, LANES), 1) < SSD_HEAD_DIM

    for g in range(SSD_GROUPS):
        cc = cm[:, g * SSD_STATE:(g + 1) * SSD_STATE]
        bc = bm[:, g * SSD_STATE:(g + 1) * SSD_STATE]
        bc_t = bc.T
        cb = _dot_nt(cc.astype(BF16), bc.astype(BF16))
        for hp in range(heads_per_group // 2):
            pair = g * (heads_per_group // 2) + hp
            sl = slice(pair * LANES, (pair + 1) * LANES)
            xs_pair = xs[:, sl]
            st_pair = state_ref[:, sl]
            xs_b = xs_pair.astype(BF16)
            rhs = jnp.concatenate([xs_b, st_pair.astype(BF16)], axis=0)
            ys, sts, cds = [], [], []
            for hh in range(2):
                h = 2 * pair + hh
                a_col = jnp.broadcast_to(a_cs[:, h:h + 1], (L, L))
                dec = jnp.exp(jnp.where(causal, a_col - a_cs_t[h:h + 1, :], -jnp.inf))
                m = cb * dec * dt_t[h:h + 1, :]
                lhs = jnp.concatenate([m, cc * jnp.exp(a_col)], axis=1).astype(BF16)
                ys.append(_dot(lhs, rhs))
                sts.append(_dot((bc_t * w_t[h:h + 1, :]).astype(BF16), xs_b))
                cds.append(cd[:, h:h + 1])
            y_scr[:, sl] = jnp.where(lo, ys[0], ys[1]) + dsk[:, sl] * xs_pair
            state_ref[:, sl] = jnp.where(lo, cds[0], cds[1]) * st_pair + jnp.where(lo, sts[0], sts[1])

    return _rms(y_scr[...] * (zf * _sigmoid(zf)), ng).astype(BF16)


def _ssd_kernel(xbc_ref, dt_ref, z_ref, dtb_ref, alog_ref, dsk_ref, ng_ref, y_ref, state_scr, y_scr):
    c = pl.program_id(1)

    @pl.when(c == 0)
    def _():
        state_scr[...] = jnp.zeros_like(state_scr)

    for bb in range(xbc_ref.shape[0]):
        y_ref[bb] = _ssd_chunk(c, xbc_ref[bb], dt_ref[bb], z_ref[bb].astype(F32), dtb_ref[...], alog_ref[...],
                               dsk_ref[...], ng_ref[...], state_scr.at[bb], y_scr.at[bb])


def _ssd(xbc3, dt3, z3, dtb, alog, dsk, ng):
    b, tp, _ = xbc3.shape
    nc = tp // CHUNK
    gb = _pick_tile(b, (SSD_BATCH_BLOCK, 1))
    row3 = lambda bi, ci: (bi, ci, 0)
    in_specs = [
        pl.BlockSpec((gb, CHUNK, SSD_CONV_CH), row3),
        pl.BlockSpec((gb, CHUNK, LANES), row3),
        pl.BlockSpec((gb, CHUNK, SSD_D_INNER), row3),
    ] + [_const_spec(a.shape) for a in (dtb, alog, dsk, ng)]
    return pl.pallas_call(
        _ssd_kernel, grid=(b // gb, nc), in_specs=in_specs,
        out_specs=pl.BlockSpec((gb, CHUNK, SSD_D_INNER), row3),
        out_shape=jax.ShapeDtypeStruct((b, tp, SSD_D_INNER), BF16),
        scratch_shapes=[pltpu.VMEM((gb, SSD_STATE, SSD_D_INNER), F32),
                        pltpu.VMEM((gb, CHUNK, SSD_D_INNER), F32)],
        compiler_params=_cparams("parallel", "arbitrary"), name="ssd",
    )(xbc3, dt3, z3, dtb, alog, dsk, ng)


ATTN_KV_BLOCK = 256
ATTN_V_ROWS = MLA_V + 16


def _attn_kernel(q_ref, k_ref, v_ref, o_ref, vt_scr, s_scr, bmax_scr, p_scr, alpha_scr, m_scr, acc_scr, *, tq):
    kb = ATTN_KV_BLOCK
    qi = pl.program_id(2)
    q0 = qi * tq
    end = q0 + tq
    n_int = q0 // kb
    n_blocks = (end + kb - 1) // kb
    last_off = end - kb

    @pl.when(qi == 0)
    def _():
        ones = jnp.ones((ATTN_V_ROWS - MLA_V, CHUNK), BF16)
        for c in range(vt_scr.shape[0]):
            vt = v_ref[c * CHUNK:(c + 1) * CHUNK, :].astype(F32).T.astype(BF16)
            for hh in range(2):
                vt_scr[c, hh, :MLA_V, :] = vt[hh * MLA_V:(hh + 1) * MLA_V, :]
                vt_scr[c, hh, MLA_V:, :] = ones

    m_scr[...] = jnp.full_like(m_scr, NEG_BIG)
    acc_scr[...] = jnp.zeros_like(acc_scr)

    def offset(bk):
        return pl.multiple_of(jnp.minimum(bk * kb, last_off), CHUNK)

    def qk(hh, bk, masked):
        off = offset(bk)
        s = _dot_nt(k_ref[hh, pl.ds(off, kb), :], q_ref[hh])
        if masked:
            key = lax.broadcasted_iota(jnp.int32, (kb, tq), 0)
            keep = (key - lax.broadcasted_iota(jnp.int32, (kb, tq), 1)) <= (q0 - off)
            first_key = jnp.where(bk == 0, PAD, bk * kb - off)
            s = jnp.where(keep, s, NEG_BIG)
            s = jnp.where(key >= first_key, s, NEG_BIG)
        s_scr[hh] = s
        bmax_scr[hh] = jnp.broadcast_to(jnp.max(s, axis=0, keepdims=True), (SUBLANES, tq))

    def softmax(hh):
        m_prev = m_scr[hh, 0:1, :]
        m_new = jnp.maximum(m_prev, bmax_scr[hh, 0:1, :])
        alpha_scr[hh] = jnp.broadcast_to(jnp.exp2(m_prev - m_new), (SUBLANES, tq))
        m_scr[hh] = jnp.broadcast_to(m_new, (SUBLANES, tq))
        p_scr[hh] = jnp.exp2(s_scr[hh] - m_new).astype(BF16)

    def pv(hh, bk):
        c0 = offset(bk) // CHUNK
        vt = jnp.concatenate([vt_scr[c0 + c, hh] for c in range(kb // CHUNK)], axis=1)
        acc_scr[hh] = alpha_scr[hh, 0:1, :] * acc_scr[hh] + _dot(vt, p_scr[hh])

    def step(bk, masked):
        softmax(1)
        qk(0, bk, masked)
        pv(0, bk - 1)
        softmax(0)
        qk(1, bk, masked)
        pv(1, bk - 1)

    def plain_step(bk, carry):
        step(bk, False)
        return carry

    def masked_step(bk, carry):
        step(bk, True)
        return carry

    qk(0, 0, True)
    softmax(0)
    qk(1, 0, True)
    lax.fori_loop(1, n_int, plain_step, 0)
    lax.fori_loop(jnp.maximum(n_int, 1), n_blocks, masked_step, 0)
    softmax(1)
    pv(0, n_blocks - 1)
    pv(1, n_blocks - 1)

    out_t = jnp.concatenate([acc_scr[hh, :MLA_V, :] / acc_scr[hh, MLA_V:MLA_V + 1, :] for hh in range(2)],
                            axis=0)
    qpos = q0 + lax.broadcasted_iota(jnp.int32, (1, tq), 1)
    out_t = jnp.where(qpos >= PAD, out_t, 0.0)
    for c in range(tq // CHUNK):
        o_ref[c * CHUNK:(c + 1) * CHUNK, :] = out_t[:, c * CHUNK:(c + 1) * CHUNK].T.astype(BF16)


def _attention(q, k, v, b, tp):
    tq = _pick_tile(tp, (1408, 384, 256))
    nq = tp // tq
    n = b * tp
    in_specs = [
        pl.BlockSpec((2, tq, LANES), lambda bi, hp, qi: (hp, bi * nq + qi, 0)),
        pl.BlockSpec((2, tp, LANES), lambda bi, hp, qi: (hp, bi, 0)),
        pl.BlockSpec((tp, LANES), lambda bi, hp, qi: (bi, hp)),
    ]
    return pl.pallas_call(
        functools.partial(_attn_kernel, tq=tq), grid=(b, MLA_HEADS // 2, nq), in_specs=in_specs,
        out_specs=pl.BlockSpec((tq, LANES), lambda bi, hp, qi: (bi * nq + qi, hp)),
        out_shape=jax.ShapeDtypeStruct((n, MLA_HEADS * MLA_V), BF16),
        scratch_shapes=[pltpu.VMEM((tp // CHUNK, 2, ATTN_V_ROWS, CHUNK), BF16),
                        pltpu.VMEM((2, ATTN_KV_BLOCK, tq), F32), pltpu.VMEM((2, SUBLANES, tq), F32),
                        pltpu.VMEM((2, ATTN_KV_BLOCK, tq), BF16),
                        pltpu.VMEM((2, SUBLANES, tq), F32), pltpu.VMEM((2, SUBLANES, tq), F32),
                        pltpu.VMEM((2, ATTN_V_ROWS, tq), F32)],
        compiler_params=_cparams("parallel", "parallel", "arbitrary"), name="attention",
    )(q, k, v)


def _post_kernel(*refs, n_in):
    h_ref = refs[0]
    a_refs = refs[1:1 + n_in]
    w_refs = refs[1 + n_in:1 + 2 * n_in]
    g_ref = refs[1 + 2 * n_in]
    o_ref = refs[2 + 2 * n_in]
    m = _dot(a_refs[0][...], w_refs[0][...])
    for a_ref, w_ref in zip(a_refs[1:], w_refs[1:]):
        m = m + _dot(a_ref[...], w_ref[...])
    o_ref[...] = h_ref[...] + _rms(m, g_ref[...])


def _post(h2, acts, wts, g):
    n, d = h2.shape
    tm = _pick_tile(n, (512, 384, 256, 128))
    rows = lambda i: (i, 0)
    in_specs = [pl.BlockSpec((tm, d), rows)]
    in_specs += [pl.BlockSpec((tm, a.shape[1]), rows) for a in acts]
    in_specs += [_const_spec(w.shape) for w in wts] + [_const_spec(g.shape)]
    return pl.pallas_call(
        functools.partial(_post_kernel, n_in=len(acts)), grid=(n // tm,), in_specs=in_specs,
        out_specs=pl.BlockSpec((tm, d), rows), out_shape=jax.ShapeDtypeStruct((n, d), F32),
        compiler_params=_cparams("parallel"), name="mixer_out",
    )(h2, *acts, *wts, g)


def _mlp_kernel(h_ref, g1_ref, wup_ref, wdn_ref, g2_ref, o_ref, *, ff_chunk):
    x = h_ref[...]
    hn = _rms(x, g1_ref[...]).astype(BF16)
    d_ff = wup_ref.shape[1]
    y = None
    for c0 in range(0, d_ff, ff_chunk):
        u = jnp.maximum(_dot(hn, wup_ref[:, c0:c0 + ff_chunk]), 0.0)
        part = _dot((u * u).astype(BF16), wdn_ref[c0:c0 + ff_chunk, :])
        y = part if y is None else y + part
    o_ref[...] = x + _rms(y, g2_ref[...])


def _mlp(h2, g1, wup, wdn, g2):
    n, d = h2.shape
    tm = _pick_tile(n, (256, 128))
    rows = lambda i: (i, 0)
    in_specs = [pl.BlockSpec((tm, d), rows), _const_spec(g1.shape), _const_spec(wup.shape),
                _const_spec(wdn.shape), _const_spec(g2.shape)]
    return pl.pallas_call(
        functools.partial(_mlp_kernel, ff_chunk=1024), grid=(n // tm,), in_specs=in_specs,
        out_specs=pl.BlockSpec((tm, d), rows), out_shape=jax.ShapeDtypeStruct((n, d), F32),
        compiler_params=_cparams("parallel"), name="mlp",
    )(h2, g1, wup, wdn, g2)


def _proj_odd_kernel(h_ref, g_ref, wx_ref, wy_ref, cw_ref, cb_ref, x_ref, gate_ref, tail_scr):
    @pl.when(pl.program_id(0) == 0)
    def _():
        tail_scr[...] = jnp.zeros_like(tail_scr)

    hn = _rms(h_ref[...], g_ref[...]).astype(BF16)
    for c0 in range(0, LRU_WIDTH, LRU_CONV_COLS):
        cols = slice(c0, c0 + LRU_CONV_COLS)
        x_ref[:, cols] = _causal_conv(_dot(hn, wx_ref[:, cols]), tail_scr, cw_ref, cb_ref, cols).astype(BF16)
    y = _dot(hn, wy_ref[...])
    gelu = 0.5 * y * (1.0 + jnp.tanh(np.sqrt(2.0 / np.pi).astype(np.float32) * (y + 0.044715 * (y * y * y))))
    gate_ref[...] = gelu.astype(BF16)


def _proj_odd(h2, g, wx, wy, cw, cb):
    n, d = h2.shape
    tm = _pick_tile(n, (512, 384, 256, 128))
    rows = lambda i: (i, 0)
    w = wx.shape[1]
    return pl.pallas_call(
        _proj_odd_kernel, grid=(n // tm,),
        in_specs=[pl.BlockSpec((tm, d), rows)] + [_const_spec(a.shape) for a in (g, wx, wy, cw, cb)],
        out_specs=(pl.BlockSpec((tm, w), rows), pl.BlockSpec((tm, w), rows)),
        out_shape=(jax.ShapeDtypeStruct((n, w), BF16), jax.ShapeDtypeStruct((n, w), BF16)),
        scratch_shapes=[pltpu.VMEM((SUBLANES, w), F32)],
        compiler_params=_cparams("arbitrary"), name="proj_odd",
    )(h2, g, wx, wy, cw, cb)


def _rglru_kernel(x_ref, gate_ref, wa_ref, ba_ref, wi_ref, bi_ref, lam_ref, o_ref, h_scr):
    t = pl.program_id(1)
    L = x_ref.shape[0]

    @pl.when(t == 0)
    def _():
        h_scr[...] = jnp.zeros_like(h_scr)

    xr = x_ref[...].astype(F32)
    rs, gs = [], []
    for nb in range(LRU_BLOCKS):
        sl = slice(nb * LRU_BLOCK, (nb + 1) * LRU_BLOCK)
        xb = x_ref[:, sl]
        rs.append(_dot(xb, wa_ref[nb]))
        gs.append(_dot(xb, wi_ref[nb]))
    r = _sigmoid(jnp.concatenate(rs, axis=1) + ba_ref[...])
    ig = _sigmoid(jnp.concatenate(gs, axis=1) + bi_ref[...])
    log_a = (-LRU_C) * r * _softplus(-lam_ref[...])
    a = jnp.exp(log_a)
    row = lax.broadcasted_iota(jnp.int32, (L, 1), 0)
    valid = (t * L + row) >= PAD
    om = 1.0 - a * a
    root = jnp.where(om > 0.0, om * lax.rsqrt(om), 0.0)
    u = jnp.where(valid, root * (ig * xr), 0.0)

    rowf = lax.broadcasted_iota(jnp.int32, (L, LRU_WIDTH), 0)
    d = 1
    while d < L:
        keep = rowf >= d
        a_s = jnp.where(keep, pltpu.roll(a, d, 0), 1.0)
        u_s = jnp.where(keep, pltpu.roll(u, d, 0), 0.0)
        u = a * u_s + u
        a = a * a_s
        d *= 2
    hs = u + a * h_scr[0:1, :]
    h_scr[...] = jnp.broadcast_to(hs[L - 1:L, :], h_scr.shape)
    o_ref[...] = (hs * gate_ref[...].astype(F32)).astype(BF16)


def _rglru(x3, gate3, wa, ba, wi, bi, lam):
    b, tp, w = x3.shape
    tt = CHUNK
    row3 = lambda bi_, ti: (bi_, ti, 0)
    in_specs = [pl.BlockSpec((None, tt, w), row3), pl.BlockSpec((None, tt, w), row3)]
    in_specs += [_const_spec(a.shape) for a in (wa, ba, wi, bi, lam)]
    return pl.pallas_call(
        _rglru_kernel, grid=(b, tp // tt), in_specs=in_specs,
        out_specs=pl.BlockSpec((None, tt, w), row3),
        out_shape=jax.ShapeDtypeStruct((b, tp, w), BF16),
        scratch_shapes=[pltpu.VMEM((SUBLANES, w), F32)],
        compiler_params=_cparams("parallel", "arbitrary"), name="rglru",
    )(x3, gate3, wa, ba, wi, bi, lam)


def _rope_tables(tp):
    inv = ROPE_BASE ** (-jnp.arange(0, MLA_ROPE, 2, dtype=F32) / MLA_ROPE)
    pos = jnp.maximum(jnp.arange(tp, dtype=F32) - PAD, 0.0)
    ang = pos[:, None] * inv[None, :]
    cos, sin = jnp.cos(ang), jnp.sin(ang)
    zeros_lo = jnp.zeros((tp, MLA_NOPE), F32)
    zeros_hi = jnp.zeros((tp, LANES - MLA_NOPE - MLA_ROPE), F32)
    ck = jnp.concatenate([zeros_lo, cos, cos, zeros_hi], axis=1)
    sk = jnp.concatenate([zeros_lo, sin, sin, zeros_hi], axis=1)
    scale = (MLA_NOPE + MLA_ROPE) ** -0.5 * np.log2(np.e)
    cq = jnp.concatenate([jnp.ones((tp, MLA_NOPE), F32), cos, cos, zeros_hi], axis=1) * scale
    sq = sk * scale
    return cq, sq, ck, sk


def _even_weights(w_in, w_q_up, w_kv_up):
    offs = np.cumsum((SSD_D_INNER, SSD_CONV_CH, SSD_HEADS, MLA_Q_RANK, MLA_KV_RANK, MLA_ROPE))
    wz = w_in[:, :offs[0]]
    wxbc = w_in[:, offs[0]:offs[1]]
    wdt = w_in[:, offs[1]:offs[2]]
    wcq = w_in[:, offs[2]:offs[3]]
    wckv = w_in[:, offs[3]:offs[4]]
    wkr = w_in[:, offs[4]:offs[5]]
    half = MLA_ROPE // 2
    hi_pad = LANES - MLA_NOPE - MLA_ROPE
    wdt_p = jnp.pad(wdt, ((0, 0), (0, LANES - SSD_HEADS)))
    wkr_p = jnp.pad(wkr, ((0, 0), (MLA_NOPE, hi_pad)))
    wkr_s = jnp.pad(jnp.concatenate([-wkr[:, half:], wkr[:, :half]], axis=1), ((0, 0), (MLA_NOPE, hi_pad)))
    wq = w_q_up.reshape(MLA_Q_RANK, MLA_HEADS, MLA_NOPE + MLA_ROPE)
    q_nope, q_rope = wq[..., :MLA_NOPE], wq[..., MLA_NOPE:]
    wq_p = jnp.pad(wq, ((0, 0), (0, 0), (0, hi_pad))).reshape(MLA_Q_RANK, MLA_HEADS * LANES)
    q_swap = jnp.concatenate([jnp.zeros_like(q_nope), -q_rope[..., half:], q_rope[..., :half]], axis=-1)
    wq_s = jnp.pad(q_swap, ((0, 0), (0, 0), (0, hi_pad))).reshape(MLA_Q_RANK, MLA_HEADS * LANES)
    wkv = w_kv_up.reshape(MLA_KV_RANK, MLA_HEADS, MLA_NOPE + MLA_V)
    wk_p = jnp.pad(wkv[..., :MLA_NOPE], ((0, 0), (0, 0), (0, LANES - MLA_NOPE))).reshape(
        MLA_KV_RANK, MLA_HEADS * LANES)
    wv = wkv[..., MLA_NOPE:].reshape(MLA_KV_RANK, MLA_HEADS * MLA_V)
    bf = lambda a: a.astype(BF16)
    return dict(wz=bf(wz), wxbc=bf(wxbc), wdt=bf(wdt_p), wkr_p=bf(wkr_p), wkr_s=bf(wkr_s), wcq=bf(wcq),
                wckv=bf(wckv), wq_p=bf(wq_p), wq_s=bf(wq_s), wk_p=bf(wk_p), wv=bf(wv))


def _row(a):
    return a.reshape(1, -1).astype(F32)


def kernel(x, meta_tokens, mix_pre_g, mix_post_g, mlp_pre_g, mlp_post_g, w_up, w_down, w_in, ssd_conv_w,
           ssd_conv_b, ssd_dt_bias, ssd_a_log, ssd_d, ssd_norm_g, mla_q_norm_g, mla_w_q_up, mla_kv_norm_g,
           mla_w_kv_up, w_out_ab, rg_w_x, rg_w_y, rg_conv_w, rg_conv_b, rg_w_a, rg_b_a, rg_w_i, rg_b_i,
           rg_lambda, rg_w_out):
    b, seq, d = x.shape
    depth = mix_pre_g.shape[0]
    tp = PAD + N_META + seq
    n = b * tp
    meta = jnp.broadcast_to(meta_tokens[None].astype(x.dtype), (b, N_META, d))
    h = jnp.concatenate([jnp.zeros((b, PAD, d), x.dtype), meta, x], axis=1).reshape(n, d)
    tables = _rope_tables(tp)

    for layer in range(depth):
        if layer % 2 == 0:
            e = layer // 2
            w = _even_weights(w_in[e], mla_w_q_up[e], mla_w_kv_up[e])
            wts = (w["wz"], w["wxbc"], w["wdt"], w["wkr_p"], w["wkr_s"], w["wcq"], _row(mla_q_norm_g[e]),
                   w["wckv"], _row(mla_kv_norm_g[e]), w["wq_p"], w["wq_s"], w["wk_p"], w["wv"],
                   ssd_conv_w[e].astype(F32), _row(ssd_conv_b[e]))
            z, xbc, dt, q, k, v = _proj_even(h.reshape(b, tp, d), _row(mix_pre_g[layer]), wts, tables)
            lane_pad = lambda a: jnp.pad(_row(a), ((0, 0), (0, LANES - SSD_HEADS)))
            y_ssd = _ssd(xbc.reshape(b, tp, SSD_CONV_CH), dt.reshape(b, tp, LANES), z.reshape(b, tp, SSD_D_INNER),
                         lane_pad(ssd_dt_bias[e]), lane_pad(ssd_a_log[e]),
                         _row(jnp.repeat(ssd_d[e], SSD_HEAD_DIM)), _row(ssd_norm_g[e]))
            y_att = _attention(q, k, v, b, tp)
            wo = w_out_ab[e].astype(BF16)
            h = _post(h, (y_ssd.reshape(n, SSD_D_INNER), y_att), (wo[:SSD_D_INNER], wo[SSD_D_INNER:]),
                      _row(mix_post_g[layer]))
        else:
            o = layer // 2
            xr, gate = _proj_odd(h, _row(mix_pre_g[layer]), rg_w_x[o].astype(BF16), rg_w_y[o].astype(BF16),
                                 rg_conv_w[o].astype(F32), _row(rg_conv_b[o]))
            y = _rglru(xr.reshape(b, tp, LRU_WIDTH), gate.reshape(b, tp, LRU_WIDTH), rg_w_a[o].astype(BF16),
                       _row(rg_b_a[o]), rg_w_i[o].astype(BF16), _row(rg_b_i[o]), _row(rg_lambda[o]))
            h = _post(h, (y.reshape(n, LRU_WIDTH),), (rg_w_out[o].astype(BF16),), _row(mix_post_g[layer]))
        h = _mlp(h, _row(mlp_pre_g[layer]), w_up[layer].astype(BF16), w_down[layer].astype(BF16),
                 _row(mlp_post_g[layer]))
    return h.reshape(b, tp, d)[:, PAD + N_META:]
```

```python
import functools

import jax
import jax.numpy as jnp
import numpy as np
from jax import lax
from jax.experimental import pallas as pl
from jax.experimental.pallas import tpu as pltpu

F32 = jnp.float32
BF16 = jnp.bfloat16

N_META = 16
CHUNK = 128
PAD = CHUNK - N_META
EPS = 1e-6
SSD_HEADS = 16
SSD_HEAD_DIM = 64
SSD_D_INNER = SSD_HEADS * SSD_HEAD_DIM
SSD_GROUPS = 2
SSD_STATE = 128
SSD_CONV = 4
SSD_BC = SSD_GROUPS * SSD_STATE
SSD_CONV_CH = SSD_D_INNER + 2 * SSD_BC
MLA_HEADS = 16
MLA_NOPE = 64
MLA_ROPE = 32
MLA_V = 64
MLA_Q_RANK = 384
MLA_KV_RANK = 256
ROPE_BASE = 10000.0
LRU_BLOCKS = 10
LRU_BLOCK = 128
LRU_WIDTH = LRU_BLOCKS * LRU_BLOCK
LRU_C = 8.0
LRU_CONV_COLS = 640
LANES = 128
SUBLANES = 8
VMEM_LIMIT_BYTES = 56 * 1024 * 1024
NEG_BIG = -1e30


def _cparams(*sem):
    return pltpu.CompilerParams(dimension_semantics=sem, vmem_limit_bytes=VMEM_LIMIT_BYTES)


def _pick_tile(n, candidates):
    for c in candidates:
        if n % c == 0:
            return c
    raise ValueError(f"no tile in {candidates} divides {n}")


def _const_spec(shape):
    nd = len(shape)
    return pl.BlockSpec(shape, lambda *_: (0,) * nd)


def _rms(x, g):
    ms = jnp.mean(x * x, axis=-1, keepdims=True)
    return x * lax.rsqrt(ms + EPS) * g


def _sigmoid(x):
    return 1.0 / (1.0 + jnp.exp(-x))


def _softplus(x):
    return jnp.maximum(x, 0.0) + jnp.log(1.0 + jnp.exp(-jnp.abs(x)))


def _dot(a, b):
    return jnp.dot(a, b, preferred_element_type=F32)


def _dot_nt(a, b):
    return lax.dot_general(a, b, (((1,), (1,)), ((), ())), preferred_element_type=F32)


def _dot_tn(a, b):
    return lax.dot_general(a, b, (((0,), (0,)), ((), ())), preferred_element_type=F32)


def _shift_rows(cur, tail, k):
    if k == 0:
        return cur
    rolled = pltpu.roll(cur, k, 0)
    row = lax.broadcasted_iota(jnp.int32, tail.shape, 0)
    top = jnp.where(row < k, pltpu.roll(tail, k, 0), rolled[0:SUBLANES])
    return jnp.concatenate([top, rolled[SUBLANES:]], axis=0)


def _causal_conv(cur, tail_scr, w_ref, b_ref, cols):
    k = w_ref.shape[0]
    tail = tail_scr[:, cols]
    out = b_ref[:, cols]
    for i in range(k):
        out = out + _shift_rows(cur, tail, k - 1 - i) * w_ref[i:i + 1, cols]
    tail_scr[:, cols] = cur[cur.shape[0] - SUBLANES:]
    return out


CONV_COLS = 512


def _proj_even_kernel(h_ref, g_ref, wz_ref, wxbc_ref, wdt_ref, wkrp_ref, wkrs_ref, wcq_ref, qg_ref,
                      wckv_ref, kvg_ref, wqp_ref, wqs_ref, wkp_ref, wv_ref, cw_ref, cb_ref,
                      cq_ref, sq_ref, ck_ref, sk_ref,
                      z_ref, xbc_ref, dt_ref, q_ref, k_ref, v_ref, tail_scr):
    @pl.when(pl.program_id(1) == 0)
    def _():
        tail_scr[...] = jnp.zeros_like(tail_scr)

    hn = _rms(h_ref[...], g_ref[...]).astype(BF16)
    z_ref[...] = _dot(hn, wz_ref[...]).astype(BF16)
    for c0 in range(0, SSD_CONV_CH, CONV_COLS):
        cols = slice(c0, c0 + CONV_COLS)
        conv = _causal_conv(_dot(hn, wxbc_ref[:, cols]), tail_scr, cw_ref, cb_ref, cols)
        xbc_ref[:, cols] = (conv * _sigmoid(conv)).astype(BF16)
    dt_ref[...] = _dot(hn, wdt_ref[...])
    kr = _dot(hn, wkrp_ref[...]) * ck_ref[...] + _dot(hn, wkrs_ref[...]) * sk_ref[...]
    cqn = _rms(_dot(hn, wcq_ref[...]), qg_ref[...]).astype(BF16)
    ckvn = _rms(_dot(hn, wckv_ref[...]), kvg_ref[...]).astype(BF16)
    v_ref[...] = _dot(ckvn, wv_ref[...]).astype(BF16)
    cq_t = jnp.concatenate([cq_ref[...]] * 2, axis=1)
    sq_t = jnp.concatenate([sq_ref[...]] * 2, axis=1)
    kr2 = jnp.concatenate([kr] * 2, axis=1)
    for hd in range(0, MLA_HEADS, 2):
        sl = slice(hd * LANES, (hd + 2) * LANES)
        qh = (_dot(cqn, wqp_ref[:, sl]) * cq_t + _dot(cqn, wqs_ref[:, sl]) * sq_t).astype(BF16)
        kh = (_dot(ckvn, wkp_ref[:, sl]) + kr2).astype(BF16)
        for j in range(2):
            q_ref[hd + j] = qh[:, j * LANES:(j + 1) * LANES]
            k_ref[hd + j] = kh[:, j * LANES:(j + 1) * LANES]


def _proj_even(h3, g, wts, tables):
    b, tp, d = h3.shape
    tm = _pick_tile(tp, (384, 128))
    nt = tp // tm
    n = b * tp
    row3 = lambda bi, i: (bi, i, 0)
    flat = lambda bi, i: (bi * nt + i, 0)
    tab = lambda bi, i: (i, 0)
    in_specs = [pl.BlockSpec((None, tm, d), row3), _const_spec(g.shape)]
    in_specs += [_const_spec(w.shape) for w in wts]
    in_specs += [pl.BlockSpec((tm, LANES), tab) for _ in tables]
    out_shape = (
        jax.ShapeDtypeStruct((n, SSD_D_INNER), BF16),
        jax.ShapeDtypeStruct((n, SSD_CONV_CH), BF16),
        jax.ShapeDtypeStruct((n, LANES), F32),
        jax.ShapeDtypeStruct((MLA_HEADS, n, LANES), BF16),
        jax.ShapeDtypeStruct((MLA_HEADS, n, LANES), BF16),
        jax.ShapeDtypeStruct((n, MLA_HEADS * MLA_V), BF16),
    )
    head3 = lambda bi, i: (0, bi * nt + i, 0)
    out_specs = (
        pl.BlockSpec((tm, SSD_D_INNER), flat),
        pl.BlockSpec((tm, SSD_CONV_CH), flat),
        pl.BlockSpec((tm, LANES), flat),
        pl.BlockSpec((MLA_HEADS, tm, LANES), head3),
        pl.BlockSpec((MLA_HEADS, tm, LANES), head3),
        pl.BlockSpec((tm, MLA_HEADS * MLA_V), flat),
    )
    return pl.pallas_call(
        _proj_even_kernel, grid=(b, nt), in_specs=in_specs, out_specs=out_specs, out_shape=out_shape,
        scratch_shapes=[pltpu.VMEM((SUBLANES, SSD_CONV_CH), F32)],
        compiler_params=_cparams("parallel", "arbitrary"), name="proj_even",
    )(h3, g, *wts, *tables)


def _cumsum_rows(x):
    n = x.shape[0]
    row = lax.broadcasted_iota(jnp.int32, x.shape, 0)
    d = 1
    while d < n:
        x = x + jnp.where(row >= d, pltpu.roll(x, d, 0), 0.0)
        d *= 2
    return x


SSD_BATCH_BLOCK = 4


def _ssd_chunk(c, xbc, dt_raw, zf, dtb, alog, dsk, ng, state_ref, y_scr):
    L = CHUNK
    heads_per_group = SSD_HEADS // SSD_GROUPS
    row = lax.broadcasted_iota(jnp.int32, (L, 1), 0)
    valid = (c * L + row) >= PAD
    act = jnp.where(valid, xbc.astype(F32), 0.0)
    xs = act[:, :SSD_D_INNER]
    bm = act[:, SSD_D_INNER:SSD_D_INNER + SSD_BC]
    cm = act[:, SSD_D_INNER + SSD_BC:]

    dtv = jnp.where(valid, _softplus(dt_raw + dtb), 0.0)
    a_cs = _cumsum_rows(dtv * (-jnp.exp(alog)))
    a_cs_t = a_cs.T
    dt_t = dtv.T
    tot = a_cs[L - 1:L, :]
    w_t = jnp.exp(a_cs_t[:, L - 1:L] - a_cs_t) * dt_t
    cd = jnp.exp(tot)
    causal = lax.broadcasted_iota(jnp.int32, (L, L), 0) >= lax.broadcasted_iota(jnp.int32, (L, L), 1)
    lo = lax.broadcasted_iota(jnp.int32, (L, LANES), 1) < SSD_HEAD_DIM

    for g in range(SSD_GROUPS):
        cc = cm[:, g * SSD_STATE:(g + 1) * SSD_STATE]
        bc = bm[:, g * SSD_STATE:(g + 1) * SSD_STATE]
        bc_t = bc.T
        cb = _dot_nt(cc.astype(BF16), bc.astype(BF16))
        for hp in range(heads_per_group // 2):
            pair = g * (heads_per_group // 2) + hp
            sl = slice(pair * LANES, (pair + 1) * LANES)
            xs_pair = xs[:, sl]
            st_pair = state_ref[:, sl]
            xs_b = xs_pair.astype(BF16)
            rhs = jnp.concatenate([xs_b, st_pair.astype(BF16)], axis=0)
            ys, sts, cds = [], [], []
            for hh in range(2):
                h = 2 * pair + hh
                a_col = jnp.broadcast_to(a_cs[:, h:h + 1], (L, L))
                dec = jnp.exp(jnp.where(causal, a_col - a_cs_t[h:h + 1, :], -jnp.inf))
                m = cb * dec * dt_t[h:h + 1, :]
                lhs = jnp.concatenate([m, cc * jnp.exp(a_col)], axis=1).astype(BF16)
                ys.append(_dot(lhs, rhs))
                sts.append(_dot((bc_t * w_t[h:h + 1, :]).astype(BF16), xs_b))
                cds.append(cd[:, h:h + 1])
            y_scr[:, sl] = jnp.where(lo, ys[0], ys[1]) + dsk[:, sl] * xs_pair
            state_ref[:, sl] = jnp.where(lo, cds[0], cds[1]) * st_pair + jnp.where(lo, sts[0], sts[1])

    return _rms(y_scr[...] * (zf * _sigmoid(zf)), ng).astype(BF16)


def _ssd_kernel(xbc_ref, dt_ref, z_ref, dtb_ref, alog_ref, dsk_ref, ng_ref, y_ref, state_scr, y_scr):
    c = pl.program_id(1)

    @pl.when(c == 0)
    def _():
        state_scr[...] = jnp.zeros_like(state_scr)

    for bb in range(xbc_ref.shape[0]):
        y_ref[bb] = _ssd_chunk(c, xbc_ref[bb], dt_ref[bb], z_ref[bb].astype(F32), dtb_ref[...], alog_ref[...],
                               dsk_ref[...], ng_ref[...], state_scr.at[bb], y_scr.at[bb])


def _ssd(xbc3, dt3, z3, dtb, alog, dsk, ng):
    b, tp, _ = xbc3.shape
    nc = tp // CHUNK
    gb = _pick_tile(b, (SSD_BATCH_BLOCK, 1))
    row3 = lambda bi, ci: (bi, ci, 0)
    in_specs = [
        pl.BlockSpec((gb, CHUNK, SSD_CONV_CH), row3),
        pl.BlockSpec((gb, CHUNK, LANES), row3),
        pl.BlockSpec((gb, CHUNK, SSD_D_INNER), row3),
    ] + [_const_spec(a.shape) for a in (dtb, alog, dsk, ng)]
    return pl.pallas_call(
        _ssd_kernel, grid=(b // gb, nc), in_specs=in_specs,
        out_specs=pl.BlockSpec((gb, CHUNK, SSD_D_INNER), row3),
        out_shape=jax.ShapeDtypeStruct((b, tp, SSD_D_INNER), BF16),
        scratch_shapes=[pltpu.VMEM((gb, SSD_STATE, SSD_D_INNER), F32),
                        pltpu.VMEM((gb, CHUNK, SSD_D_INNER), F32)],
        compiler_params=_cparams("parallel", "arbitrary"), name="ssd",
    )(xbc3, dt3, z3, dtb, alog, dsk, ng)


ATTN_KV_BLOCK = 256
ATTN_V_ROWS = MLA_V + 16


def _attn_kernel(q_ref, k_ref, v_ref, o_ref, vt_scr, s_scr, bmax_scr, p_scr, alpha_scr, m_scr, acc_scr, *, tq):
    kb = ATTN_KV_BLOCK
    qi = pl.program_id(2)
    q0 = qi * tq
    end = q0 + tq
    n_int = q0 // kb
    n_blocks = (end + kb - 1) // kb
    last_off = end - kb

    @pl.when(qi == 0)
    def _():
        ones = jnp.ones((ATTN_V_ROWS - MLA_V, CHUNK), BF16)
        for c in range(vt_scr.shape[0]):
            vt = v_ref[c * CHUNK:(c + 1) * CHUNK, :].astype(F32).T.astype(BF16)
            for hh in range(2):
                vt_scr[c, hh, :MLA_V, :] = vt[hh * MLA_V:(hh + 1) * MLA_V, :]
                vt_scr[c, hh, MLA_V:, :] = ones

    m_scr[...] = jnp.full_like(m_scr, NEG_BIG)
    acc_scr[...] = jnp.zeros_like(acc_scr)

    def offset(bk):
        return pl.multiple_of(jnp.minimum(bk * kb, last_off), CHUNK)

    def qk(hh, bk, masked):
        off = offset(bk)
        s = _dot_nt(k_ref[hh, pl.ds(off, kb), :], q_ref[hh])
        if masked:
            key = lax.broadcasted_iota(jnp.int32, (kb, tq), 0)
            keep = (key - lax.broadcasted_iota(jnp.int32, (kb, tq), 1)) <= (q0 - off)
            first_key = jnp.where(bk == 0, PAD, bk * kb - off)
            s = jnp.where(keep, s, NEG_BIG)
            s = jnp.where(key >= first_key, s, NEG_BIG)
        s_scr[hh] = s
        bmax_scr[hh] = jnp.broadcast_to(jnp.max(s, axis=0, keepdims=True), (SUBLANES, tq))

    def softmax(hh):
        m_prev = m_scr[hh, 0:1, :]
        m_new = jnp.maximum(m_prev, bmax_scr[hh, 0:1, :])
        alpha_scr[hh] = jnp.broadcast_to(jnp.exp2(m_prev - m_new), (SUBLANES, tq))
        m_scr[hh] = jnp.broadcast_to(m_new, (SUBLANES, tq))
        p_scr[hh] = jnp.exp2(s_scr[hh] - m_new).astype(BF16)

    def pv(hh, bk):
        c0 = offset(bk) // CHUNK
        vt = jnp.concatenate([vt_scr[c0 + c, hh] for c in range(kb // CHUNK)], axis=1)
        acc_scr[hh] = alpha_scr[hh, 0:1, :] * acc_scr[hh] + _dot(vt, p_scr[hh])

    def step(bk, masked):
        softmax(1)
        qk(0, bk, masked)
        pv(0, bk - 1)
        softmax(0)
        qk(1, bk, masked)
        pv(1, bk - 1)

    def plain_step(bk, carry):
        step(bk, False)
        return carry

    def masked_step(bk, carry):
        step(bk, True)
        return carry

    qk(0, 0, True)
    softmax(0)
    qk(1, 0, True)
    lax.fori_loop(1, n_int, plain_step, 0)
    lax.fori_loop(jnp.maximum(n_int, 1), n_blocks, masked_step, 0)
    softmax(1)
    pv(0, n_blocks - 1)
    pv(1, n_blocks - 1)

    out_t = jnp.concatenate([acc_scr[hh, :MLA_V, :] / acc_scr[hh, MLA_V:MLA_V + 1, :] for hh in range(2)],
                            axis=0)
    qpos = q0 + lax.broadcasted_iota(jnp.int32, (1, tq), 1)
    out_t = jnp.where(qpos >= PAD, out_t, 0.0)
    for c in range(tq // CHUNK):
        o_ref[c * CHUNK:(c + 1) * CHUNK, :] = out_t[:, c * CHUNK:(c + 1) * CHUNK].T.astype(BF16)


def _attention(q, k, v, b, tp):
    tq = _pick_tile(tp, (1408, 384, 256))
    nq = tp // tq
    n = b * tp
    in_specs = [
        pl.BlockSpec((2, tq, LANES), lambda bi, hp, qi: (hp, bi * nq + qi, 0)),
        pl.BlockSpec((2, tp, LANES), lambda bi, hp, qi: (hp, bi, 0)),
        pl.BlockSpec((tp, LANES), lambda bi, hp, qi: (bi, hp)),
    ]
    return pl.pallas_call(
        functools.partial(_attn_kernel, tq=tq), grid=(b, MLA_HEADS // 2, nq), in_specs=in_specs,
        out_specs=pl.BlockSpec((tq, LANES), lambda bi, hp, qi: (bi * nq + qi, hp)),
        out_shape=jax.ShapeDtypeStruct((n, MLA_HEADS * MLA_V), BF16),
        scratch_shapes=[pltpu.VMEM((tp // CHUNK, 2, ATTN_V_ROWS, CHUNK), BF16),
                        pltpu.VMEM((2, ATTN_KV_BLOCK, tq), F32), pltpu.VMEM((2, SUBLANES, tq), F32),
                        pltpu.VMEM((2, ATTN_KV_BLOCK, tq), BF16),
                        pltpu.VMEM((2, SUBLANES, tq), F32), pltpu.VMEM((2, SUBLANES, tq), F32),
                        pltpu.VMEM((2, ATTN_V_ROWS, tq), F32)],
        compiler_params=_cparams("parallel", "parallel", "arbitrary"), name="attention",
    )(q, k, v)


def _post_kernel(*refs, n_in):
    h_ref = refs[0]
    a_refs = refs[1:1 + n_in]
    w_refs = refs[1 + n_in:1 + 2 * n_in]
    g_ref = refs[1 + 2 * n_in]
    o_ref = refs[2 + 2 * n_in]
    m = _dot(a_refs[0][...], w_refs[0][...])
    for a_ref, w_ref in zip(a_refs[1:], w_refs[1:]):
        m = m + _dot(a_ref[...], w_ref[...])
    o_ref[...] = h_ref[...] + _rms(m, g_ref[...])


def _post(h2, acts, wts, g):
    n, d = h2.shape
    tm = _pick_tile(n, (512, 384, 256, 128))
    rows = lambda i: (i, 0)
    in_specs = [pl.BlockSpec((tm, d), rows)]
    in_specs += [pl.BlockSpec((tm, a.shape[1]), rows) for a in acts]
    in_specs += [_const_spec(w.shape) for w in wts] + [_const_spec(g.shape)]
    return pl.pallas_call(
        functools.partial(_post_kernel, n_in=len(acts)), grid=(n // tm,), in_specs=in_specs,
        out_specs=pl.BlockSpec((tm, d), rows), out_shape=jax.ShapeDtypeStruct((n, d), F32),
        compiler_params=_cparams("parallel"), name="mixer_out",
    )(h2, *acts, *wts, g)


def _mlp_kernel(h_ref, g1_ref, wup_ref, wdn_ref, g2_ref, o_ref, *, ff_chunk):
    x = h_ref[...]
    hn = _rms(x, g1_ref[...]).astype(BF16)
    d_ff = wup_ref.shape[1]
    y = None
    for c0 in range(0, d_ff, ff_chunk):
        u = jnp.maximum(_dot(hn, wup_ref[:, c0:c0 + ff_chunk]), 0.0)
        part = _dot((u * u).astype(BF16), wdn_ref[c0:c0 + ff_chunk, :])
        y = part if y is None else y + part
    o_ref[...] = x + _rms(y, g2_ref[...])


def _mlp(h2, g1, wup, wdn, g2):
    n, d = h2.shape
    tm = _pick_tile(n, (512, 256, 128))
    rows = lambda i: (i, 0)
    in_specs = [pl.BlockSpec((tm, d), rows), _const_spec(g1.shape), _const_spec(wup.shape),
                _const_spec(wdn.shape), _const_spec(g2.shape)]
    return pl.pallas_call(
        functools.partial(_mlp_kernel, ff_chunk=1024), grid=(n // tm,), in_specs=in_specs,
        out_specs=pl.BlockSpec((tm, d), rows), out_shape=jax.ShapeDtypeStruct((n, d), F32),
        compiler_params=_cparams("parallel"), name="mlp",
    )(h2, g1, wup, wdn, g2)


def _proj_odd_kernel(h_ref, g_ref, wx_ref, wy_ref, cw_ref, cb_ref, x_ref, gate_ref, tail_scr):
    @pl.when(pl.program_id(0) == 0)
    def _():
        tail_scr[...] = jnp.zeros_like(tail_scr)

    hn = _rms(h_ref[...], g_ref[...]).astype(BF16)
    for c0 in range(0, LRU_WIDTH, LRU_CONV_COLS):
        cols = slice(c0, c0 + LRU_CONV_COLS)
        x_ref[:, cols] = _causal_conv(_dot(hn, wx_ref[:, cols]), tail_scr, cw_ref, cb_ref, cols).astype(BF16)
    y = _dot(hn, wy_ref[...])
    gelu = 0.5 * y * (1.0 + jnp.tanh(np.sqrt(2.0 / np.pi).astype(np.float32) * (y + 0.044715 * (y * y * y))))
    gate_ref[...] = gelu.astype(BF16)


def _proj_odd(h2, g, wx, wy, cw, cb):
    n, d = h2.shape
    tm = _pick_tile(n, (512, 384, 256, 128))
    rows = lambda i: (i, 0)
    w = wx.shape[1]
    return pl.pallas_call(
        _proj_odd_kernel, grid=(n // tm,),
        in_specs=[pl.BlockSpec((tm, d), rows)] + [_const_spec(a.shape) for a in (g, wx, wy, cw, cb)],
        out_specs=(pl.BlockSpec((tm, w), rows), pl.BlockSpec((tm, w), rows)),
        out_shape=(jax.ShapeDtypeStruct((n, w), BF16), jax.ShapeDtypeStruct((n, w), BF16)),
        scratch_shapes=[pltpu.VMEM((SUBLANES, w), F32)],
        compiler_params=_cparams("arbitrary"), name="proj_odd",
    )(h2, g, wx, wy, cw, cb)


def _rglru_kernel(x_ref, gate_ref, wa_ref, ba_ref, wi_ref, bi_ref, lam_ref, o_ref, h_scr):
    t = pl.program_id(1)
    L = x_ref.shape[0]

    @pl.when(t == 0)
    def _():
        h_scr[...] = jnp.zeros_like(h_scr)

    xr = x_ref[...].astype(F32)
    rs, gs = [], []
    for nb in range(LRU_BLOCKS):
        sl = slice(nb * LRU_BLOCK, (nb + 1) * LRU_BLOCK)
        xb = x_ref[:, sl]
        rs.append(_dot(xb, wa_ref[nb]))
        gs.append(_dot(xb, wi_ref[nb]))
    r = _sigmoid(jnp.concatenate(rs, axis=1) + ba_ref[...])
    ig = _sigmoid(jnp.concatenate(gs, axis=1) + bi_ref[...])
    log_a = (-LRU_C) * r * _softplus(-lam_ref[...])
    a = jnp.exp(log_a)
    row = lax.broadcasted_iota(jnp.int32, (L, 1), 0)
    valid = (t * L + row) >= PAD
    om = 1.0 - a * a
    root = jnp.where(om > 0.0, om * lax.rsqrt(om), 0.0)
    u = jnp.where(valid, root * (ig * xr), 0.0)

    groups = L // SUBLANES
    a3 = a.reshape(groups, SUBLANES, LRU_WIDTH)
    u3 = u.reshape(groups, SUBLANES, LRU_WIDTH)
    sub = lax.broadcasted_iota(jnp.int32, a3.shape, 1)
    d = 1
    while d < SUBLANES:
        keep = sub >= d
        a_s = jnp.where(keep, pltpu.roll(a3, d, 1), 1.0)
        u_s = jnp.where(keep, pltpu.roll(u3, d, 1), 0.0)
        u3 = a3 * u_s + u3
        a3 = a3 * a_s
        d *= 2
    h_prev = h_scr[0:1, :]
    hs = []
    for gi in range(groups):
        hs.append(u3[gi] + a3[gi] * h_prev)
        h_prev = hs[-1][SUBLANES - 1:SUBLANES, :]
    h_scr[...] = jnp.broadcast_to(h_prev, h_scr.shape)
    o_ref[...] = (jnp.concatenate(hs, axis=0) * gate_ref[...].astype(F32)).astype(BF16)


def _rglru(x3, gate3, wa, ba, wi, bi, lam):
    b, tp, w = x3.shape
    tt = CHUNK
    row3 = lambda bi_, ti: (bi_, ti, 0)
    in_specs = [pl.BlockSpec((None, tt, w), row3), pl.BlockSpec((None, tt, w), row3)]
    in_specs += [_const_spec(a.shape) for a in (wa, ba, wi, bi, lam)]
    return pl.pallas_call(
        _rglru_kernel, grid=(b, tp // tt), in_specs=in_specs,
        out_specs=pl.BlockSpec((None, tt, w), row3),
        out_shape=jax.ShapeDtypeStruct((b, tp, w), BF16),
        scratch_shapes=[pltpu.VMEM((SUBLANES, w), F32)],
        compiler_params=_cparams("parallel", "arbitrary"), name="rglru",
    )(x3, gate3, wa, ba, wi, bi, lam)


def _rope_tables(tp):
    inv = ROPE_BASE ** (-jnp.arange(0, MLA_ROPE, 2, dtype=F32) / MLA_ROPE)
    pos = jnp.maximum(jnp.arange(tp, dtype=F32) - PAD, 0.0)
    ang = pos[:, None] * inv[None, :]
    cos, sin = jnp.cos(ang), jnp.sin(ang)
    zeros_lo = jnp.zeros((tp, MLA_NOPE), F32)
    zeros_hi = jnp.zeros((tp, LANES - MLA_NOPE - MLA_ROPE), F32)
    ck = jnp.concatenate([zeros_lo, cos, cos, zeros_hi], axis=1)
    sk = jnp.concatenate([zeros_lo, sin, sin, zeros_hi], axis=1)
    scale = (MLA_NOPE + MLA_ROPE) ** -0.5 * np.log2(np.e)
    cq = jnp.concatenate([jnp.ones((tp, MLA_NOPE), F32), cos, cos, zeros_hi], axis=1) * scale
    sq = sk * scale
    return cq, sq, ck, sk


def _even_weights(w_in, w_q_up, w_kv_up):
    offs = np.cumsum((SSD_D_INNER, SSD_CONV_CH, SSD_HEADS, MLA_Q_RANK, MLA_KV_RANK, MLA_ROPE))
    wz = w_in[:, :offs[0]]
    wxbc = w_in[:, offs[0]:offs[1]]
    wdt = w_in[:, offs[1]:offs[2]]
    wcq = w_in[:, offs[2]:offs[3]]
    wckv = w_in[:, offs[3]:offs[4]]
    wkr = w_in[:, offs[4]:offs[5]]
    half = MLA_ROPE // 2
    hi_pad = LANES - MLA_NOPE - MLA_ROPE
    wdt_p = jnp.pad(wdt, ((0, 0), (0, LANES - SSD_HEADS)))
    wkr_p = jnp.pad(wkr, ((0, 0), (MLA_NOPE, hi_pad)))
    wkr_s = jnp.pad(jnp.concatenate([-wkr[:, half:], wkr[:, :half]], axis=1), ((0, 0), (MLA_NOPE, hi_pad)))
    wq = w_q_up.reshape(MLA_Q_RANK, MLA_HEADS, MLA_NOPE + MLA_ROPE)
    q_nope, q_rope = wq[..., :MLA_NOPE], wq[..., MLA_NOPE:]
    wq_p = jnp.pad(wq, ((0, 0), (0, 0), (0, hi_pad))).reshape(MLA_Q_RANK, MLA_HEADS * LANES)
    q_swap = jnp.concatenate([jnp.zeros_like(q_nope), -q_rope[..., half:], q_rope[..., :half]], axis=-1)
    wq_s = jnp.pad(q_swap, ((0, 0), (0, 0), (0, hi_pad))).reshape(MLA_Q_RANK, MLA_HEADS * LANES)
    wkv = w_kv_up.reshape(MLA_KV_RANK, MLA_HEADS, MLA_NOPE + MLA_V)
    wk_p = jnp.pad(wkv[..., :MLA_NOPE], ((0, 0), (0, 0), (0, LANES - MLA_NOPE))).reshape(
        MLA_KV_RANK, MLA_HEADS * LANES)
    wv = wkv[..., MLA_NOPE:].reshape(MLA_KV_RANK, MLA_HEADS * MLA_V)
    bf = lambda a: a.astype(BF16)
    return dict(wz=bf(wz), wxbc=bf(wxbc), wdt=bf(wdt_p), wkr_p=bf(wkr_p), wkr_s=bf(wkr_s), wcq=bf(wcq),
                wckv=bf(wckv), wq_p=bf(wq_p), wq_s=bf(wq_s), wk_p=bf(wk_p), wv=bf(wv))


def _row(a):
    return a.reshape(1, -1).astype(F32)


def kernel(x, meta_tokens, mix_pre_g, mix_post_g, mlp_pre_g, mlp_post_g, w_up, w_down, w_in, ssd_conv_w,
           ssd_conv_b, ssd_dt_bias, ssd_a_log, ssd_d, ssd_norm_g, mla_q_norm_g, mla_w_q_up, mla_kv_norm_g,
           mla_w_kv_up, w_out_ab, rg_w_x, rg_w_y, rg_conv_w, rg_conv_b, rg_w_a, rg_b_a, rg_w_i, rg_b_i,
           rg_lambda, rg_w_out):
    b, seq, d = x.shape
    depth = mix_pre_g.shape[0]
    tp = PAD + N_META + seq
    n = b * tp
    meta = jnp.broadcast_to(meta_tokens[None].astype(x.dtype), (b, N_META, d))
    h = jnp.concatenate([jnp.zeros((b, PAD, d), x.dtype), meta, x], axis=1).reshape(n, d)
    tables = _rope_tables(tp)

    for layer in range(depth):
        if layer % 2 == 0:
            e = layer // 2
            w = _even_weights(w_in[e], mla_w_q_up[e], mla_w_kv_up[e])
            wts = (w["wz"], w["wxbc"], w["wdt"], w["wkr_p"], w["wkr_s"], w["wcq"], _row(mla_q_norm_g[e]),
                   w["wckv"], _row(mla_kv_norm_g[e]), w["wq_p"], w["wq_s"], w["wk_p"], w["wv"],
                   ssd_conv_w[e].astype(F32), _row(ssd_conv_b[e]))
            z, xbc, dt, q, k, v = _proj_even(h.reshape(b, tp, d), _row(mix_pre_g[layer]), wts, tables)
            lane_pad = lambda a: jnp.pad(_row(a), ((0, 0), (0, LANES - SSD_HEADS)))
            y_ssd = _ssd(xbc.reshape(b, tp, SSD_CONV_CH), dt.reshape(b, tp, LANES), z.reshape(b, tp, SSD_D_INNER),
                         lane_pad(ssd_dt_bias[e]), lane_pad(ssd_a_log[e]),
                         _row(jnp.repeat(ssd_d[e], SSD_HEAD_DIM)), _row(ssd_norm_g[e]))
            y_att = _attention(q, k, v, b, tp)
            wo = w_out_ab[e].astype(BF16)
            h = _post(h, (y_ssd.reshape(n, SSD_D_INNER), y_att), (wo[:SSD_D_INNER], wo[SSD_D_INNER:]),
                      _row(mix_post_g[layer]))
        else:
            o = layer // 2
            xr, gate = _proj_odd(h, _row(mix_pre_g[layer]), rg_w_x[o].astype(BF16), rg_w_y[o].astype(BF16),
                                 rg_conv_w[o].astype(F32), _row(rg_conv_b[o]))
            y = _rglru(xr.reshape(b, tp, LRU_WIDTH), gate.reshape(b, tp, LRU_WIDTH), rg_w_a[o].astype(BF16),
                       _row(rg_b_a[o]), rg_w_i[o].astype(BF16), _row(rg_b_i[o]), _row(rg_lambda[o]))
            h = _post(h, (y.reshape(n, LRU_WIDTH),), (rg_w_out[o].astype(BF16),), _row(mix_post_g[layer]))
        h = _mlp(h, _row(mlp_pre_g[layer]), w_up[layer].astype(BF16), w_down[layer].astype(BF16),
                 _row(mlp_post_g[layer]))
    return h.reshape(b, tp, d)[:, PAD + N_META:]
```

```python
import functools

import jax
import jax.numpy as jnp
import numpy as np
from jax import lax
from jax.experimental import pallas as pl
from jax.experimental.pallas import tpu as pltpu

F32 = jnp.float32
BF16 = jnp.bfloat16

N_META = 16
CHUNK = 128
PAD = CHUNK - N_META
EPS = 1e-6
SSD_HEADS = 16
SSD_HEAD_DIM = 64
SSD_D_INNER = SSD_HEADS * SSD_HEAD_DIM
SSD_GROUPS = 2
SSD_STATE = 128
SSD_CONV = 4
SSD_BC = SSD_GROUPS * SSD_STATE
SSD_CONV_CH = SSD_D_INNER + 2 * SSD_BC
MLA_HEADS = 16
MLA_NOPE = 64
MLA_ROPE = 32
MLA_V = 64
MLA_Q_RANK = 384
MLA_KV_RANK = 256
ROPE_BASE = 10000.0
LRU_BLOCKS = 10
LRU_BLOCK = 128
LRU_WIDTH = LRU_BLOCKS * LRU_BLOCK
LRU_C = 8.0
LRU_CONV_COLS = 640
LANES = 128
SUBLANES = 8
VMEM_LIMIT_BYTES = 56 * 1024 * 1024
NEG_BIG = -1e30


def _cparams(*sem):
    return pltpu.CompilerParams(dimension_semantics=sem, vmem_limit_bytes=VMEM_LIMIT_BYTES)


def _pick_tile(n, candidates):
    for c in candidates:
        if n % c == 0:
            return c
    raise ValueError(f"no tile in {candidates} divides {n}")


def _const_spec(shape):
    nd = len(shape)
    return pl.BlockSpec(shape, lambda *_: (0,) * nd)


def _rms(x, g):
    ms = jnp.mean(x * x, axis=-1, keepdims=True)
    return x * lax.rsqrt(ms + EPS) * g


def _sigmoid(x):
    return 1.0 / (1.0 + jnp.exp(-x))


def _softplus(x):
    return jnp.maximum(x, 0.0) + jnp.log(1.0 + jnp.exp(-jnp.abs(x)))


def _dot(a, b):
    return jnp.dot(a, b, preferred_element_type=F32)


def _dot_nt(a, b):
    return lax.dot_general(a, b, (((1,), (1,)), ((), ())), preferred_element_type=F32)


def _dot_tn(a, b):
    return lax.dot_general(a, b, (((0,), (0,)), ((), ())), preferred_element_type=F32)


def _shift_rows(cur, tail, k):
    if k == 0:
        return cur
    rolled = pltpu.roll(cur, k, 0)
    row = lax.broadcasted_iota(jnp.int32, tail.shape, 0)
    top = jnp.where(row < k, pltpu.roll(tail, k, 0), rolled[0:SUBLANES])
    return jnp.concatenate([top, rolled[SUBLANES:]], axis=0)


def _causal_conv(cur, tail_scr, w_ref, b_ref, cols):
    k = w_ref.shape[0]
    tail = tail_scr[:, cols]
    out = b_ref[:, cols]
    for i in range(k):
        out = out + _shift_rows(cur, tail, k - 1 - i) * w_ref[i:i + 1, cols]
    tail_scr[:, cols] = cur[cur.shape[0] - SUBLANES:]
    return out


CONV_COLS = 512


def _proj_even_kernel(h_ref, g_ref, wz_ref, wxbc_ref, wdt_ref, wkrp_ref, wkrs_ref, wcq_ref, qg_ref,
                      wckv_ref, kvg_ref, wqp_ref, wqs_ref, wkp_ref, wv_ref, cw_ref, cb_ref,
                      cq_ref, sq_ref, ck_ref, sk_ref,
                      z_ref, xbc_ref, dt_ref, q_ref, k_ref, v_ref, tail_scr):
    @pl.when(pl.program_id(1) == 0)
    def _():
        tail_scr[...] = jnp.zeros_like(tail_scr)

    hn = _rms(h_ref[...], g_ref[...]).astype(BF16)
    z_ref[...] = _dot(hn, wz_ref[...]).astype(BF16)
    for c0 in range(0, SSD_CONV_CH, CONV_COLS):
        cols = slice(c0, c0 + CONV_COLS)
        conv = _causal_conv(_dot(hn, wxbc_ref[:, cols]), tail_scr, cw_ref, cb_ref, cols)
        xbc_ref[:, cols] = (conv * _sigmoid(conv)).astype(BF16)
    dt_ref[...] = _dot(hn, wdt_ref[...])
    kr = _dot(hn, wkrp_ref[...]) * ck_ref[...] + _dot(hn, wkrs_ref[...]) * sk_ref[...]
    cqn = _rms(_dot(hn, wcq_ref[...]), qg_ref[...]).astype(BF16)
    ckvn = _rms(_dot(hn, wckv_ref[...]), kvg_ref[...]).astype(BF16)
    v_ref[...] = _dot(ckvn, wv_ref[...]).astype(BF16)
    cq_t = jnp.concatenate([cq_ref[...]] * 2, axis=1)
    sq_t = jnp.concatenate([sq_ref[...]] * 2, axis=1)
    kr2 = jnp.concatenate([kr] * 2, axis=1)
    for hd in range(0, MLA_HEADS, 2):
        sl = slice(hd * LANES, (hd + 2) * LANES)
        qh = (_dot(cqn, wqp_ref[:, sl]) * cq_t + _dot(cqn, wqs_ref[:, sl]) * sq_t).astype(BF16)
        kh = (_dot(ckvn, wkp_ref[:, sl]) + kr2).astype(BF16)
        for j in range(2):
            q_ref[hd + j] = qh[:, j * LANES:(j + 1) * LANES]
            k_ref[hd + j] = kh[:, j * LANES:(j + 1) * LANES]


def _proj_even(h3, g, wts, tables):
    b, tp, d = h3.shape
    tm = _pick_tile(tp, (384, 128))
    nt = tp // tm
    n = b * tp
    row3 = lambda bi, i: (bi, i, 0)
    flat = lambda bi, i: (bi * nt + i, 0)
    tab = lambda bi, i: (i, 0)
    in_specs = [pl.BlockSpec((None, tm, d), row3), _const_spec(g.shape)]
    in_specs += [_const_spec(w.shape) for w in wts]
    in_specs += [pl.BlockSpec((tm, LANES), tab) for _ in tables]
    out_shape = (
        jax.ShapeDtypeStruct((n, SSD_D_INNER), BF16),
        jax.ShapeDtypeStruct((n, SSD_CONV_CH), BF16),
        jax.ShapeDtypeStruct((n, LANES), F32),
        jax.ShapeDtypeStruct((MLA_HEADS, n, LANES), BF16),
        jax.ShapeDtypeStruct((MLA_HEADS, n, LANES), BF16),
        jax.ShapeDtypeStruct((n, MLA_HEADS * MLA_V), BF16),
    )
    head3 = lambda bi, i: (0, bi * nt + i, 0)
    out_specs = (
        pl.BlockSpec((tm, SSD_D_INNER), flat),
        pl.BlockSpec((tm, SSD_CONV_CH), flat),
        pl.BlockSpec((tm, LANES), flat),
        pl.BlockSpec((MLA_HEADS, tm, LANES), head3),
        pl.BlockSpec((MLA_HEADS, tm, LANES), head3),
        pl.BlockSpec((tm, MLA_HEADS * MLA_V), flat),
    )
    return pl.pallas_call(
        _proj_even_kernel, grid=(b, nt), in_specs=in_specs, out_specs=out_specs, out_shape=out_shape,
        scratch_shapes=[pltpu.VMEM((SUBLANES, SSD_CONV_CH), F32)],
        compiler_params=_cparams("parallel", "arbitrary"), name="proj_even",
    )(h3, g, *wts, *tables)


def _cumsum_rows(x):
    n = x.shape[0]
    row = lax.broadcasted_iota(jnp.int32, x.shape, 0)
    d = 1
    while d < n:
        x = x + jnp.where(row >= d, pltpu.roll(x, d, 0), 0.0)
        d *= 2
    return x


SSD_BATCH_BLOCK = 4


def _ssd_chunk(c, xbc, dt_raw, zf, dtb, alog, dsk, ng, state_ref, y_scr):
    L = CHUNK
    heads_per_group = SSD_HEADS // SSD_GROUPS
    row = lax.broadcasted_iota(jnp.int32, (L, 1), 0)
    valid = (c * L + row) >= PAD
    act = jnp.where(valid, xbc.astype(F32), 0.0)
    xs = act[:, :SSD_D_INNER]
    bm = act[:, SSD_D_INNER:SSD_D_INNER + SSD_BC]
    cm = act[:, SSD_D_INNER + SSD_BC:]

    dtv = jnp.where(valid, _softplus(dt_raw + dtb), 0.0)
    a_cs = _cumsum_rows(dtv * (-jnp.exp(alog)))
    a_cs_t = a_cs.T
    dt_t = dtv.T
    tot = a_cs[L - 1:L, :]
    w_t = jnp.exp(a_cs_t[:, L - 1:L] - a_cs_t) * dt_t
    cd = jnp.exp(tot)
    causal = lax.broadcasted_iota(jnp.int32, (L, L), 0) >= lax.broadcasted_iota(jnp.int32, (L, L), 1)
    lo = lax.broadcasted_iota(jnp.int32, (L, LANES), 1) < SSD_HEAD_DIM

    for g in range(SSD_GROUPS):
        cc = cm[:, g * SSD_STATE:(g + 1) * SSD_STATE]
        bc = bm[:, g * SSD_STATE:(g + 1) * SSD_STATE]
        bc_t = bc.T
        cb = _dot_nt(cc.astype(BF16), bc.astype(BF16))
        for hp in range(heads_per_group // 2):
            pair = g * (heads_per_group // 2) + hp
            sl = slice(pair * LANES, (pair + 1) * LANES)
            xs_pair = xs[:, sl]
            st_pair = state_ref[:, sl]
            xs_b = xs_pair.astype(BF16)
            rhs = jnp.concatenate([xs_b, st_pair.astype(BF16)], axis=0)
            ys, sts, cds = [], [], []
            for hh in range(2):
                h = 2 * pair + hh
                a_col = jnp.broadcast_to(a_cs[:, h:h + 1], (L, L))
                dec = jnp.exp(jnp.where(causal, a_col - a_cs_t[h:h + 1, :], -jnp.inf))
                m = cb * dec * dt_t[h:h + 1, :]
                lhs = jnp.concatenate([m, cc * jnp.exp(a_col)], axis=1).astype(BF16)
                ys.append(_dot(lhs, rhs))
                sts.append(_dot((bc_t * w_t[h:h + 1, :]).astype(BF16), xs_b))
                cds.append(cd[:, h:h + 1])
            y_scr[:, sl] = jnp.where(lo, ys[0], ys[1]) + dsk[:, sl] * xs_pair
            state_ref[:, sl] = jnp.where(lo, cds[0], cds[1]) * st_pair + jnp.where(lo, sts[0], sts[1])

    return _rms(y_scr[...] * (zf * _sigmoid(zf)), ng).astype(BF16)


def _ssd_kernel(xbc_ref, dt_ref, z_ref, dtb_ref, alog_ref, dsk_ref, ng_ref, y_ref, state_scr, y_scr):
    c = pl.program_id(1)

    @pl.when(c == 0)
    def _():
        state_scr[...] = jnp.zeros_like(state_scr)

    for bb in range(xbc_ref.shape[0]):
        y_ref[bb] = _ssd_chunk(c, xbc_ref[bb], dt_ref[bb], z_ref[bb].astype(F32), dtb_ref[...], alog_ref[...],
                               dsk_ref[...], ng_ref[...], state_scr.at[bb], y_scr.at[bb])


def _ssd(xbc3, dt3, z3, dtb, alog, dsk, ng):
    b, tp, _ = xbc3.shape
    nc = tp // CHUNK
    gb = _pick_tile(b, (SSD_BATCH_BLOCK, 1))
    row3 = lambda bi, ci: (bi, ci, 0)
    in_specs = [
        pl.BlockSpec((gb, CHUNK, SSD_CONV_CH), row3),
        pl.BlockSpec((gb, CHUNK, LANES), row3),
        pl.BlockSpec((gb, CHUNK, SSD_D_INNER), row3),
    ] + [_const_spec(a.shape) for a in (dtb, alog, dsk, ng)]
    return pl.pallas_call(
        _ssd_kernel, grid=(b // gb, nc), in_specs=in_specs,
        out_specs=pl.BlockSpec((gb, CHUNK, SSD_D_INNER), row3),
        out_shape=jax.ShapeDtypeStruct((b, tp, SSD_D_INNER), BF16),
        scratch_shapes=[pltpu.VMEM((gb, SSD_STATE, SSD_D_INNER), F32),
                        pltpu.VMEM((gb, CHUNK, SSD_D_INNER), F32)],
        compiler_params=_cparams("parallel", "arbitrary"), name="ssd",
    )(xbc3, dt3, z3, dtb, alog, dsk, ng)


ATTN_KV_BLOCK = 256
ATTN_V_ROWS = MLA_V + 16


def _attn_kernel(q_ref, k_ref, v_ref, o_ref, vt_scr, s_scr, bmax_scr, p_scr, alpha_scr, m_scr, acc_scr, *, tq):
    kb = ATTN_KV_BLOCK
    qi = pl.program_id(2)
    q0 = qi * tq
    end = q0 + tq
    n_int = q0 // kb
    n_blocks = (end + kb - 1) // kb
    last_off = end - kb

    @pl.when(qi == 0)
    def _():
        ones = jnp.ones((ATTN_V_ROWS - MLA_V, CHUNK), BF16)
        for c in range(vt_scr.shape[0]):
            vt = v_ref[c * CHUNK:(c + 1) * CHUNK, :].astype(F32).T.astype(BF16)
            for hh in range(2):
                vt_scr[c, hh, :MLA_V, :] = vt[hh * MLA_V:(hh + 1) * MLA_V, :]
                vt_scr[c, hh, MLA_V:, :] = ones

    m_scr[...] = jnp.full_like(m_scr, NEG_BIG)
    acc_scr[...] = jnp.zeros_like(acc_scr)

    def offset(bk):
        return pl.multiple_of(jnp.minimum(bk * kb, last_off), CHUNK)

    levels = tuple(range(0, tq - kb + 1, kb))

    def level(bk):
        return jnp.minimum(jnp.maximum(offset(bk) - q0, 0) // kb, len(levels) - 1) * kb

    def qk(hh, bk, masked, cs):
        off = offset(bk)
        s = _dot_nt(k_ref[hh, pl.ds(off, kb), :], q_ref[hh, cs:, :])
        if masked:
            key = lax.broadcasted_iota(jnp.int32, s.shape, 0)
            keep = (key - lax.broadcasted_iota(jnp.int32, s.shape, 1)) <= (q0 + cs - off)
            first_key = jnp.where(bk == 0, PAD, bk * kb - off)
            s = jnp.where(keep, s, NEG_BIG)
            s = jnp.where(key >= first_key, s, NEG_BIG)
        s_scr[hh, :, cs:] = s
        bmax_scr[hh, :, cs:] = jnp.broadcast_to(jnp.max(s, axis=0, keepdims=True), (SUBLANES, tq - cs))

    def softmax(hh, cs):
        m_prev = m_scr[hh, 0:1, cs:]
        m_new = jnp.maximum(m_prev, bmax_scr[hh, 0:1, cs:])
        alpha_scr[hh, :, cs:] = jnp.broadcast_to(jnp.exp2(m_prev - m_new), (SUBLANES, tq - cs))
        m_scr[hh, :, cs:] = jnp.broadcast_to(m_new, (SUBLANES, tq - cs))
        p_scr[hh, :, cs:] = jnp.exp2(s_scr[hh, :, cs:] - m_new).astype(BF16)

    def pv(hh, bk, cs):
        c0 = offset(bk) // CHUNK
        vt = jnp.concatenate([vt_scr[c0 + c, hh] for c in range(kb // CHUNK)], axis=1)
        acc_scr[hh, :, cs:] = alpha_scr[hh, 0:1, cs:] * acc_scr[hh, :, cs:] + _dot(vt, p_scr[hh, :, cs:])

    def step(bk, masked, cs_prev, cs):
        softmax(1, cs_prev)
        qk(0, bk, masked, cs)
        pv(0, bk - 1, cs_prev)
        softmax(0, cs)
        qk(1, bk, masked, cs)
        pv(1, bk - 1, cs_prev)

    def plain_step(bk, carry):
        step(bk, False, 0, 0)
        return carry

    def masked_step(bk, carry):
        lvl_prev, lvl = level(bk - 1), level(bk)
        for a in levels:
            for b in (a, a + kb):
                if b in levels:
                    @pl.when(jnp.logical_and(lvl_prev == a, lvl == b))
                    def _():
                        step(bk, True, a, b)
        return carry

    qk(0, 0, True, 0)
    softmax(0, 0)
    qk(1, 0, True, 0)
    lax.fori_loop(1, n_int, plain_step, 0)
    lax.fori_loop(jnp.maximum(n_int, 1), n_blocks, masked_step, 0)
    lvl_last = level(n_blocks - 1)
    for a in levels:
        @pl.when(lvl_last == a)
        def _():
            softmax(1, a)
            pv(0, n_blocks - 1, a)
            pv(1, n_blocks - 1, a)

    out_t = jnp.concatenate([acc_scr[hh, :MLA_V, :] / acc_scr[hh, MLA_V:MLA_V + 1, :] for hh in range(2)],
                            axis=0)
    qpos = q0 + lax.broadcasted_iota(jnp.int32, (1, tq), 1)
    out_t = jnp.where(qpos >= PAD, out_t, 0.0)
    for c in range(tq // CHUNK):
        o_ref[c * CHUNK:(c + 1) * CHUNK, :] = out_t[:, c * CHUNK:(c + 1) * CHUNK].T.astype(BF16)


def _attention(q, k, v, b, tp):
    tq = _pick_tile(tp, (1408, 384, 256))
    nq = tp // tq
    n = b * tp
    in_specs = [
        pl.BlockSpec((2, tq, LANES), lambda bi, hp, qi: (hp, bi * nq + qi, 0)),
        pl.BlockSpec((2, tp, LANES), lambda bi, hp, qi: (hp, bi, 0)),
        pl.BlockSpec((tp, LANES), lambda bi, hp, qi: (bi, hp)),
    ]
    return pl.pallas_call(
        functools.partial(_attn_kernel, tq=tq), grid=(b, MLA_HEADS // 2, nq), in_specs=in_specs,
        out_specs=pl.BlockSpec((tq, LANES), lambda bi, hp, qi: (bi * nq + qi, hp)),
        out_shape=jax.ShapeDtypeStruct((n, MLA_HEADS * MLA_V), BF16),
        scratch_shapes=[pltpu.VMEM((tp // CHUNK, 2, ATTN_V_ROWS, CHUNK), BF16),
                        pltpu.VMEM((2, ATTN_KV_BLOCK, tq), F32), pltpu.VMEM((2, SUBLANES, tq), F32),
                        pltpu.VMEM((2, ATTN_KV_BLOCK, tq), BF16),
                        pltpu.VMEM((2, SUBLANES, tq), F32), pltpu.VMEM((2, SUBLANES, tq), F32),
                        pltpu.VMEM((2, ATTN_V_ROWS, tq), F32)],
        compiler_params=_cparams("parallel", "parallel", "arbitrary"), name="attention",
    )(q, k, v)


def _post_kernel(*refs, n_in):
    h_ref = refs[0]
    a_refs = refs[1:1 + n_in]
    w_refs = refs[1 + n_in:1 + 2 * n_in]
    g_ref = refs[1 + 2 * n_in]
    o_ref = refs[2 + 2 * n_in]
    m = _dot(a_refs[0][...], w_refs[0][...])
    for a_ref, w_ref in zip(a_refs[1:], w_refs[1:]):
        m = m + _dot(a_ref[...], w_ref[...])
    o_ref[...] = h_ref[...] + _rms(m, g_ref[...])


def _post(h2, acts, wts, g):
    n, d = h2.shape
    tm = _pick_tile(n, (512, 384, 256, 128))
    rows = lambda i: (i, 0)
    in_specs = [pl.BlockSpec((tm, d), rows)]
    in_specs += [pl.BlockSpec((tm, a.shape[1]), rows) for a in acts]
    in_specs += [_const_spec(w.shape) for w in wts] + [_const_spec(g.shape)]
    return pl.pallas_call(
        functools.partial(_post_kernel, n_in=len(acts)), grid=(n // tm,), in_specs=in_specs,
        out_specs=pl.BlockSpec((tm, d), rows), out_shape=jax.ShapeDtypeStruct((n, d), F32),
        compiler_params=_cparams("parallel"), name="mixer_out",
    )(h2, *acts, *wts, g)


def _mlp_kernel(h_ref, g1_ref, wup_ref, wdn_ref, g2_ref, o_ref, *, ff_chunk):
    x = h_ref[...]
    hn = _rms(x, g1_ref[...]).astype(BF16)
    d_ff = wup_ref.shape[1]
    y = None
    for c0 in range(0, d_ff, ff_chunk):
        u = jnp.maximum(_dot(hn, wup_ref[:, c0:c0 + ff_chunk]), 0.0)
        part = _dot((u * u).astype(BF16), wdn_ref[c0:c0 + ff_chunk, :])
        y = part if y is None else y + part
    o_ref[...] = x + _rms(y, g2_ref[...])


def _mlp(h2, g1, wup, wdn, g2):
    n, d = h2.shape
    tm = _pick_tile(n, (512, 256, 128))
    rows = lambda i: (i, 0)
    in_specs = [pl.BlockSpec((tm, d), rows), _const_spec(g1.shape), _const_spec(wup.shape),
                _const_spec(wdn.shape), _const_spec(g2.shape)]
    return pl.pallas_call(
        functools.partial(_mlp_kernel, ff_chunk=1024), grid=(n // tm,), in_specs=in_specs,
        out_specs=pl.BlockSpec((tm, d), rows), out_shape=jax.ShapeDtypeStruct((n, d), F32),
        compiler_params=_cparams("parallel"), name="mlp",
    )(h2, g1, wup, wdn, g2)


def _proj_odd_kernel(h_ref, g_ref, wx_ref, wy_ref, cw_ref, cb_ref, x_ref, gate_ref, tail_scr):
    @pl.when(pl.program_id(0) == 0)
    def _():
        tail_scr[...] = jnp.zeros_like(tail_scr)

    hn = _rms(h_ref[...], g_ref[...]).astype(BF16)
    for c0 in range(0, LRU_WIDTH, LRU_CONV_COLS):
        cols = slice(c0, c0 + LRU_CONV_COLS)
        x_ref[:, cols] = _causal_conv(_dot(hn, wx_ref[:, cols]), tail_scr, cw_ref, cb_ref, cols).astype(BF16)
    y = _dot(hn, wy_ref[...])
    gelu = 0.5 * y * (1.0 + jnp.tanh(np.sqrt(2.0 / np.pi).astype(np.float32) * (y + 0.044715 * (y * y * y))))
    gate_ref[...] = gelu.astype(BF16)


def _proj_odd(h2, g, wx, wy, cw, cb):
    n, d = h2.shape
    tm = _pick_tile(n, (512, 384, 256, 128))
    rows = lambda i: (i, 0)
    w = wx.shape[1]
    return pl.pallas_call(
        _proj_odd_kernel, grid=(n // tm,),
        in_specs=[pl.BlockSpec((tm, d), rows)] + [_const_spec(a.shape) for a in (g, wx, wy, cw, cb)],
        out_specs=(pl.BlockSpec((tm, w), rows), pl.BlockSpec((tm, w), rows)),
        out_shape=(jax.ShapeDtypeStruct((n, w), BF16), jax.ShapeDtypeStruct((n, w), BF16)),
        scratch_shapes=[pltpu.VMEM((SUBLANES, w), F32)],
        compiler_params=_cparams("arbitrary"), name="proj_odd",
    )(h2, g, wx, wy, cw, cb)


def _rglru_kernel(x_ref, gate_ref, wa_ref, ba_ref, wi_ref, bi_ref, lam_ref, o_ref, h_scr):
    t = pl.program_id(1)
    L = x_ref.shape[0]

    @pl.when(t == 0)
    def _():
        h_scr[...] = jnp.zeros_like(h_scr)

    xr = x_ref[...].astype(F32)
    rs, gs = [], []
    for nb in range(LRU_BLOCKS):
        sl = slice(nb * LRU_BLOCK, (nb + 1) * LRU_BLOCK)
        xb = x_ref[:, sl]
        rs.append(_dot(xb, wa_ref[nb]))
        gs.append(_dot(xb, wi_ref[nb]))
    r = _sigmoid(jnp.concatenate(rs, axis=1) + ba_ref[...])
    ig = _sigmoid(jnp.concatenate(gs, axis=1) + bi_ref[...])
    log_a = (-LRU_C) * r * _softplus(-lam_ref[...])
    a = jnp.exp(log_a)
    row = lax.broadcasted_iota(jnp.int32, (L, 1), 0)
    valid = (t * L + row) >= PAD
    om = 1.0 - a * a
    root = jnp.where(om > 0.0, om * lax.rsqrt(om), 0.0)
    u = jnp.where(valid, root * (ig * xr), 0.0)

    groups = L // SUBLANES
    a3 = a.reshape(groups, SUBLANES, LRU_WIDTH)
    u3 = u.reshape(groups, SUBLANES, LRU_WIDTH)
    sub = lax.broadcasted_iota(jnp.int32, a3.shape, 1)
    d = 1
    while d < SUBLANES:
        keep = sub >= d
        a_s = jnp.where(keep, pltpu.roll(a3, d, 1), 1.0)
        u_s = jnp.where(keep, pltpu.roll(u3, d, 1), 0.0)
        u3 = a3 * u_s + u3
        a3 = a3 * a_s
        d *= 2
    h_prev = h_scr[0:1, :]
    hs = []
    for gi in range(groups):
        hs.append(u3[gi] + a3[gi] * h_prev)
        h_prev = hs[-1][SUBLANES - 1:SUBLANES, :]
    h_scr[...] = jnp.broadcast_to(h_prev, h_scr.shape)
    o_ref[...] = (jnp.concatenate(hs, axis=0) * gate_ref[...].astype(F32)).astype(BF16)


def _rglru(x3, gate3, wa, ba, wi, bi, lam):
    b, tp, w = x3.shape
    tt = CHUNK
    row3 = lambda bi_, ti: (bi_, ti, 0)
    in_specs = [pl.BlockSpec((None, tt, w), row3), pl.BlockSpec((None, tt, w), row3)]
    in_specs += [_const_spec(a.shape) for a in (wa, ba, wi, bi, lam)]
    return pl.pallas_call(
        _rglru_kernel, grid=(b, tp // tt), in_specs=in_specs,
        out_specs=pl.BlockSpec((None, tt, w), row3),
        out_shape=jax.ShapeDtypeStruct((b, tp, w), BF16),
        scratch_shapes=[pltpu.VMEM((SUBLANES, w), F32)],
        compiler_params=_cparams("parallel", "arbitrary"), name="rglru",
    )(x3, gate3, wa, ba, wi, bi, lam)


def _rope_tables(tp):
    inv = ROPE_BASE ** (-jnp.arange(0, MLA_ROPE, 2, dtype=F32) / MLA_ROPE)
    pos = jnp.maximum(jnp.arange(tp, dtype=F32) - PAD, 0.0)
    ang = pos[:, None] * inv[None, :]
    cos, sin = jnp.cos(ang), jnp.sin(ang)
    zeros_lo = jnp.zeros((tp, MLA_NOPE), F32)
    zeros_hi = jnp.zeros((tp, LANES - MLA_NOPE - MLA_ROPE), F32)
    ck = jnp.concatenate([zeros_lo, cos, cos, zeros_hi], axis=1)
    sk = jnp.concatenate([zeros_lo, sin, sin, zeros_hi], axis=1)
    scale = (MLA_NOPE + MLA_ROPE) ** -0.5 * np.log2(np.e)
    cq = jnp.concatenate([jnp.ones((tp, MLA_NOPE), F32), cos, cos, zeros_hi], axis=1) * scale
    sq = sk * scale
    return cq, sq, ck, sk


def _even_weights(w_in, w_q_up, w_kv_up):
    offs = np.cumsum((SSD_D_INNER, SSD_CONV_CH, SSD_HEADS, MLA_Q_RANK, MLA_KV_RANK, MLA_ROPE))
    wz = w_in[:, :offs[0]]
    wxbc = w_in[:, offs[0]:offs[1]]
    wdt = w_in[:, offs[1]:offs[2]]
    wcq = w_in[:, offs[2]:offs[3]]
    wckv = w_in[:, offs[3]:offs[4]]
    wkr = w_in[:, offs[4]:offs[5]]
    half = MLA_ROPE // 2
    hi_pad = LANES - MLA_NOPE - MLA_ROPE
    wdt_p = jnp.pad(wdt, ((0, 0), (0, LANES - SSD_HEADS)))
    wkr_p = jnp.pad(wkr, ((0, 0), (MLA_NOPE, hi_pad)))
    wkr_s = jnp.pad(jnp.concatenate([-wkr[:, half:], wkr[:, :half]], axis=1), ((0, 0), (MLA_NOPE, hi_pad)))
    wq = w_q_up.reshape(MLA_Q_RANK, MLA_HEADS, MLA_NOPE + MLA_ROPE)
    q_nope, q_rope = wq[..., :MLA_NOPE], wq[..., MLA_NOPE:]
    wq_p = jnp.pad(wq, ((0, 0), (0, 0), (0, hi_pad))).reshape(MLA_Q_RANK, MLA_HEADS * LANES)
    q_swap = jnp.concatenate([jnp.zeros_like(q_nope), -q_rope[..., half:], q_rope[..., :half]], axis=-1)
    wq_s = jnp.pad(q_swap, ((0, 0), (0, 0), (0, hi_pad))).reshape(MLA_Q_RANK, MLA_HEADS * LANES)
    wkv = w_kv_up.reshape(MLA_KV_RANK, MLA_HEADS, MLA_NOPE + MLA_V)
    wk_p = jnp.pad(wkv[..., :MLA_NOPE], ((0, 0), (0, 0), (0, LANES - MLA_NOPE))).reshape(
        MLA_KV_RANK, MLA_HEADS * LANES)
    wv = wkv[..., MLA_NOPE:].reshape(MLA_KV_RANK, MLA_HEADS * MLA_V)
    bf = lambda a: a.astype(BF16)
    return dict(wz=bf(wz), wxbc=bf(wxbc), wdt=bf(wdt_p), wkr_p=bf(wkr_p), wkr_s=bf(wkr_s), wcq=bf(wcq),
                wckv=bf(wckv), wq_p=bf(wq_p), wq_s=bf(wq_s), wk_p=bf(wk_p), wv=bf(wv))


def _row(a):
    return a.reshape(1, -1).astype(F32)


def kernel(x, meta_tokens, mix_pre_g, mix_post_g, mlp_pre_g, mlp_post_g, w_up, w_down, w_in, ssd_conv_w,
           ssd_conv_b, ssd_dt_bias, ssd_a_log, ssd_d, ssd_norm_g, mla_q_norm_g, mla_w_q_up, mla_kv_norm_g,
           mla_w_kv_up, w_out_ab, rg_w_x, rg_w_y, rg_conv_w, rg_conv_b, rg_w_a, rg_b_a, rg_w_i, rg_b_i,
           rg_lambda, rg_w_out):
    b, seq, d = x.shape
    depth = mix_pre_g.shape[0]
    tp = PAD + N_META + seq
    n = b * tp
    meta = jnp.broadcast_to(meta_tokens[None].astype(x.dtype), (b, N_META, d))
    h = jnp.concatenate([jnp.zeros((b, PAD, d), x.dtype), meta, x], axis=1).reshape(n, d)
    tables = _rope_tables(tp)

    for layer in range(depth):
        if layer % 2 == 0:
            e = layer // 2
            w = _even_weights(w_in[e], mla_w_q_up[e], mla_w_kv_up[e])
            wts = (w["wz"], w["wxbc"], w["wdt"], w["wkr_p"], w["wkr_s"], w["wcq"], _row(mla_q_norm_g[e]),
                   w["wckv"], _row(mla_kv_norm_g[e]), w["wq_p"], w["wq_s"], w["wk_p"], w["wv"],
                   ssd_conv_w[e].astype(F32), _row(ssd_conv_b[e]))
            z, xbc, dt, q, k, v = _proj_even(h.reshape(b, tp, d), _row(mix_pre_g[layer]), wts, tables)
            lane_pad = lambda a: jnp.pad(_row(a), ((0, 0), (0, LANES - SSD_HEADS)))
            y_ssd = _ssd(xbc.reshape(b, tp, SSD_CONV_CH), dt.reshape(b, tp, LANES), z.reshape(b, tp, SSD_D_INNER),
                         lane_pad(ssd_dt_bias[e]), lane_pad(ssd_a_log[e]),
                         _row(jnp.repeat(ssd_d[e], SSD_HEAD_DIM)), _row(ssd_norm_g[e]))
            y_att = _attention(q, k, v, b, tp)
            wo = w_out_ab[e].astype(BF16)
            h = _post(h, (y_ssd.reshape(n, SSD_D_INNER), y_att), (wo[:SSD_D_INNER], wo[SSD_D_INNER:]),
                      _row(mix_post_g[layer]))
        else:
            o = layer // 2
            xr, gate = _proj_odd(h, _row(mix_pre_g[layer]), rg_w_x[o].astype(BF16), rg_w_y[o].astype(BF16),
                                 rg_conv_w[o].astype(F32), _row(rg_conv_b[o]))
            y = _rglru(xr.reshape(b, tp, LRU_WIDTH), gate.reshape(b, tp, LRU_WIDTH), rg_w_a[o].astype(BF16),
                       _row(rg_b_a[o]), rg_w_i[o].astype(BF16), _row(rg_b_i[o]), _row(rg_lambda[o]))
            h = _post(h, (y.reshape(n, LRU_WIDTH),), (rg_w_out[o].astype(BF16),), _row(mix_post_g[layer]))
        h = _mlp(h, _row(mlp_pre_g[layer]), w_up[layer].astype(BF16), w_down[layer].astype(BF16),
                 _row(mlp_post_g[layer]))
    return h.reshape(b, tp, d)[:, PAD + N_META:]
```

```python
import functools

import jax
import jax.numpy as jnp
import numpy as np
from jax import lax
from jax.experimental import pallas as pl
from jax.experimental.pallas import tpu as pltpu

F32 = jnp.float32
BF16 = jnp.bfloat16

N_META = 16
CHUNK = 128
PAD = CHUNK - N_META
EPS = 1e-6
SSD_HEADS = 16
SSD_HEAD_DIM = 64
SSD_D_INNER = SSD_HEADS * SSD_HEAD_DIM
SSD_GROUPS = 2
SSD_STATE = 128
SSD_CONV = 4
SSD_BC = SSD_GROUPS * SSD_STATE
SSD_CONV_CH = SSD_D_INNER + 2 * SSD_BC
MLA_HEADS = 16
MLA_NOPE = 64
MLA_ROPE = 32
MLA_V = 64
MLA_Q_RANK = 384
MLA_KV_RANK = 256
ROPE_BASE = 10000.0
LRU_BLOCKS = 10
LRU_BLOCK = 128
LRU_WIDTH = LRU_BLOCKS * LRU_BLOCK
LRU_C = 8.0
LRU_CONV_COLS = 640
LANES = 128
SUBLANES = 8
VMEM_LIMIT_BYTES = 56 * 1024 * 1024
NEG_BIG = -1e30


def _cparams(*sem):
    return pltpu.CompilerParams(dimension_semantics=sem, vmem_limit_bytes=VMEM_LIMIT_BYTES)


def _pick_tile(n, candidates):
    for c in candidates:
        if n % c == 0:
            return c
    raise ValueError(f"no tile in {candidates} divides {n}")


def _const_spec(shape):
    nd = len(shape)
    return pl.BlockSpec(shape, lambda *_: (0,) * nd)


def _rms(x, g):
    ms = jnp.mean(x * x, axis=-1, keepdims=True)
    return x * lax.rsqrt(ms + EPS) * g


def _sigmoid(x):
    return 1.0 / (1.0 + jnp.exp(-x))


def _softplus(x):
    return jnp.maximum(x, 0.0) + jnp.log(1.0 + jnp.exp(-jnp.abs(x)))


def _dot(a, b):
    return jnp.dot(a, b, preferred_element_type=F32)


def _dot_nt(a, b):
    return lax.dot_general(a, b, (((1,), (1,)), ((), ())), preferred_element_type=F32)


def _dot_tn(a, b):
    return lax.dot_general(a, b, (((0,), (0,)), ((), ())), preferred_element_type=F32)


def _shift_rows(cur, tail, k):
    if k == 0:
        return cur
    rolled = pltpu.roll(cur, k, 0)
    row = lax.broadcasted_iota(jnp.int32, tail.shape, 0)
    top = jnp.where(row < k, pltpu.roll(tail, k, 0), rolled[0:SUBLANES])
    return jnp.concatenate([top, rolled[SUBLANES:]], axis=0)


def _causal_conv(cur, tail_scr, w_ref, b_ref, cols):
    k = w_ref.shape[0]
    tail = tail_scr[:, cols]
    out = b_ref[:, cols]
    for i in range(k):
        out = out + _shift_rows(cur, tail, k - 1 - i) * w_ref[i:i + 1, cols]
    tail_scr[:, cols] = cur[cur.shape[0] - SUBLANES:]
    return out


CONV_COLS = 512


def _proj_even_kernel(h_ref, g_ref, wz_ref, wxbc_ref, wdt_ref, wkrp_ref, wkrs_ref, wcq_ref, qg_ref,
                      wckv_ref, kvg_ref, wqp_ref, wqs_ref, wkp_ref, wv_ref, cw_ref, cb_ref,
                      cq_ref, sq_ref, ck_ref, sk_ref,
                      z_ref, xbc_ref, dt_ref, q_ref, k_ref, v_ref, tail_scr):
    @pl.when(pl.program_id(1) == 0)
    def _():
        tail_scr[...] = jnp.zeros_like(tail_scr)

    hn = _rms(h_ref[...], g_ref[...]).astype(BF16)
    z_ref[...] = _dot(hn, wz_ref[...]).astype(BF16)
    for c0 in range(0, SSD_CONV_CH, CONV_COLS):
        cols = slice(c0, c0 + CONV_COLS)
        conv = _causal_conv(_dot(hn, wxbc_ref[:, cols]), tail_scr, cw_ref, cb_ref, cols)
        xbc_ref[:, cols] = (conv * _sigmoid(conv)).astype(BF16)
    dt_ref[...] = _dot(hn, wdt_ref[...])
    kr = _dot(hn, wkrp_ref[...]) * ck_ref[...] + _dot(hn, wkrs_ref[...]) * sk_ref[...]
    cqn = _rms(_dot(hn, wcq_ref[...]), qg_ref[...]).astype(BF16)
    ckvn = _rms(_dot(hn, wckv_ref[...]), kvg_ref[...]).astype(BF16)
    v_ref[...] = _dot(ckvn, wv_ref[...]).astype(BF16)
    cq_t = jnp.concatenate([cq_ref[...]] * 2, axis=1)
    sq_t = jnp.concatenate([sq_ref[...]] * 2, axis=1)
    kr2 = jnp.concatenate([kr] * 2, axis=1)
    for hd in range(0, MLA_HEADS, 2):
        sl = slice(hd * LANES, (hd + 2) * LANES)
        qh = (_dot(cqn, wqp_ref[:, sl]) * cq_t + _dot(cqn, wqs_ref[:, sl]) * sq_t).astype(BF16)
        kh = (_dot(ckvn, wkp_ref[:, sl]) + kr2).astype(BF16)
        for j in range(2):
            q_ref[hd + j] = qh[:, j * LANES:(j + 1) * LANES]
            k_ref[hd + j] = kh[:, j * LANES:(j + 1) * LANES]


def _proj_even(h3, g, wts, tables):
    b, tp, d = h3.shape
    tm = _pick_tile(tp, (384, 128))
    nt = tp // tm
    n = b * tp
    row3 = lambda bi, i: (bi, i, 0)
    flat = lambda bi, i: (bi * nt + i, 0)
    tab = lambda bi, i: (i, 0)
    in_specs = [pl.BlockSpec((None, tm, d), row3), _const_spec(g.shape)]
    in_specs += [_const_spec(w.shape) for w in wts]
    in_specs += [pl.BlockSpec((tm, LANES), tab) for _ in tables]
    out_shape = (
        jax.ShapeDtypeStruct((n, SSD_D_INNER), BF16),
        jax.ShapeDtypeStruct((n, SSD_CONV_CH), BF16),
        jax.ShapeDtypeStruct((n, LANES), F32),
        jax.ShapeDtypeStruct((MLA_HEADS, n, LANES), BF16),
        jax.ShapeDtypeStruct((MLA_HEADS, n, LANES), BF16),
        jax.ShapeDtypeStruct((n, MLA_HEADS * MLA_V), BF16),
    )
    head3 = lambda bi, i: (0, bi * nt + i, 0)
    out_specs = (
        pl.BlockSpec((tm, SSD_D_INNER), flat),
        pl.BlockSpec((tm, SSD_CONV_CH), flat),
        pl.BlockSpec((tm, LANES), flat),
        pl.BlockSpec((MLA_HEADS, tm, LANES), head3),
        pl.BlockSpec((MLA_HEADS, tm, LANES), head3),
        pl.BlockSpec((tm, MLA_HEADS * MLA_V), flat),
    )
    return pl.pallas_call(
        _proj_even_kernel, grid=(b, nt), in_specs=in_specs, out_specs=out_specs, out_shape=out_shape,
        scratch_shapes=[pltpu.VMEM((SUBLANES, SSD_CONV_CH), F32)],
        compiler_params=_cparams("parallel", "arbitrary"), name="proj_even",
    )(h3, g, *wts, *tables)


def _cumsum_rows(x):
    n = x.shape[0]
    row = lax.broadcasted_iota(jnp.int32, x.shape, 0)
    d = 1
    while d < n:
        x = x + jnp.where(row >= d, pltpu.roll(x, d, 0), 0.0)
        d *= 2
    return x


SSD_BATCH_BLOCK = 4


def _ssd_chunk(c, xbc, dt_raw, zf, dtb, alog, dsk, ng, state_ref, y_scr):
    L = CHUNK
    heads_per_group = SSD_HEADS // SSD_GROUPS
    row = lax.broadcasted_iota(jnp.int32, (L, 1), 0)
    valid = (c * L + row) >= PAD
    act = jnp.where(valid, xbc.astype(F32), 0.0)
    xs = act[:, :SSD_D_INNER]
    bm = act[:, SSD_D_INNER:SSD_D_INNER + SSD_BC]
    cm = act[:, SSD_D_INNER + SSD_BC:]

    dtv = jnp.where(valid, _softplus(dt_raw + dtb), 0.0)
    a_cs = _cumsum_rows(dtv * (-jnp.exp(alog)))
    a_cs_t = a_cs.T
    dt_t = dtv.T
    tot = a_cs[L - 1:L, :]
    w_t = jnp.exp(a_cs_t[:, L - 1:L] - a_cs_t) * dt_t
    cd = jnp.exp(tot)
    causal = lax.broadcasted_iota(jnp.int32, (L, L), 0) >= lax.broadcasted_iota(jnp.int32, (L, L), 1)
    lo = lax.broadcasted_iota(jnp.int32, (L, LANES), 1) < SSD_HEAD_DIM

    for g in range(SSD_GROUPS):
        cc = cm[:, g * SSD_STATE:(g + 1) * SSD_STATE]
        bc = bm[:, g * SSD_STATE:(g + 1) * SSD_STATE]
        bc_t = bc.T
        cb = _dot_nt(cc.astype(BF16), bc.astype(BF16))
        for hp in range(heads_per_group // 2):
            pair = g * (heads_per_group // 2) + hp
            sl = slice(pair * LANES, (pair + 1) * LANES)
            xs_pair = xs[:, sl]
            st_pair = state_ref[:, sl]
            xs_b = xs_pair.astype(BF16)
            rhs = jnp.concatenate([xs_b, st_pair.astype(BF16)], axis=0)
            ys, sts, cds = [], [], []
            for hh in range(2):
                h = 2 * pair + hh
                a_col = jnp.broadcast_to(a_cs[:, h:h + 1], (L, L))
                dec = jnp.exp(jnp.where(causal, a_col - a_cs_t[h:h + 1, :], -jnp.inf))
                m = cb * dec * dt_t[h:h + 1, :]
                lhs = jnp.concatenate([m, cc * jnp.exp(a_col)], axis=1).astype(BF16)
                ys.append(_dot(lhs, rhs))
                sts.append(_dot((bc_t * w_t[h:h + 1, :]).astype(BF16), xs_b))
                cds.append(cd[:, h:h + 1])
            y_scr[:, sl] = jnp.where(lo, ys[0], ys[1]) + dsk[:, sl] * xs_pair
            state_ref[:, sl] = jnp.where(lo, cds[0], cds[1]) * st_pair + jnp.where(lo, sts[0], sts[1])

    return _rms(y_scr[...] * (zf * _sigmoid(zf)), ng).astype(BF16)


def _ssd_kernel(xbc_ref, dt_ref, z_ref, dtb_ref, alog_ref, dsk_ref, ng_ref, y_ref, state_scr, y_scr):
    c = pl.program_id(1)

    @pl.when(c == 0)
    def _():
        state_scr[...] = jnp.zeros_like(state_scr)

    for bb in range(xbc_ref.shape[0]):
        y_ref[bb] = _ssd_chunk(c, xbc_ref[bb], dt_ref[bb], z_ref[bb].astype(F32), dtb_ref[...], alog_ref[...],
                               dsk_ref[...], ng_ref[...], state_scr.at[bb], y_scr.at[bb])


def _ssd(xbc3, dt3, z3, dtb, alog, dsk, ng):
    b, tp, _ = xbc3.shape
    nc = tp // CHUNK
    gb = _pick_tile(b, (SSD_BATCH_BLOCK, 1))
    row3 = lambda bi, ci: (bi, ci, 0)
    in_specs = [
        pl.BlockSpec((gb, CHUNK, SSD_CONV_CH), row3),
        pl.BlockSpec((gb, CHUNK, LANES), row3),
        pl.BlockSpec((gb, CHUNK, SSD_D_INNER), row3),
    ] + [_const_spec(a.shape) for a in (dtb, alog, dsk, ng)]
    return pl.pallas_call(
        _ssd_kernel, grid=(b // gb, nc), in_specs=in_specs,
        out_specs=pl.BlockSpec((gb, CHUNK, SSD_D_INNER), row3),
        out_shape=jax.ShapeDtypeStruct((b, tp, SSD_D_INNER), BF16),
        scratch_shapes=[pltpu.VMEM((gb, SSD_STATE, SSD_D_INNER), F32),
                        pltpu.VMEM((gb, CHUNK, SSD_D_INNER), F32)],
        compiler_params=_cparams("parallel", "arbitrary"), name="ssd",
    )(xbc3, dt3, z3, dtb, alog, dsk, ng)


ATTN_KV_BLOCK = 256
ATTN_V_ROWS = MLA_V + 16


def _attn_kernel(q_ref, k_ref, v_ref, o_ref, vt_scr, s_scr, bmax_scr, p_scr, alpha_scr, m_scr, acc_scr, *, tq):
    kb = ATTN_KV_BLOCK
    qi = pl.program_id(2)
    q0 = qi * tq
    end = q0 + tq
    n_int = q0 // kb
    n_blocks = (end + kb - 1) // kb
    last_off = end - kb

    @pl.when(qi == 0)
    def _():
        ones = jnp.ones((ATTN_V_ROWS - MLA_V, CHUNK), BF16)
        for c in range(vt_scr.shape[0]):
            vt = v_ref[c * CHUNK:(c + 1) * CHUNK, :].astype(F32).T.astype(BF16)
            for hh in range(2):
                vt_scr[c, hh, :MLA_V, :] = vt[hh * MLA_V:(hh + 1) * MLA_V, :]
                vt_scr[c, hh, MLA_V:, :] = ones

    m_scr[...] = jnp.full_like(m_scr, NEG_BIG)
    acc_scr[...] = jnp.zeros_like(acc_scr)

    def offset(bk):
        return pl.multiple_of(jnp.minimum(bk * kb, last_off), CHUNK)

    levels = tuple(range(0, tq - kb + 1, kb))

    def level(bk):
        return jnp.minimum(jnp.maximum(offset(bk) - q0, 0) // kb, len(levels) - 1) * kb

    def qk(hh, bk, masked, cs):
        off = offset(bk)
        s = _dot_nt(k_ref[hh, pl.ds(off, kb), :], q_ref[hh, cs:, :])
        if masked:
            key = lax.broadcasted_iota(jnp.int32, s.shape, 0)
            keep = (key - lax.broadcasted_iota(jnp.int32, s.shape, 1)) <= (q0 + cs - off)
            first_key = jnp.where(bk == 0, PAD, bk * kb - off)
            s = jnp.where(keep, s, NEG_BIG)
            s = jnp.where(key >= first_key, s, NEG_BIG)
        s_scr[hh, :, cs:] = s
        bmax_scr[hh, :, cs:] = jnp.broadcast_to(jnp.max(s, axis=0, keepdims=True), (SUBLANES, tq - cs))

    def softmax(hh, cs):
        m_prev = m_scr[hh, 0:1, cs:]
        m_new = jnp.maximum(m_prev, bmax_scr[hh, 0:1, cs:])
        alpha_scr[hh, :, cs:] = jnp.broadcast_to(jnp.exp2(m_prev - m_new), (SUBLANES, tq - cs))
        m_scr[hh, :, cs:] = jnp.broadcast_to(m_new, (SUBLANES, tq - cs))
        p_scr[hh, :, cs:] = jnp.exp2(s_scr[hh, :, cs:] - m_new).astype(BF16)

    def pv(hh, bk, cs):
        c0 = offset(bk) // CHUNK
        vt = jnp.concatenate([vt_scr[c0 + c, hh] for c in range(kb // CHUNK)], axis=1)
        acc_scr[hh, :, cs:] = alpha_scr[hh, 0:1, cs:] * acc_scr[hh, :, cs:] + _dot(vt, p_scr[hh, :, cs:])

    def step(bk, masked, cs_prev, cs):
        softmax(1, cs_prev)
        qk(0, bk, masked, cs)
        pv(0, bk - 1, cs_prev)
        softmax(0, cs)
        qk(1, bk, masked, cs)
        pv(1, bk - 1, cs_prev)

    def plain_step(bk, carry):
        step(bk, False, 0, 0)
        return carry

    def masked_step(bk, carry):
        lvl_prev, lvl = level(bk - 1), level(bk)
        for a in levels:
            for b in (a, a + kb):
                if b in levels:
                    @pl.when(jnp.logical_and(lvl_prev == a, lvl == b))
                    def _():
                        step(bk, True, a, b)
        return carry

    qk(0, 0, True, 0)
    softmax(0, 0)
    qk(1, 0, True, 0)
    lax.fori_loop(1, n_int, plain_step, 0)
    lax.fori_loop(jnp.maximum(n_int, 1), n_blocks, masked_step, 0)
    lvl_last = level(n_blocks - 1)
    for a in levels:
        @pl.when(lvl_last == a)
        def _():
            softmax(1, a)
            pv(0, n_blocks - 1, a)
            pv(1, n_blocks - 1, a)

    out_t = jnp.concatenate([acc_scr[hh, :MLA_V, :] / acc_scr[hh, MLA_V:MLA_V + 1, :] for hh in range(2)],
                            axis=0)
    qpos = q0 + lax.broadcasted_iota(jnp.int32, (1, tq), 1)
    out_t = jnp.where(qpos >= PAD, out_t, 0.0)
    for c in range(tq // CHUNK):
        o_ref[c * CHUNK:(c + 1) * CHUNK, :] = out_t[:, c * CHUNK:(c + 1) * CHUNK].T.astype(BF16)


def _attention(q, k, v, b, tp):
    tq = _pick_tile(tp, (1408, 384, 256))
    nq = tp // tq
    n = b * tp
    in_specs = [
        pl.BlockSpec((2, tq, LANES), lambda bi, hp, qi: (hp, bi * nq + qi, 0)),
        pl.BlockSpec((2, tp, LANES), lambda bi, hp, qi: (hp, bi, 0)),
        pl.BlockSpec((tp, LANES), lambda bi, hp, qi: (bi, hp)),
    ]
    return pl.pallas_call(
        functools.partial(_attn_kernel, tq=tq), grid=(b, MLA_HEADS // 2, nq), in_specs=in_specs,
        out_specs=pl.BlockSpec((tq, LANES), lambda bi, hp, qi: (bi * nq + qi, hp)),
        out_shape=jax.ShapeDtypeStruct((n, MLA_HEADS * MLA_V), BF16),
        scratch_shapes=[pltpu.VMEM((tp // CHUNK, 2, ATTN_V_ROWS, CHUNK), BF16),
                        pltpu.VMEM((2, ATTN_KV_BLOCK, tq), F32), pltpu.VMEM((2, SUBLANES, tq), F32),
                        pltpu.VMEM((2, ATTN_KV_BLOCK, tq), BF16),
                        pltpu.VMEM((2, SUBLANES, tq), F32), pltpu.VMEM((2, SUBLANES, tq), F32),
                        pltpu.VMEM((2, ATTN_V_ROWS, tq), F32)],
        compiler_params=_cparams("parallel", "parallel", "arbitrary"), name="attention",
    )(q, k, v)


def _post_kernel(*refs, n_in):
    h_ref = refs[0]
    a_refs = refs[1:1 + n_in]
    w_refs = refs[1 + n_in:1 + 2 * n_in]
    g_ref = refs[1 + 2 * n_in]
    o_ref = refs[2 + 2 * n_in]
    m = _dot(a_refs[0][...], w_refs[0][...])
    for a_ref, w_ref in zip(a_refs[1:], w_refs[1:]):
        m = m + _dot(a_ref[...], w_ref[...])
    o_ref[...] = h_ref[...] + _rms(m, g_ref[...])


def _post(h2, acts, wts, g):
    n, d = h2.shape
    tm = _pick_tile(n, (1024, 512, 384, 256, 128))
    rows = lambda i: (i, 0)
    in_specs = [pl.BlockSpec((tm, d), rows)]
    in_specs += [pl.BlockSpec((tm, a.shape[1]), rows) for a in acts]
    in_specs += [_const_spec(w.shape) for w in wts] + [_const_spec(g.shape)]
    return pl.pallas_call(
        functools.partial(_post_kernel, n_in=len(acts)), grid=(n // tm,), in_specs=in_specs,
        out_specs=pl.BlockSpec((tm, d), rows), out_shape=jax.ShapeDtypeStruct((n, d), F32),
        compiler_params=_cparams("parallel"), name="mixer_out",
    )(h2, *acts, *wts, g)


def _mlp_kernel(h_ref, g1_ref, wup_ref, wdn_ref, g2_ref, o_ref, *, ff_chunk):
    x = h_ref[...]
    hn = _rms(x, g1_ref[...]).astype(BF16)
    d_ff = wup_ref.shape[1]
    y = None
    for c0 in range(0, d_ff, ff_chunk):
        u = jnp.maximum(_dot(hn, wup_ref[:, c0:c0 + ff_chunk]), 0.0)
        part = _dot((u * u).astype(BF16), wdn_ref[c0:c0 + ff_chunk, :])
        y = part if y is None else y + part
    o_ref[...] = x + _rms(y, g2_ref[...])


def _mlp(h2, g1, wup, wdn, g2):
    n, d = h2.shape
    tm = _pick_tile(n, (512, 256, 128))
    rows = lambda i: (i, 0)
    in_specs = [pl.BlockSpec((tm, d), rows), _const_spec(g1.shape), _const_spec(wup.shape),
                _const_spec(wdn.shape), _const_spec(g2.shape)]
    return pl.pallas_call(
        functools.partial(_mlp_kernel, ff_chunk=1024), grid=(n // tm,), in_specs=in_specs,
        out_specs=pl.BlockSpec((tm, d), rows), out_shape=jax.ShapeDtypeStruct((n, d), F32),
        compiler_params=_cparams("parallel"), name="mlp",
    )(h2, g1, wup, wdn, g2)


def _mlp_last(h3, g1, wup, wdn, g2, lead):
    b, tp, d = h3.shape
    seq = tp - lead
    tm = _pick_tile(seq, (512, 256, 128))
    in_specs = [pl.BlockSpec((pl.Element(tm), pl.Element(d)), lambda bi, i: (pl.multiple_of(bi * tp + lead + i * tm, CHUNK), 0)),
                _const_spec(g1.shape), _const_spec(wup.shape), _const_spec(wdn.shape), _const_spec(g2.shape)]
    return pl.pallas_call(
        functools.partial(_mlp_kernel, ff_chunk=1024), grid=(b, seq // tm), in_specs=in_specs,
        out_specs=pl.BlockSpec((None, tm, d), lambda bi, i: (bi, i, 0)),
        out_shape=jax.ShapeDtypeStruct((b, seq, d), F32),
        compiler_params=_cparams("parallel", "parallel"), name="mlp_last",
    )(h3.reshape(b * tp, d), g1, wup, wdn, g2)


def _proj_odd_kernel(h_ref, g_ref, wx_ref, wy_ref, cw_ref, cb_ref, x_ref, gate_ref, tail_scr):
    @pl.when(pl.program_id(0) == 0)
    def _():
        tail_scr[...] = jnp.zeros_like(tail_scr)

    hn = _rms(h_ref[...], g_ref[...]).astype(BF16)
    for c0 in range(0, LRU_WIDTH, LRU_CONV_COLS):
        cols = slice(c0, c0 + LRU_CONV_COLS)
        x_ref[:, cols] = _causal_conv(_dot(hn, wx_ref[:, cols]), tail_scr, cw_ref, cb_ref, cols).astype(BF16)
    y = _dot(hn, wy_ref[...])
    gelu = 0.5 * y * (1.0 + jnp.tanh(np.sqrt(2.0 / np.pi).astype(np.float32) * (y + 0.044715 * (y * y * y))))
    gate_ref[...] = gelu.astype(BF16)


def _proj_odd(h2, g, wx, wy, cw, cb):
    n, d = h2.shape
    tm = _pick_tile(n, (1024, 512, 384, 256, 128))
    rows = lambda i: (i, 0)
    w = wx.shape[1]
    return pl.pallas_call(
        _proj_odd_kernel, grid=(n // tm,),
        in_specs=[pl.BlockSpec((tm, d), rows)] + [_const_spec(a.shape) for a in (g, wx, wy, cw, cb)],
        out_specs=(pl.BlockSpec((tm, w), rows), pl.BlockSpec((tm, w), rows)),
        out_shape=(jax.ShapeDtypeStruct((n, w), BF16), jax.ShapeDtypeStruct((n, w), BF16)),
        scratch_shapes=[pltpu.VMEM((SUBLANES, w), F32)],
        compiler_params=_cparams("arbitrary"), name="proj_odd",
    )(h2, g, wx, wy, cw, cb)


def _rglru_kernel(x_ref, gate_ref, wa_ref, ba_ref, wi_ref, bi_ref, lam_ref, o_ref, h_scr):
    t = pl.program_id(1)
    L = x_ref.shape[0]

    @pl.when(t == 0)
    def _():
        h_scr[...] = jnp.zeros_like(h_scr)

    xr = x_ref[...].astype(F32)
    rs, gs = [], []
    for nb in range(LRU_BLOCKS):
        sl = slice(nb * LRU_BLOCK, (nb + 1) * LRU_BLOCK)
        xb = x_ref[:, sl]
        rs.append(_dot(xb, wa_ref[nb]))
        gs.append(_dot(xb, wi_ref[nb]))
    r = _sigmoid(jnp.concatenate(rs, axis=1) + ba_ref[...])
    ig = _sigmoid(jnp.concatenate(gs, axis=1) + bi_ref[...])
    log_a = (-LRU_C) * r * _softplus(-lam_ref[...])
    a = jnp.exp(log_a)
    row = lax.broadcasted_iota(jnp.int32, (L, 1), 0)
    valid = (t * L + row) >= PAD
    om = 1.0 - a * a
    root = jnp.where(om > 0.0, om * lax.rsqrt(om), 0.0)
    u = jnp.where(valid, root * (ig * xr), 0.0)

    groups = L // SUBLANES
    a3 = a.reshape(groups, SUBLANES, LRU_WIDTH)
    u3 = u.reshape(groups, SUBLANES, LRU_WIDTH)
    sub = lax.broadcasted_iota(jnp.int32, a3.shape, 1)
    d = 1
    while d < SUBLANES:
        keep = sub >= d
        a_s = jnp.where(keep, pltpu.roll(a3, d, 1), 1.0)
        u_s = jnp.where(keep, pltpu.roll(u3, d, 1), 0.0)
        u3 = a3 * u_s + u3
        a3 = a3 * a_s
        d *= 2
    h_prev = h_scr[0:1, :]
    hs = []
    for gi in range(groups):
        hs.append(u3[gi] + a3[gi] * h_prev)
        h_prev = hs[-1][SUBLANES - 1:SUBLANES, :]
    h_scr[...] = jnp.broadcast_to(h_prev, h_scr.shape)
    o_ref[...] = (jnp.concatenate(hs, axis=0) * gate_ref[...].astype(F32)).astype(BF16)


def _rglru(x3, gate3, wa, ba, wi, bi, lam):
    b, tp, w = x3.shape
    tt = CHUNK
    row3 = lambda bi_, ti: (bi_, ti, 0)
    in_specs = [pl.BlockSpec((None, tt, w), row3), pl.BlockSpec((None, tt, w), row3)]
    in_specs += [_const_spec(a.shape) for a in (wa, ba, wi, bi, lam)]
    return pl.pallas_call(
        _rglru_kernel, grid=(b, tp // tt), in_specs=in_specs,
        out_specs=pl.BlockSpec((None, tt, w), row3),
        out_shape=jax.ShapeDtypeStruct((b, tp, w), BF16),
        scratch_shapes=[pltpu.VMEM((SUBLANES, w), F32)],
        compiler_params=_cparams("parallel", "arbitrary"), name="rglru",
    )(x3, gate3, wa, ba, wi, bi, lam)


def _rope_tables(tp):
    f32 = np.float32
    inv = f32(ROPE_BASE) ** (-np.arange(0, MLA_ROPE, 2, dtype=f32) / f32(MLA_ROPE))
    pos = np.maximum(np.arange(tp, dtype=f32) - f32(PAD), f32(0))
    ang = (pos[:, None] * inv[None, :]).astype(f32)
    cos, sin = np.cos(ang).astype(f32), np.sin(ang).astype(f32)
    zeros_lo = np.zeros((tp, MLA_NOPE), f32)
    zeros_hi = np.zeros((tp, LANES - MLA_NOPE - MLA_ROPE), f32)
    ck = np.concatenate([zeros_lo, cos, cos, zeros_hi], axis=1)
    sk = np.concatenate([zeros_lo, sin, sin, zeros_hi], axis=1)
    scale = f32((MLA_NOPE + MLA_ROPE) ** -0.5 * np.log2(np.e))
    cq = np.concatenate([np.ones((tp, MLA_NOPE), f32), cos, cos, zeros_hi], axis=1) * scale
    sq = sk * scale
    return tuple(jnp.asarray(t) for t in (cq, sq, ck, sk))


def _even_weights(w_in, w_q_up, w_kv_up):
    offs = np.cumsum((SSD_D_INNER, SSD_CONV_CH, SSD_HEADS, MLA_Q_RANK, MLA_KV_RANK, MLA_ROPE))
    wz = w_in[:, :offs[0]]
    wxbc = w_in[:, offs[0]:offs[1]]
    wdt = w_in[:, offs[1]:offs[2]]
    wcq = w_in[:, offs[2]:offs[3]]
    wckv = w_in[:, offs[3]:offs[4]]
    wkr = w_in[:, offs[4]:offs[5]]
    half = MLA_ROPE // 2
    hi_pad = LANES - MLA_NOPE - MLA_ROPE
    wdt_p = jnp.pad(wdt, ((0, 0), (0, LANES - SSD_HEADS)))
    wkr_p = jnp.pad(wkr, ((0, 0), (MLA_NOPE, hi_pad)))
    wkr_s = jnp.pad(jnp.concatenate([-wkr[:, half:], wkr[:, :half]], axis=1), ((0, 0), (MLA_NOPE, hi_pad)))
    wq = w_q_up.reshape(MLA_Q_RANK, MLA_HEADS, MLA_NOPE + MLA_ROPE)
    q_nope, q_rope = wq[..., :MLA_NOPE], wq[..., MLA_NOPE:]
    wq_p = jnp.pad(wq, ((0, 0), (0, 0), (0, hi_pad))).reshape(MLA_Q_RANK, MLA_HEADS * LANES)
    q_swap = jnp.concatenate([jnp.zeros_like(q_nope), -q_rope[..., half:], q_rope[..., :half]], axis=-1)
    wq_s = jnp.pad(q_swap, ((0, 0), (0, 0), (0, hi_pad))).reshape(MLA_Q_RANK, MLA_HEADS * LANES)
    wkv = w_kv_up.reshape(MLA_KV_RANK, MLA_HEADS, MLA_NOPE + MLA_V)
    wk_p = jnp.pad(wkv[..., :MLA_NOPE], ((0, 0), (0, 0), (0, LANES - MLA_NOPE))).reshape(
        MLA_KV_RANK, MLA_HEADS * LANES)
    wv = wkv[..., MLA_NOPE:].reshape(MLA_KV_RANK, MLA_HEADS * MLA_V)
    bf = lambda a: a.astype(BF16)
    return dict(wz=bf(wz), wxbc=bf(wxbc), wdt=bf(wdt_p), wkr_p=bf(wkr_p), wkr_s=bf(wkr_s), wcq=bf(wcq),
                wckv=bf(wckv), wq_p=bf(wq_p), wq_s=bf(wq_s), wk_p=bf(wk_p), wv=bf(wv))


def _row(a):
    return a.reshape(1, -1).astype(F32)


def kernel(x, meta_tokens, mix_pre_g, mix_post_g, mlp_pre_g, mlp_post_g, w_up, w_down, w_in, ssd_conv_w,
           ssd_conv_b, ssd_dt_bias, ssd_a_log, ssd_d, ssd_norm_g, mla_q_norm_g, mla_w_q_up, mla_kv_norm_g,
           mla_w_kv_up, w_out_ab, rg_w_x, rg_w_y, rg_conv_w, rg_conv_b, rg_w_a, rg_b_a, rg_w_i, rg_b_i,
           rg_lambda, rg_w_out):
    b, seq, d = x.shape
    depth = mix_pre_g.shape[0]
    tp = PAD + N_META + seq
    n = b * tp
    meta = jnp.broadcast_to(meta_tokens[None].astype(x.dtype), (b, N_META, d))
    h = jnp.concatenate([jnp.zeros((b, PAD, d), x.dtype), meta, x], axis=1).reshape(n, d)
    tables = _rope_tables(tp)

    for layer in range(depth):
        if layer % 2 == 0:
            e = layer // 2
            w = _even_weights(w_in[e], mla_w_q_up[e], mla_w_kv_up[e])
            wts = (w["wz"], w["wxbc"], w["wdt"], w["wkr_p"], w["wkr_s"], w["wcq"], _row(mla_q_norm_g[e]),
                   w["wckv"], _row(mla_kv_norm_g[e]), w["wq_p"], w["wq_s"], w["wk_p"], w["wv"],
                   ssd_conv_w[e].astype(F32), _row(ssd_conv_b[e]))
            z, xbc, dt, q, k, v = _proj_even(h.reshape(b, tp, d), _row(mix_pre_g[layer]), wts, tables)
            lane_pad = lambda a: jnp.pad(_row(a), ((0, 0), (0, LANES - SSD_HEADS)))
            y_ssd = _ssd(xbc.reshape(b, tp, SSD_CONV_CH), dt.reshape(b, tp, LANES), z.reshape(b, tp, SSD_D_INNER),
                         lane_pad(ssd_dt_bias[e]), lane_pad(ssd_a_log[e]),
                         _row(jnp.repeat(ssd_d[e], SSD_HEAD_DIM)), _row(ssd_norm_g[e]))
            y_att = _attention(q, k, v, b, tp)
            wo = w_out_ab[e].astype(BF16)
            h = _post(h, (y_ssd.reshape(n, SSD_D_INNER), y_att), (wo[:SSD_D_INNER], wo[SSD_D_INNER:]),
                      _row(mix_post_g[layer]))
        else:
            o = layer // 2
            xr, gate = _proj_odd(h, _row(mix_pre_g[layer]), rg_w_x[o].astype(BF16), rg_w_y[o].astype(BF16),
                                 rg_conv_w[o].astype(F32), _row(rg_conv_b[o]))
            y = _rglru(xr.reshape(b, tp, LRU_WIDTH), gate.reshape(b, tp, LRU_WIDTH), rg_w_a[o].astype(BF16),
                       _row(rg_b_a[o]), rg_w_i[o].astype(BF16), _row(rg_b_i[o]), _row(rg_lambda[o]))
            h = _post(h, (y.reshape(n, LRU_WIDTH),), (rg_w_out[o].astype(BF16),), _row(mix_post_g[layer]))
        mlp_args = (_row(mlp_pre_g[layer]), w_up[layer].astype(BF16), w_down[layer].astype(BF16),
                    _row(mlp_post_g[layer]))
        if layer == depth - 1:
            return _mlp_last(h.reshape(b, tp, d), *mlp_args, PAD + N_META)
        h = _mlp(h, *mlp_args)
```

```python
import functools

import jax
import jax.numpy as jnp
import numpy as np
from jax import lax
from jax.experimental import pallas as pl
from jax.experimental.pallas import tpu as pltpu

F32 = jnp.float32
BF16 = jnp.bfloat16

N_META = 16
CHUNK = 128
PAD = CHUNK - N_META
EPS = 1e-6
SSD_HEADS = 16
SSD_HEAD_DIM = 64
SSD_D_INNER = SSD_HEADS * SSD_HEAD_DIM
SSD_GROUPS = 2
SSD_STATE = 128
SSD_CONV = 4
SSD_BC = SSD_GROUPS * SSD_STATE
SSD_CONV_CH = SSD_D_INNER + 2 * SSD_BC
MLA_HEADS = 16
MLA_NOPE = 64
MLA_ROPE = 32
MLA_V = 64
MLA_Q_RANK = 384
MLA_KV_RANK = 256
ROPE_BASE = 10000.0
LRU_BLOCKS = 10
LRU_BLOCK = 128
LRU_WIDTH = LRU_BLOCKS * LRU_BLOCK
LRU_C = 8.0
LRU_CONV_COLS = 640
LANES = 128
SUBLANES = 8
VMEM_LIMIT_BYTES = 56 * 1024 * 1024
NEG_BIG = -1e30


def _cparams(*sem):
    return pltpu.CompilerParams(dimension_semantics=sem, vmem_limit_bytes=VMEM_LIMIT_BYTES)


def _pick_tile(n, candidates):
    for c in candidates:
        if n % c == 0:
            return c
    raise ValueError(f"no tile in {candidates} divides {n}")


def _const_spec(shape):
    nd = len(shape)
    return pl.BlockSpec(shape, lambda *_: (0,) * nd)


def _rms(x, g):
    ms = jnp.mean(x * x, axis=-1, keepdims=True)
    return x * lax.rsqrt(ms + EPS) * g


def _sigmoid(x):
    return 1.0 / (1.0 + jnp.exp(-x))


def _softplus(x):
    return jnp.maximum(x, 0.0) + jnp.log(1.0 + jnp.exp(-jnp.abs(x)))


def _dot(a, b):
    return jnp.dot(a, b, preferred_element_type=F32)


def _dot_nt(a, b):
    return lax.dot_general(a, b, (((1,), (1,)), ((), ())), preferred_element_type=F32)


def _dot_tn(a, b):
    return lax.dot_general(a, b, (((0,), (0,)), ((), ())), preferred_element_type=F32)


def _shift_rows(cur, tail, k):
    if k == 0:
        return cur
    rolled = pltpu.roll(cur, k, 0)
    row = lax.broadcasted_iota(jnp.int32, tail.shape, 0)
    top = jnp.where(row < k, pltpu.roll(tail, k, 0), rolled[0:SUBLANES])
    return jnp.concatenate([top, rolled[SUBLANES:]], axis=0)


def _causal_conv(cur, tail_scr, w_ref, b_ref, cols):
    k = w_ref.shape[0]
    tail = tail_scr[:, cols]
    out = b_ref[:, cols]
    for i in range(k):
        out = out + _shift_rows(cur, tail, k - 1 - i) * w_ref[i:i + 1, cols]
    tail_scr[:, cols] = cur[cur.shape[0] - SUBLANES:]
    return out


CONV_COLS = 512


def _proj_even_kernel(h_ref, g_ref, wz_ref, wxbc_ref, wdt_ref, wkrp_ref, wkrs_ref, wcq_ref, qg_ref,
                      wckv_ref, kvg_ref, wqp_ref, wqs_ref, wkp_ref, wv_ref, cw_ref, cb_ref,
                      cq_ref, sq_ref, ck_ref, sk_ref,
                      z_ref, xbc_ref, dt_ref, q_ref, k_ref, v_ref, tail_scr):
    @pl.when(pl.program_id(1) == 0)
    def _():
        tail_scr[...] = jnp.zeros_like(tail_scr)

    hn = _rms(h_ref[...], g_ref[...]).astype(BF16)
    z_ref[...] = _dot(hn, wz_ref[...]).astype(BF16)
    for c0 in range(0, SSD_CONV_CH, CONV_COLS):
        cols = slice(c0, c0 + CONV_COLS)
        conv = _causal_conv(_dot(hn, wxbc_ref[:, cols]), tail_scr, cw_ref, cb_ref, cols)
        xbc_ref[:, cols] = (conv * _sigmoid(conv)).astype(BF16)
    dt_ref[...] = _dot(hn, wdt_ref[...])
    kr = _dot(hn, wkrp_ref[...]) * ck_ref[...] + _dot(hn, wkrs_ref[...]) * sk_ref[...]
    cqn = _rms(_dot(hn, wcq_ref[...]), qg_ref[...]).astype(BF16)
    ckvn = _rms(_dot(hn, wckv_ref[...]), kvg_ref[...]).astype(BF16)
    v_ref[...] = _dot(ckvn, wv_ref[...]).astype(BF16)
    cq_t = jnp.concatenate([cq_ref[...]] * 2, axis=1)
    sq_t = jnp.concatenate([sq_ref[...]] * 2, axis=1)
    kr2 = jnp.concatenate([kr] * 2, axis=1)
    for hd in range(0, MLA_HEADS, 2):
        sl = slice(hd * LANES, (hd + 2) * LANES)
        qh = (_dot(cqn, wqp_ref[:, sl]) * cq_t + _dot(cqn, wqs_ref[:, sl]) * sq_t).astype(BF16)
        kh = (_dot(ckvn, wkp_ref[:, sl]) + kr2).astype(BF16)
        for j in range(2):
            q_ref[hd + j] = qh[:, j * LANES:(j + 1) * LANES]
            k_ref[hd + j] = kh[:, j * LANES:(j + 1) * LANES]


def _proj_even(h3, g, wts, tables):
    b, tp, d = h3.shape
    tm = _pick_tile(tp, (384, 128))
    nt = tp // tm
    n = b * tp
    row3 = lambda bi, i: (bi, i, 0)
    flat = lambda bi, i: (bi * nt + i, 0)
    tab = lambda bi, i: (i, 0)
    in_specs = [pl.BlockSpec((None, tm, d), row3), _const_spec(g.shape)]
    in_specs += [_const_spec(w.shape) for w in wts]
    in_specs += [pl.BlockSpec((tm, LANES), tab) for _ in tables]
    out_shape = (
        jax.ShapeDtypeStruct((n, SSD_D_INNER), BF16),
        jax.ShapeDtypeStruct((n, SSD_CONV_CH), BF16),
        jax.ShapeDtypeStruct((n, LANES), F32),
        jax.ShapeDtypeStruct((MLA_HEADS, n, LANES), BF16),
        jax.ShapeDtypeStruct((MLA_HEADS, n, LANES), BF16),
        jax.ShapeDtypeStruct((n, MLA_HEADS * MLA_V), BF16),
    )
    head3 = lambda bi, i: (0, bi * nt + i, 0)
    out_specs = (
        pl.BlockSpec((tm, SSD_D_INNER), flat),
        pl.BlockSpec((tm, SSD_CONV_CH), flat),
        pl.BlockSpec((tm, LANES), flat),
        pl.BlockSpec((MLA_HEADS, tm, LANES), head3),
        pl.BlockSpec((MLA_HEADS, tm, LANES), head3),
        pl.BlockSpec((tm, MLA_HEADS * MLA_V), flat),
    )
    return pl.pallas_call(
        _proj_even_kernel, grid=(b, nt), in_specs=in_specs, out_specs=out_specs, out_shape=out_shape,
        scratch_shapes=[pltpu.VMEM((SUBLANES, SSD_CONV_CH), F32)],
        compiler_params=_cparams("parallel", "arbitrary"), name="proj_even",
    )(h3, g, *wts, *tables)


def _cumsum_rows(x):
    n = x.shape[0]
    row = lax.broadcasted_iota(jnp.int32, x.shape, 0)
    d = 1
    while d < n:
        x = x + jnp.where(row >= d, pltpu.roll(x, d, 0), 0.0)
        d *= 2
    return x


SSD_BATCH_BLOCK = 4


def _ssd_chunk(c, xbc, dt_raw, zf, dtb, alog, dsk, ng, state_ref, y_scr):
    L = CHUNK
    heads_per_group = SSD_HEADS // SSD_GROUPS
    row = lax.broadcasted_iota(jnp.int32, (L, 1), 0)
    valid = (c * L + row) >= PAD
    act = jnp.where(valid, xbc.astype(F32), 0.0)
    xs = act[:, :SSD_D_INNER]
    bm = act[:, SSD_D_INNER:SSD_D_INNER + SSD_BC]
    cm = act[:, SSD_D_INNER + SSD_BC:]

    dtv = jnp.where(valid, _softplus(dt_raw + dtb), 0.0)
    a_cs = _cumsum_rows(dtv * (-jnp.exp(alog)))
    a_cs_t = a_cs.T
    dt_t = dtv.T
    tot = a_cs[L - 1:L, :]
    w_t = jnp.exp(a_cs_t[:, L - 1:L] - a_cs_t) * dt_t
    cd = jnp.exp(tot)
    causal = lax.broadcasted_iota(jnp.int32, (L, L), 0) >= lax.broadcasted_iota(jnp.int32, (L, L), 1)
    lo = lax.broadcasted_iota(jnp.int32, (L, LANES), 1) < SSD_HEAD_DIM

    for g in range(SSD_GROUPS):
        cc = cm[:, g * SSD_STATE:(g + 1) * SSD_STATE]
        bc = bm[:, g * SSD_STATE:(g + 1) * SSD_STATE]
        bc_t = bc.T
        cb = _dot_nt(cc.astype(BF16), bc.astype(BF16))
        for hp in range(heads_per_group // 2):
            pair = g * (heads_per_group // 2) + hp
            sl = slice(pair * LANES, (pair + 1) * LANES)
            xs_pair = xs[:, sl]
            st_pair = state_ref[:, sl]
            xs_b = xs_pair.astype(BF16)
            rhs = jnp.concatenate([xs_b, st_pair.astype(BF16)], axis=0)
            ys, sts, cds = [], [], []
            for hh in range(2):
                h = 2 * pair + hh
                a_col = jnp.broadcast_to(a_cs[:, h:h + 1], (L, L))
                dec = jnp.exp(jnp.where(causal, a_col - a_cs_t[h:h + 1, :], -jnp.inf))
                m = cb * dec * dt_t[h:h + 1, :]
                lhs = jnp.concatenate([m, cc * jnp.exp(a_col)], axis=1).astype(BF16)
                ys.append(_dot(lhs, rhs))
                sts.append(_dot((bc_t * w_t[h:h + 1, :]).astype(BF16), xs_b))
                cds.append(cd[:, h:h + 1])
            y_scr[:, sl] = jnp.where(lo, ys[0], ys[1]) + dsk[:, sl] * xs_pair
            state_ref[:, sl] = jnp.where(lo, cds[0], cds[1]) * st_pair + jnp.where(lo, sts[0], sts[1])

    return _rms(y_scr[...] * (zf * _sigmoid(zf)), ng).astype(BF16)


def _ssd_kernel(xbc_ref, dt_ref, z_ref, dtb_ref, alog_ref, dsk_ref, ng_ref, y_ref, state_scr, y_scr):
    c = pl.program_id(1)

    @pl.when(c == 0)
    def _():
        state_scr[...] = jnp.zeros_like(state_scr)

    for bb in range(xbc_ref.shape[0]):
        y_ref[bb] = _ssd_chunk(c, xbc_ref[bb], dt_ref[bb], z_ref[bb].astype(F32), dtb_ref[...], alog_ref[...],
                               dsk_ref[...], ng_ref[...], state_scr.at[bb], y_scr.at[bb])


def _ssd(xbc3, dt3, z3, dtb, alog, dsk, ng):
    b, tp, _ = xbc3.shape
    nc = tp // CHUNK
    gb = _pick_tile(b, (SSD_BATCH_BLOCK, 1))
    row3 = lambda bi, ci: (bi, ci, 0)
    in_specs = [
        pl.BlockSpec((gb, CHUNK, SSD_CONV_CH), row3),
        pl.BlockSpec((gb, CHUNK, LANES), row3),
        pl.BlockSpec((gb, CHUNK, SSD_D_INNER), row3),
    ] + [_const_spec(a.shape) for a in (dtb, alog, dsk, ng)]
    return pl.pallas_call(
        _ssd_kernel, grid=(b // gb, nc), in_specs=in_specs,
        out_specs=pl.BlockSpec((gb, CHUNK, SSD_D_INNER), row3),
        out_shape=jax.ShapeDtypeStruct((b, tp, SSD_D_INNER), BF16),
        scratch_shapes=[pltpu.VMEM((gb, SSD_STATE, SSD_D_INNER), F32),
                        pltpu.VMEM((gb, CHUNK, SSD_D_INNER), F32)],
        compiler_params=_cparams("parallel", "arbitrary"), name="ssd",
    )(xbc3, dt3, z3, dtb, alog, dsk, ng)


ATTN_KV_BLOCK = 256
ATTN_V_ROWS = MLA_V + 16


def _attn_kernel(q_ref, k_ref, v_ref, o_ref, vt_scr, s_scr, bmax_scr, p_scr, alpha_scr, m_scr, acc_scr, *, tq):
    kb = ATTN_KV_BLOCK
    qi = pl.program_id(2)
    q0 = qi * tq
    end = q0 + tq
    n_int = q0 // kb
    n_blocks = (end + kb - 1) // kb
    last_off = end - kb

    @pl.when(qi == 0)
    def _():
        ones = jnp.ones((ATTN_V_ROWS - MLA_V, CHUNK), BF16)
        for c in range(vt_scr.shape[0]):
            vt = v_ref[c * CHUNK:(c + 1) * CHUNK, :].astype(F32).T.astype(BF16)
            for hh in range(2):
                vt_scr[c, hh, :MLA_V, :] = vt[hh * MLA_V:(hh + 1) * MLA_V, :]
                vt_scr[c, hh, MLA_V:, :] = ones

    m_scr[...] = jnp.full_like(m_scr, NEG_BIG)
    acc_scr[...] = jnp.zeros_like(acc_scr)

    def offset(bk):
        return pl.multiple_of(jnp.minimum(bk * kb, last_off), CHUNK)

    levels = tuple(range(0, tq - kb + 1, kb))

    def level(bk):
        return jnp.minimum(jnp.maximum(offset(bk) - q0, 0) // kb, len(levels) - 1) * kb

    def qk(hh, bk, masked, cs):
        off = offset(bk)
        s = _dot_nt(k_ref[hh, pl.ds(off, kb), :], q_ref[hh, cs:, :])
        if masked:
            key = lax.broadcasted_iota(jnp.int32, s.shape, 0)
            keep = (key - lax.broadcasted_iota(jnp.int32, s.shape, 1)) <= (q0 + cs - off)
            first_key = jnp.where(bk == 0, PAD, bk * kb - off)
            s = jnp.where(keep, s, NEG_BIG)
            s = jnp.where(key >= first_key, s, NEG_BIG)
        s_scr[hh, :, cs:] = s
        bmax_scr[hh, :, cs:] = jnp.broadcast_to(jnp.max(s, axis=0, keepdims=True), (SUBLANES, tq - cs))

    def softmax(hh, cs):
        m_prev = m_scr[hh, 0:1, cs:]
        m_new = jnp.maximum(m_prev, bmax_scr[hh, 0:1, cs:])
        alpha_scr[hh, :, cs:] = jnp.broadcast_to(jnp.exp2(m_prev - m_new), (SUBLANES, tq - cs))
        m_scr[hh, :, cs:] = jnp.broadcast_to(m_new, (SUBLANES, tq - cs))
        p_scr[hh, :, cs:] = jnp.exp2(s_scr[hh, :, cs:] - m_new).astype(BF16)

    def pv(hh, bk, cs):
        c0 = offset(bk) // CHUNK
        vt = jnp.concatenate([vt_scr[c0 + c, hh] for c in range(kb // CHUNK)], axis=1)
        acc_scr[hh, :, cs:] = alpha_scr[hh, 0:1, cs:] * acc_scr[hh, :, cs:] + _dot(vt, p_scr[hh, :, cs:])

    def step(bk, masked, cs_prev, cs):
        softmax(1, cs_prev)
        qk(0, bk, masked, cs)
        pv(0, bk - 1, cs_prev)
        softmax(0, cs)
        qk(1, bk, masked, cs)
        pv(1, bk - 1, cs_prev)

    def plain_step(bk, carry):
        step(bk, False, 0, 0)
        return carry

    def masked_step(bk, carry):
        lvl_prev, lvl = level(bk - 1), level(bk)
        for a in levels:
            for b in (a, a + kb):
                if b in levels:
                    @pl.when(jnp.logical_and(lvl_prev == a, lvl == b))
                    def _():
                        step(bk, True, a, b)
        return carry

    qk(0, 0, True, 0)
    softmax(0, 0)
    qk(1, 0, True, 0)
    lax.fori_loop(1, n_int, plain_step, 0)
    lax.fori_loop(jnp.maximum(n_int, 1), n_blocks, masked_step, 0)
    lvl_last = level(n_blocks - 1)
    for a in levels:
        @pl.when(lvl_last == a)
        def _():
            softmax(1, a)
            pv(0, n_blocks - 1, a)
            pv(1, n_blocks - 1, a)

    out_t = jnp.concatenate([acc_scr[hh, :MLA_V, :] / acc_scr[hh, MLA_V:MLA_V + 1, :] for hh in range(2)],
                            axis=0)
    qpos = q0 + lax.broadcasted_iota(jnp.int32, (1, tq), 1)
    out_t = jnp.where(qpos >= PAD, out_t, 0.0)
    for c in range(tq // CHUNK):
        o_ref[c * CHUNK:(c + 1) * CHUNK, :] = out_t[:, c * CHUNK:(c + 1) * CHUNK].T.astype(BF16)


def _attention(q, k, v, b, tp):
    tq = _pick_tile(tp, (1408, 384, 256))
    nq = tp // tq
    n = b * tp
    in_specs = [
        pl.BlockSpec((2, tq, LANES), lambda bi, hp, qi: (hp, bi * nq + qi, 0)),
        pl.BlockSpec((2, tp, LANES), lambda bi, hp, qi: (hp, bi, 0)),
        pl.BlockSpec((tp, LANES), lambda bi, hp, qi: (bi, hp)),
    ]
    return pl.pallas_call(
        functools.partial(_attn_kernel, tq=tq), grid=(b, MLA_HEADS // 2, nq), in_specs=in_specs,
        out_specs=pl.BlockSpec((tq, LANES), lambda bi, hp, qi: (bi * nq + qi, hp)),
        out_shape=jax.ShapeDtypeStruct((n, MLA_HEADS * MLA_V), BF16),
        scratch_shapes=[pltpu.VMEM((tp // CHUNK, 2, ATTN_V_ROWS, CHUNK), BF16),
                        pltpu.VMEM((2, ATTN_KV_BLOCK, tq), F32), pltpu.VMEM((2, SUBLANES, tq), F32),
                        pltpu.VMEM((2, ATTN_KV_BLOCK, tq), BF16),
                        pltpu.VMEM((2, SUBLANES, tq), F32), pltpu.VMEM((2, SUBLANES, tq), F32),
                        pltpu.VMEM((2, ATTN_V_ROWS, tq), F32)],
        compiler_params=_cparams("parallel", "parallel", "arbitrary"), name="attention",
    )(q, k, v)


def _post_kernel(*refs, n_in):
    h_ref = refs[0]
    a_refs = refs[1:1 + n_in]
    w_refs = refs[1 + n_in:1 + 2 * n_in]
    g_ref = refs[1 + 2 * n_in]
    o_ref = refs[2 + 2 * n_in]
    m = _dot(a_refs[0][...], w_refs[0][...])
    for a_ref, w_ref in zip(a_refs[1:], w_refs[1:]):
        m = m + _dot(a_ref[...], w_ref[...])
    o_ref[...] = h_ref[...] + _rms(m, g_ref[...])


def _post(h2, acts, wts, g):
    n, d = h2.shape
    tm = _pick_tile(n, (1024, 512, 384, 256, 128))
    rows = lambda i: (i, 0)
    in_specs = [pl.BlockSpec((tm, d), rows)]
    in_specs += [pl.BlockSpec((tm, a.shape[1]), rows) for a in acts]
    in_specs += [_const_spec(w.shape) for w in wts] + [_const_spec(g.shape)]
    return pl.pallas_call(
        functools.partial(_post_kernel, n_in=len(acts)), grid=(n // tm,), in_specs=in_specs,
        out_specs=pl.BlockSpec((tm, d), rows), out_shape=jax.ShapeDtypeStruct((n, d), F32),
        compiler_params=_cparams("parallel"), name="mixer_out",
    )(h2, *acts, *wts, g)


def _mlp_kernel(h_ref, g1_ref, wup_ref, wdn_ref, g2_ref, o_ref, *, ff_chunk):
    x = h_ref[...]
    hn = _rms(x, g1_ref[...]).astype(BF16)
    d_ff = wup_ref.shape[1]
    y = None
    for c0 in range(0, d_ff, ff_chunk):
        u = jnp.maximum(_dot(hn, wup_ref[:, c0:c0 + ff_chunk]), 0.0)
        part = _dot((u * u).astype(BF16), wdn_ref[c0:c0 + ff_chunk, :])
        y = part if y is None else y + part
    o_ref[...] = x + _rms(y, g2_ref[...])


def _mlp(h2, g1, wup, wdn, g2):
    n, d = h2.shape
    tm = _pick_tile(n, (512, 256, 128))
    rows = lambda i: (i, 0)
    in_specs = [pl.BlockSpec((tm, d), rows), _const_spec(g1.shape), _const_spec(wup.shape),
                _const_spec(wdn.shape), _const_spec(g2.shape)]
    return pl.pallas_call(
        functools.partial(_mlp_kernel, ff_chunk=1024), grid=(n // tm,), in_specs=in_specs,
        out_specs=pl.BlockSpec((tm, d), rows), out_shape=jax.ShapeDtypeStruct((n, d), F32),
        compiler_params=_cparams("parallel"), name="mlp",
    )(h2, g1, wup, wdn, g2)


def _mlp_last(h3, g1, wup, wdn, g2, lead):
    b, tp, d = h3.shape
    seq = tp - lead
    tm = _pick_tile(seq, (512, 256, 128))
    in_specs = [pl.BlockSpec((pl.Element(tm), pl.Element(d)), lambda bi, i: (pl.multiple_of(bi * tp + lead + i * tm, CHUNK), 0)),
                _const_spec(g1.shape), _const_spec(wup.shape), _const_spec(wdn.shape), _const_spec(g2.shape)]
    return pl.pallas_call(
        functools.partial(_mlp_kernel, ff_chunk=1024), grid=(b, seq // tm), in_specs=in_specs,
        out_specs=pl.BlockSpec((None, tm, d), lambda bi, i: (bi, i, 0)),
        out_shape=jax.ShapeDtypeStruct((b, seq, d), F32),
        compiler_params=_cparams("parallel", "parallel"), name="mlp_last",
    )(h3.reshape(b * tp, d), g1, wup, wdn, g2)


SEG_LEN = CHUNK // SUBLANES


def _interleave_matrix():
    p = np.zeros((CHUNK, CHUNK), np.float32)
    for j in range(SEG_LEN):
        for s in range(SUBLANES):
            p[SUBLANES * j + s, SEG_LEN * s + j] = 1.0
    return p


def _reorder_rows(perm, x):
    tiles = [_dot(perm, x[t0:t0 + CHUNK]).astype(BF16) for t0 in range(0, x.shape[0], CHUNK)]
    return tiles[0] if len(tiles) == 1 else jnp.concatenate(tiles, axis=0)


def _causal_conv_interleaved(pre, tail_scr, w_ref, b_ref, cols):
    k = w_ref.shape[0]
    halo = (k - 1) * SUBLANES
    width = pre.shape[1]
    first_sub = lax.broadcasted_iota(jnp.int32, (k - 1, SUBLANES, width), 1) == 0
    prev_last = tail_scr[:, cols]
    outs = []
    for t0 in range(0, pre.shape[0], CHUNK):
        tile = pre[t0:t0 + CHUNK]
        last = tile[CHUNK - halo:]
        fix = jnp.where(first_sub, pltpu.roll(prev_last.reshape(k - 1, SUBLANES, width), 1, 1),
                        pltpu.roll(last.reshape(k - 1, SUBLANES, width), 1, 1)).reshape(halo, width)
        out = b_ref[:, cols]
        for i in range(k):
            shift = k - 1 - i
            if shift == 0:
                xs = tile
            else:
                xs = jnp.concatenate([fix[halo - shift * SUBLANES:], tile[:CHUNK - shift * SUBLANES]], axis=0)
            out = out + xs * w_ref[i:i + 1, cols]
        outs.append(out)
        prev_last = last
    tail_scr[:, cols] = prev_last
    return jnp.concatenate(outs, axis=0)


def _proj_odd_kernel(h_ref, g_ref, wx_ref, wy_ref, cw_ref, cb_ref, perm_ref, x_ref, gate_ref, tail_scr):
    @pl.when(pl.program_id(0) == 0)
    def _():
        tail_scr[...] = jnp.zeros_like(tail_scr)

    hn = _reorder_rows(perm_ref[...], _rms(h_ref[...], g_ref[...]).astype(BF16))
    for c0 in range(0, LRU_WIDTH, LRU_CONV_COLS):
        cols = slice(c0, c0 + LRU_CONV_COLS)
        x_ref[:, cols] = _causal_conv_interleaved(_dot(hn, wx_ref[:, cols]), tail_scr, cw_ref, cb_ref,
                                                  cols).astype(BF16)
    y = _dot(hn, wy_ref[...])
    gelu = 0.5 * y * (1.0 + jnp.tanh(np.sqrt(2.0 / np.pi).astype(np.float32) * (y + 0.044715 * (y * y * y))))
    gate_ref[...] = gelu.astype(BF16)


def _proj_odd(h2, g, wx, wy, cw, cb, perm):
    n, d = h2.shape
    tm = _pick_tile(n, (1024, 512, 384, 256, 128))
    rows = lambda i: (i, 0)
    w = wx.shape[1]
    return pl.pallas_call(
        _proj_odd_kernel, grid=(n // tm,),
        in_specs=[pl.BlockSpec((tm, d), rows)] + [_const_spec(a.shape) for a in (g, wx, wy, cw, cb, perm)],
        out_specs=(pl.BlockSpec((tm, w), rows), pl.BlockSpec((tm, w), rows)),
        out_shape=(jax.ShapeDtypeStruct((n, w), BF16), jax.ShapeDtypeStruct((n, w), BF16)),
        scratch_shapes=[pltpu.VMEM(((cw.shape[0] - 1) * SUBLANES, w), F32)],
        compiler_params=_cparams("arbitrary"), name="proj_odd",
    )(h2, g, wx, wy, cw, cb, perm)


LRU_BATCH_BLOCK = 4


def _rglru_kernel(x_ref, gate_ref, wa_ref, ba_ref, wi_ref, bi_ref, lam_ref, unperm_ref, o_ref, h_scr):
    t = pl.program_id(1)

    @pl.when(t == 0)
    def _():
        h_scr[...] = jnp.zeros_like(h_scr)

    for bb in range(x_ref.shape[0]):
        _rglru_tile(t, x_ref.at[bb], gate_ref.at[bb], wa_ref, ba_ref, wi_ref, bi_ref, lam_ref, unperm_ref,
                    o_ref.at[bb], h_scr.at[bb])


def _rglru_tile(t, x_ref, gate_ref, wa_ref, ba_ref, wi_ref, bi_ref, lam_ref, unperm_ref, o_ref, h_scr):
    L = x_ref.shape[0]
    xr = x_ref[...].astype(F32)
    rs, gs = [], []
    for nb in range(LRU_BLOCKS):
        sl = slice(nb * LRU_BLOCK, (nb + 1) * LRU_BLOCK)
        xb = x_ref[:, sl]
        rs.append(_dot(xb, wa_ref[nb]))
        gs.append(_dot(xb, wi_ref[nb]))
    r = _sigmoid(jnp.concatenate(rs, axis=1) + ba_ref[...])
    ig = _sigmoid(jnp.concatenate(gs, axis=1) + bi_ref[...])
    log_a = (-LRU_C) * r * _softplus(-lam_ref[...])
    a = jnp.exp(log_a)
    row = lax.broadcasted_iota(jnp.int32, (L, 1), 0)
    valid = (t * L + SEG_LEN * (row % SUBLANES) + row // SUBLANES) >= PAD
    om = 1.0 - a * a
    root = jnp.where(om > 0.0, om * lax.rsqrt(om), 0.0)
    u = jnp.where(valid, root * (ig * xr), 0.0)

    slab = lambda v, j: v[j * SUBLANES:(j + 1) * SUBLANES, :]
    h_loc = [slab(u, 0)]
    prod = [slab(a, 0)]
    for j in range(1, SEG_LEN):
        h_loc.append(slab(a, j) * h_loc[-1] + slab(u, j))
        prod.append(slab(a, j) * prod[-1])
    carry_in = h_scr[0:1, :]
    carries = []
    for s in range(SUBLANES):
        carries.append(carry_in)
        carry_in = h_loc[-1][s:s + 1, :] + prod[-1][s:s + 1, :] * carry_in
    h_scr[...] = jnp.broadcast_to(carry_in, h_scr.shape)
    carry = jnp.concatenate(carries, axis=0)
    hs = jnp.concatenate([h_loc[j] + prod[j] * carry for j in range(SEG_LEN)], axis=0)
    o_ref[...] = _reorder_rows(unperm_ref[...], (hs * gate_ref[...].astype(F32)).astype(BF16))


def _rglru(x3, gate3, wa, ba, wi, bi, lam, unperm):
    b, tp, w = x3.shape
    tt = CHUNK
    gb = _pick_tile(b, (LRU_BATCH_BLOCK, 1))
    row3 = lambda bi_, ti: (bi_, ti, 0)
    in_specs = [pl.BlockSpec((gb, tt, w), row3), pl.BlockSpec((gb, tt, w), row3)]
    in_specs += [_const_spec(a.shape) for a in (wa, ba, wi, bi, lam, unperm)]
    return pl.pallas_call(
        _rglru_kernel, grid=(b // gb, tp // tt), in_specs=in_specs,
        out_specs=pl.BlockSpec((gb, tt, w), row3),
        out_shape=jax.ShapeDtypeStruct((b, tp, w), BF16),
        scratch_shapes=[pltpu.VMEM((gb, SUBLANES, w), F32)],
        compiler_params=_cparams("parallel", "arbitrary"), name="rglru",
    )(x3, gate3, wa, ba, wi, bi, lam, unperm)


def _rope_tables(tp):
    f32 = np.float32
    inv = f32(ROPE_BASE) ** (-np.arange(0, MLA_ROPE, 2, dtype=f32) / f32(MLA_ROPE))
    pos = np.maximum(np.arange(tp, dtype=f32) - f32(PAD), f32(0))
    ang = (pos[:, None] * inv[None, :]).astype(f32)
    cos, sin = np.cos(ang).astype(f32), np.sin(ang).astype(f32)
    zeros_lo = np.zeros((tp, MLA_NOPE), f32)
    zeros_hi = np.zeros((tp, LANES - MLA_NOPE - MLA_ROPE), f32)
    ck = np.concatenate([zeros_lo, cos, cos, zeros_hi], axis=1)
    sk = np.concatenate([zeros_lo, sin, sin, zeros_hi], axis=1)
    scale = f32((MLA_NOPE + MLA_ROPE) ** -0.5 * np.log2(np.e))
    cq = np.concatenate([np.ones((tp, MLA_NOPE), f32), cos, cos, zeros_hi], axis=1) * scale
    sq = sk * scale
    return tuple(jnp.asarray(t) for t in (cq, sq, ck, sk))


def _even_weights(w_in, w_q_up, w_kv_up):
    offs = np.cumsum((SSD_D_INNER, SSD_CONV_CH, SSD_HEADS, MLA_Q_RANK, MLA_KV_RANK, MLA_ROPE))
    wz = w_in[:, :offs[0]]
    wxbc = w_in[:, offs[0]:offs[1]]
    wdt = w_in[:, offs[1]:offs[2]]
    wcq = w_in[:, offs[2]:offs[3]]
    wckv = w_in[:, offs[3]:offs[4]]
    wkr = w_in[:, offs[4]:offs[5]]
    half = MLA_ROPE // 2
    hi_pad = LANES - MLA_NOPE - MLA_ROPE
    wdt_p = jnp.pad(wdt, ((0, 0), (0, LANES - SSD_HEADS)))
    wkr_p = jnp.pad(wkr, ((0, 0), (MLA_NOPE, hi_pad)))
    wkr_s = jnp.pad(jnp.concatenate([-wkr[:, half:], wkr[:, :half]], axis=1), ((0, 0), (MLA_NOPE, hi_pad)))
    wq = w_q_up.reshape(MLA_Q_RANK, MLA_HEADS, MLA_NOPE + MLA_ROPE)
    q_nope, q_rope = wq[..., :MLA_NOPE], wq[..., MLA_NOPE:]
    wq_p = jnp.pad(wq, ((0, 0), (0, 0), (0, hi_pad))).reshape(MLA_Q_RANK, MLA_HEADS * LANES)
    q_swap = jnp.concatenate([jnp.zeros_like(q_nope), -q_rope[..., half:], q_rope[..., :half]], axis=-1)
    wq_s = jnp.pad(q_swap, ((0, 0), (0, 0), (0, hi_pad))).reshape(MLA_Q_RANK, MLA_HEADS * LANES)
    wkv = w_kv_up.reshape(MLA_KV_RANK, MLA_HEADS, MLA_NOPE + MLA_V)
    wk_p = jnp.pad(wkv[..., :MLA_NOPE], ((0, 0), (0, 0), (0, LANES - MLA_NOPE))).reshape(
        MLA_KV_RANK, MLA_HEADS * LANES)
    wv = wkv[..., MLA_NOPE:].reshape(MLA_KV_RANK, MLA_HEADS * MLA_V)
    bf = lambda a: a.astype(BF16)
    return dict(wz=bf(wz), wxbc=bf(wxbc), wdt=bf(wdt_p), wkr_p=bf(wkr_p), wkr_s=bf(wkr_s), wcq=bf(wcq),
                wckv=bf(wckv), wq_p=bf(wq_p), wq_s=bf(wq_s), wk_p=bf(wk_p), wv=bf(wv))


def _row(a):
    return a.reshape(1, -1).astype(F32)


def kernel(x, meta_tokens, mix_pre_g, mix_post_g, mlp_pre_g, mlp_post_g, w_up, w_down, w_in, ssd_conv_w,
           ssd_conv_b, ssd_dt_bias, ssd_a_log, ssd_d, ssd_norm_g, mla_q_norm_g, mla_w_q_up, mla_kv_norm_g,
           mla_w_kv_up, w_out_ab, rg_w_x, rg_w_y, rg_conv_w, rg_conv_b, rg_w_a, rg_b_a, rg_w_i, rg_b_i,
           rg_lambda, rg_w_out):
    b, seq, d = x.shape
    depth = mix_pre_g.shape[0]
    tp = PAD + N_META + seq
    n = b * tp
    meta = jnp.broadcast_to(meta_tokens[None].astype(x.dtype), (b, N_META, d))
    h = jnp.concatenate([jnp.zeros((b, PAD, d), x.dtype), meta, x], axis=1).reshape(n, d)
    tables = _rope_tables(tp)

    for layer in range(depth):
        if layer % 2 == 0:
            e = layer // 2
            w = _even_weights(w_in[e], mla_w_q_up[e], mla_w_kv_up[e])
            wts = (w["wz"], w["wxbc"], w["wdt"], w["wkr_p"], w["wkr_s"], w["wcq"], _row(mla_q_norm_g[e]),
                   w["wckv"], _row(mla_kv_norm_g[e]), w["wq_p"], w["wq_s"], w["wk_p"], w["wv"],
                   ssd_conv_w[e].astype(F32), _row(ssd_conv_b[e]))
            z, xbc, dt, q, k, v = _proj_even(h.reshape(b, tp, d), _row(mix_pre_g[layer]), wts, tables)
            lane_pad = lambda a: jnp.pad(_row(a), ((0, 0), (0, LANES - SSD_HEADS)))
            y_ssd = _ssd(xbc.reshape(b, tp, SSD_CONV_CH), dt.reshape(b, tp, LANES), z.reshape(b, tp, SSD_D_INNER),
                         lane_pad(ssd_dt_bias[e]), lane_pad(ssd_a_log[e]),
                         _row(jnp.repeat(ssd_d[e], SSD_HEAD_DIM)), _row(ssd_norm_g[e]))
            y_att = _attention(q, k, v, b, tp)
            wo = w_out_ab[e].astype(BF16)
            h = _post(h, (y_ssd.reshape(n, SSD_D_INNER), y_att), (wo[:SSD_D_INNER], wo[SSD_D_INNER:]),
                      _row(mix_post_g[layer]))
        else:
            o = layer // 2
            perm = _interleave_matrix()
            xr, gate = _proj_odd(h, _row(mix_pre_g[layer]), rg_w_x[o].astype(BF16), rg_w_y[o].astype(BF16),
                                 rg_conv_w[o].astype(F32), _row(rg_conv_b[o]), jnp.asarray(perm, BF16))
            y = _rglru(xr.reshape(b, tp, LRU_WIDTH), gate.reshape(b, tp, LRU_WIDTH), rg_w_a[o].astype(BF16),
                       _row(rg_b_a[o]), rg_w_i[o].astype(BF16), _row(rg_b_i[o]), _row(rg_lambda[o]),
                       jnp.asarray(perm.T, BF16))
            h = _post(h, (y.reshape(n, LRU_WIDTH),), (rg_w_out[o].astype(BF16),), _row(mix_post_g[layer]))
        mlp_args = (_row(mlp_pre_g[layer]), w_up[layer].astype(BF16), w_down[layer].astype(BF16),
                    _row(mlp_post_g[layer]))
        if layer == depth - 1:
            return _mlp_last(h.reshape(b, tp, d), *mlp_args, PAD + N_META)
        h = _mlp(h, *mlp_args)
```

```python
import functools

import jax
import jax.numpy as jnp
import numpy as np
from jax import lax
from jax.experimental import pallas as pl
from jax.experimental.pallas import tpu as pltpu

F32 = jnp.float32
BF16 = jnp.bfloat16

N_META = 16
CHUNK = 128
PAD = CHUNK - N_META
EPS = 1e-6
SSD_HEADS = 16
SSD_HEAD_DIM = 64
SSD_D_INNER = SSD_HEADS * SSD_HEAD_DIM
SSD_GROUPS = 2
SSD_STATE = 128
SSD_CONV = 4
SSD_BC = SSD_GROUPS * SSD_STATE
SSD_CONV_CH = SSD_D_INNER + 2 * SSD_BC
MLA_HEADS = 16
MLA_NOPE = 64
MLA_ROPE = 32
MLA_V = 64
MLA_Q_RANK = 384
MLA_KV_RANK = 256
ROPE_BASE = 10000.0
LRU_BLOCKS = 10
LRU_BLOCK = 128
LRU_WIDTH = LRU_BLOCKS * LRU_BLOCK
LRU_C = 8.0
LRU_CONV_COLS = 256
LANES = 128
SUBLANES = 8
VMEM_LIMIT_BYTES = 56 * 1024 * 1024
NEG_BIG = -1e30


def _cparams(*sem):
    return pltpu.CompilerParams(dimension_semantics=sem, vmem_limit_bytes=VMEM_LIMIT_BYTES)


def _pick_tile(n, candidates):
    for c in candidates:
        if n % c == 0:
            return c
    raise ValueError(f"no tile in {candidates} divides {n}")


def _const_spec(shape):
    nd = len(shape)
    return pl.BlockSpec(shape, lambda *_: (0,) * nd)


def _rms(x, g):
    ms = jnp.mean(x * x, axis=-1, keepdims=True)
    return x * lax.rsqrt(ms + EPS) * g


def _sigmoid(x):
    return 1.0 / (1.0 + jnp.exp(-x))


def _softplus(x):
    return jnp.maximum(x, 0.0) + jnp.log(1.0 + jnp.exp(-jnp.abs(x)))


def _dot(a, b):
    return jnp.dot(a, b, preferred_element_type=F32)


def _dot_nt(a, b):
    return lax.dot_general(a, b, (((1,), (1,)), ((), ())), preferred_element_type=F32)


def _dot_tn(a, b):
    return lax.dot_general(a, b, (((0,), (0,)), ((), ())), preferred_element_type=F32)


def _shift_rows(cur, tail, k):
    if k == 0:
        return cur
    rolled = pltpu.roll(cur, k, 0)
    row = lax.broadcasted_iota(jnp.int32, tail.shape, 0)
    top = jnp.where(row < k, pltpu.roll(tail, k, 0), rolled[0:SUBLANES])
    return jnp.concatenate([top, rolled[SUBLANES:]], axis=0)


def _causal_conv(cur, tail_scr, w_ref, b_ref, cols):
    k = w_ref.shape[0]
    tail = tail_scr[:, cols]
    out = b_ref[:, cols]
    for i in range(k):
        out = out + _shift_rows(cur, tail, k - 1 - i) * w_ref[i:i + 1, cols]
    tail_scr[:, cols] = cur[cur.shape[0] - SUBLANES:]
    return out


CONV_COLS = 256
Z_COLS = 256


def _rotate_half(x):
    half = MLA_ROPE // 2
    lane = lax.broadcasted_iota(jnp.int32, x.shape, 1) % LANES
    from_right = pltpu.roll(x, x.shape[1] - half, 1)
    from_left = pltpu.roll(x, half, 1)
    return jnp.where(lane < MLA_NOPE + half, -from_right, from_left)


def _proj_even_kernel(h_ref, g_ref, wz_ref, wxbc_ref, wdt_ref, wkrp_ref, wcq_ref, qg_ref,
                      wckv_ref, kvg_ref, wqp_ref, wkp_ref, wv_ref, cw_ref, cb_ref,
                      cq_ref, sq_ref, ck_ref, sk_ref,
                      z_ref, xbc_ref, dt_ref, q_ref, k_ref, v_ref, tail_scr):
    @pl.when(pl.program_id(1) == 0)
    def _():
        tail_scr[...] = jnp.zeros_like(tail_scr)

    hn = _rms(h_ref[...], g_ref[...]).astype(BF16)

    def conv_slab(c0):
        cols = slice(c0, c0 + CONV_COLS)
        conv = _causal_conv(_dot(hn, wxbc_ref[:, cols]), tail_scr, cw_ref, cb_ref, cols)
        xbc_ref[:, cols] = (conv * _sigmoid(conv)).astype(BF16)

    def z_slab(c0):
        cols = slice(c0, c0 + Z_COLS)
        z_ref[:, cols] = _dot(hn, wz_ref[:, cols]).astype(BF16)

    cqn = _rms(_dot(hn, wcq_ref[...]), qg_ref[...]).astype(BF16)
    ckvn = _rms(_dot(hn, wckv_ref[...]), kvg_ref[...]).astype(BF16)
    kr_raw = _dot(hn, wkrp_ref[...])
    kr = kr_raw * ck_ref[...] + _rotate_half(kr_raw) * sk_ref[...]
    dt_ref[...] = _dot(hn, wdt_ref[...])
    cq_t = jnp.concatenate([cq_ref[...]] * 2, axis=1)
    sq_t = jnp.concatenate([sq_ref[...]] * 2, axis=1)
    kr2 = jnp.concatenate([kr] * 2, axis=1)

    def head_pair(hd):
        sl = slice(hd * LANES, (hd + 2) * LANES)
        q_raw = _dot(cqn, wqp_ref[:, sl])
        qh = (q_raw * cq_t + _rotate_half(q_raw) * sq_t).astype(BF16)
        kh = (_dot(ckvn, wkp_ref[:, sl]) + kr2).astype(BF16)
        for j in range(2):
            q_ref[hd + j] = qh[:, j * LANES:(j + 1) * LANES]
            k_ref[hd + j] = kh[:, j * LANES:(j + 1) * LANES]

    def v_proj(_):
        v_ref[...] = _dot(ckvn, wv_ref[...]).astype(BF16)

    convs = [(conv_slab, c0) for c0 in range(0, SSD_CONV_CH, CONV_COLS)]
    pairs = [(head_pair, hd) for hd in range(0, MLA_HEADS, 2)]
    heavy = []
    while convs or pairs:
        heavy += convs[:1] + pairs[:1]
        convs, pairs = convs[1:], pairs[1:]
    light = [(v_proj, None)] + [(z_slab, c0) for c0 in range(0, SSD_D_INNER, Z_COLS)]
    order = []
    for i, item in enumerate(heavy):
        order.append(item)
        if (i + 1) % 3 == 0 and len(light) > 1:
            order.append(light.pop(0))
    order += light
    for fn, arg in order:
        fn(arg)


def _proj_even(h3, g, wts, tables):
    b, tp, d = h3.shape
    tm = _pick_tile(tp, (384, 128))
    nt = tp // tm
    n = b * tp
    row3 = lambda bi, i: (bi, i, 0)
    flat = lambda bi, i: (bi * nt + i, 0)
    tab = lambda bi, i: (i, 0)
    in_specs = [pl.BlockSpec((None, tm, d), row3), _const_spec(g.shape)]
    in_specs += [_const_spec(w.shape) for w in wts]
    in_specs += [pl.BlockSpec((tm, LANES), tab) for _ in tables]
    out_shape = (
        jax.ShapeDtypeStruct((n, SSD_D_INNER), BF16),
        jax.ShapeDtypeStruct((n, SSD_CONV_CH), BF16),
        jax.ShapeDtypeStruct((n, LANES), F32),
        jax.ShapeDtypeStruct((MLA_HEADS, n, LANES), BF16),
        jax.ShapeDtypeStruct((MLA_HEADS, n, LANES), BF16),
        jax.ShapeDtypeStruct((n, MLA_HEADS * MLA_V), BF16),
    )
    head3 = lambda bi, i: (0, bi * nt + i, 0)
    out_specs = (
        pl.BlockSpec((tm, SSD_D_INNER), flat),
        pl.BlockSpec((tm, SSD_CONV_CH), flat),
        pl.BlockSpec((tm, LANES), flat),
        pl.BlockSpec((MLA_HEADS, tm, LANES), head3),
        pl.BlockSpec((MLA_HEADS, tm, LANES), head3),
        pl.BlockSpec((tm, MLA_HEADS * MLA_V), flat),
    )
    return pl.pallas_call(
        _proj_even_kernel, grid=(b, nt), in_specs=in_specs, out_specs=out_specs, out_shape=out_shape,
        scratch_shapes=[pltpu.VMEM((SUBLANES, SSD_CONV_CH), F32)],
        compiler_params=_cparams("parallel", "arbitrary"), name="proj_even",
    )(h3, g, *wts, *tables)


def _cumsum_rows(x):
    n = x.shape[0]
    row = lax.broadcasted_iota(jnp.int32, x.shape, 0)
    d = 1
    while d < n:
        x = x + jnp.where(row >= d, pltpu.roll(x, d, 0), 0.0)
        d *= 2
    return x


SSD_BATCH_BLOCK = 4


def _ssd_chunk(c, xbc, dt_raw, zf, dtb, alog, dsk, ng, state_ref, y_scr):
    L = CHUNK
    heads_per_group = SSD_HEADS // SSD_GROUPS
    row = lax.broadcasted_iota(jnp.int32, (L, 1), 0)
    valid = (c * L + row) >= PAD
    act = jnp.where(valid, xbc.astype(F32), 0.0)
    xs = act[:, :SSD_D_INNER]
    bm = act[:, SSD_D_INNER:SSD_D_INNER + SSD_BC]
    cm = act[:, SSD_D_INNER + SSD_BC:]

    dtv = jnp.where(valid, _softplus(dt_raw + dtb), 0.0)
    a_cs = _cumsum_rows(dtv * (-jnp.exp(alog)))
    a_cs_t = a_cs.T
    dt_t = dtv.T
    tot = a_cs[L - 1:L, :]
    w_t = jnp.exp(a_cs_t[:, L - 1:L] - a_cs_t) * dt_t
    cd = jnp.exp(tot)
    causal = lax.broadcasted_iota(jnp.int32, (L, L), 0) >= lax.broadcasted_iota(jnp.int32, (L, L), 1)
    lo = lax.broadcasted_iota(jnp.int32, (L, LANES), 1) < SSD_HEAD_DIM

    for g in range(SSD_GROUPS):
        cc = cm[:, g * SSD_STATE:(g + 1) * SSD_STATE]
        bc = bm[:, g * SSD_STATE:(g + 1) * SSD_STATE]
        bc_t = bc.T
        cb = _dot_nt(cc.astype(BF16), bc.astype(BF16))
        for hp in range(heads_per_group // 2):
            pair = g * (heads_per_group // 2) + hp
            sl = slice(pair * LANES, (pair + 1) * LANES)
            xs_pair = xs[:, sl]
            st_pair = state_ref[:, sl]
            xs_b = xs_pair.astype(BF16)
            rhs = jnp.concatenate([xs_b, st_pair.astype(BF16)], axis=0)
            ys, sts, cds = [], [], []
            for hh in range(2):
                h = 2 * pair + hh
                a_col = jnp.broadcast_to(a_cs[:, h:h + 1], (L, L))
                dec = jnp.exp(jnp.where(causal, a_col - a_cs_t[h:h + 1, :], -jnp.inf))
                m = cb * dec * dt_t[h:h + 1, :]
                lhs = jnp.concatenate([m, cc * jnp.exp(a_col)], axis=1).astype(BF16)
                ys.append(_dot(lhs, rhs))
                sts.append(_dot((bc_t * w_t[h:h + 1, :]).astype(BF16), xs_b))
                cds.append(cd[:, h:h + 1])
            y_scr[:, sl] = jnp.where(lo, ys[0], ys[1]) + dsk[:, sl] * xs_pair
            state_ref[:, sl] = jnp.where(lo, cds[0], cds[1]) * st_pair + jnp.where(lo, sts[0], sts[1])

    return _rms(y_scr[...] * (zf * _sigmoid(zf)), ng).astype(BF16)


def _ssd_kernel(xbc_ref, dt_ref, z_ref, dtb_ref, alog_ref, dsk_ref, ng_ref, y_ref, state_scr, y_scr):
    c = pl.program_id(1)

    @pl.when(c == 0)
    def _():
        state_scr[...] = jnp.zeros_like(state_scr)

    for bb in range(xbc_ref.shape[0]):
        y_ref[bb] = _ssd_chunk(c, xbc_ref[bb], dt_ref[bb], z_ref[bb].astype(F32), dtb_ref[...], alog_ref[...],
                               dsk_ref[...], ng_ref[...], state_scr.at[bb], y_scr.at[bb])


def _ssd(xbc3, dt3, z3, dtb, alog, dsk, ng):
    b, tp, _ = xbc3.shape
    nc = tp // CHUNK
    gb = _pick_tile(b, (SSD_BATCH_BLOCK, 1))
    row3 = lambda bi, ci: (bi, ci, 0)
    in_specs = [
        pl.BlockSpec((gb, CHUNK, SSD_CONV_CH), row3),
        pl.BlockSpec((gb, CHUNK, LANES), row3),
        pl.BlockSpec((gb, CHUNK, SSD_D_INNER), row3),
    ] + [_const_spec(a.shape) for a in (dtb, alog, dsk, ng)]
    return pl.pallas_call(
        _ssd_kernel, grid=(b // gb, nc), in_specs=in_specs,
        out_specs=pl.BlockSpec((gb, CHUNK, SSD_D_INNER), row3),
        out_shape=jax.ShapeDtypeStruct((b, tp, SSD_D_INNER), BF16),
        scratch_shapes=[pltpu.VMEM((gb, SSD_STATE, SSD_D_INNER), F32),
                        pltpu.VMEM((gb, CHUNK, SSD_D_INNER), F32)],
        compiler_params=_cparams("parallel", "arbitrary"), name="ssd",
    )(xbc3, dt3, z3, dtb, alog, dsk, ng)


ATTN_KV_BLOCK = 256
ATTN_V_ROWS = MLA_V + 16


def _attn_kernel(q_ref, k_ref, v_ref, o_ref, vt_scr, s_scr, bmax_scr, p_scr, alpha_scr, m_scr, acc_scr, *, tq):
    kb = ATTN_KV_BLOCK
    qi = pl.program_id(2)
    q0 = qi * tq
    end = q0 + tq
    n_int = q0 // kb
    n_blocks = (end + kb - 1) // kb
    last_off = end - kb

    @pl.when(qi == 0)
    def _():
        ones = jnp.ones((ATTN_V_ROWS - MLA_V, CHUNK), BF16)
        for c in range(vt_scr.shape[0]):
            vt = v_ref[c * CHUNK:(c + 1) * CHUNK, :].astype(F32).T.astype(BF16)
            for hh in range(2):
                vt_scr[c, hh, :MLA_V, :] = vt[hh * MLA_V:(hh + 1) * MLA_V, :]
                vt_scr[c, hh, MLA_V:, :] = ones

    m_scr[...] = jnp.full_like(m_scr, NEG_BIG)
    acc_scr[...] = jnp.zeros_like(acc_scr)

    def offset(bk):
        return pl.multiple_of(jnp.minimum(bk * kb, last_off), CHUNK)

    levels = tuple(range(0, tq - kb + 1, kb))

    def level(bk):
        return jnp.minimum(jnp.maximum(offset(bk) - q0, 0) // kb, len(levels) - 1) * kb

    def qk(hh, bk, masked, cs):
        off = offset(bk)
        s = _dot_nt(k_ref[hh, pl.ds(off, kb), :], q_ref[hh, cs:, :])
        if masked:
            key = lax.broadcasted_iota(jnp.int32, s.shape, 0)
            keep = (key - lax.broadcasted_iota(jnp.int32, s.shape, 1)) <= (q0 + cs - off)
            first_key = jnp.where(bk == 0, PAD, bk * kb - off)
            s = jnp.where(keep, s, NEG_BIG)
            s = jnp.where(key >= first_key, s, NEG_BIG)
        s_scr[hh, :, cs:] = s
        bmax_scr[hh, :, cs:] = jnp.broadcast_to(jnp.max(s, axis=0, keepdims=True), (SUBLANES, tq - cs))

    def softmax(hh, cs):
        m_prev = m_scr[hh, 0:1, cs:]
        m_new = jnp.maximum(m_prev, bmax_scr[hh, 0:1, cs:])
        alpha_scr[hh, :, cs:] = jnp.broadcast_to(jnp.exp2(m_prev - m_new), (SUBLANES, tq - cs))
        m_scr[hh, :, cs:] = jnp.broadcast_to(m_new, (SUBLANES, tq - cs))
        p_scr[hh, :, cs:] = jnp.exp2(s_scr[hh, :, cs:] - m_new).astype(BF16)

    def pv(hh, bk, cs):
        c0 = offset(bk) // CHUNK
        vt = jnp.concatenate([vt_scr[c0 + c, hh] for c in range(kb // CHUNK)], axis=1)
        acc_scr[hh, :, cs:] = alpha_scr[hh, 0:1, cs:] * acc_scr[hh, :, cs:] + _dot(vt, p_scr[hh, :, cs:])

    def step(bk, masked, cs_prev, cs):
        softmax(1, cs_prev)
        qk(0, bk, masked, cs)
        pv(0, bk - 1, cs_prev)
        softmax(0, cs)
        qk(1, bk, masked, cs)
        pv(1, bk - 1, cs_prev)

    def plain_step(bk, carry):
        step(bk, False, 0, 0)
        return carry

    def masked_step(bk, carry):
        lvl_prev, lvl = level(bk - 1), level(bk)
        for a in levels:
            for b in (a, a + kb):
                if b in levels:
                    @pl.when(jnp.logical_and(lvl_prev == a, lvl == b))
                    def _():
                        step(bk, True, a, b)
        return carry

    qk(0, 0, True, 0)
    softmax(0, 0)
    qk(1, 0, True, 0)
    lax.fori_loop(1, n_int, plain_step, 0)
    lax.fori_loop(jnp.maximum(n_int, 1), n_blocks, masked_step, 0)
    lvl_last = level(n_blocks - 1)
    for a in levels:
        @pl.when(lvl_last == a)
        def _():
            softmax(1, a)
            pv(0, n_blocks - 1, a)
            pv(1, n_blocks - 1, a)

    out_t = jnp.concatenate([acc_scr[hh, :MLA_V, :] / acc_scr[hh, MLA_V:MLA_V + 1, :] for hh in range(2)],
                            axis=0)
    qpos = q0 + lax.broadcasted_iota(jnp.int32, (1, tq), 1)
    out_t = jnp.where(qpos >= PAD, out_t, 0.0)
    for c in range(tq // CHUNK):
        o_ref[c * CHUNK:(c + 1) * CHUNK, :] = out_t[:, c * CHUNK:(c + 1) * CHUNK].T.astype(BF16)


def _attention(q, k, v, b, tp):
    tq = _pick_tile(tp, (1408, 384, 256))
    nq = tp // tq
    n = b * tp
    in_specs = [
        pl.BlockSpec((2, tq, LANES), lambda bi, hp, qi: (hp, bi * nq + qi, 0)),
        pl.BlockSpec((2, tp, LANES), lambda bi, hp, qi: (hp, bi, 0)),
        pl.BlockSpec((tp, LANES), lambda bi, hp, qi: (bi, hp)),
    ]
    return pl.pallas_call(
        functools.partial(_attn_kernel, tq=tq), grid=(b, MLA_HEADS // 2, nq), in_specs=in_specs,
        out_specs=pl.BlockSpec((tq, LANES), lambda bi, hp, qi: (bi * nq + qi, hp)),
        out_shape=jax.ShapeDtypeStruct((n, MLA_HEADS * MLA_V), BF16),
        scratch_shapes=[pltpu.VMEM((tp // CHUNK, 2, ATTN_V_ROWS, CHUNK), BF16),
                        pltpu.VMEM((2, ATTN_KV_BLOCK, tq), F32), pltpu.VMEM((2, SUBLANES, tq), F32),
                        pltpu.VMEM((2, ATTN_KV_BLOCK, tq), BF16),
                        pltpu.VMEM((2, SUBLANES, tq), F32), pltpu.VMEM((2, SUBLANES, tq), F32),
                        pltpu.VMEM((2, ATTN_V_ROWS, tq), F32)],
        compiler_params=_cparams("parallel", "parallel", "arbitrary"), name="attention",
    )(q, k, v)


def _post_kernel(*refs, n_in):
    h_ref = refs[0]
    a_refs = refs[1:1 + n_in]
    w_refs = refs[1 + n_in:1 + 2 * n_in]
    g_ref = refs[1 + 2 * n_in]
    o_ref = refs[2 + 2 * n_in]
    m = _dot(a_refs[0][...], w_refs[0][...])
    for a_ref, w_ref in zip(a_refs[1:], w_refs[1:]):
        m = m + _dot(a_ref[...], w_ref[...])
    o_ref[...] = h_ref[...] + _rms(m, g_ref[...])


def _post(h2, acts, wts, g):
    n, d = h2.shape
    tm = _pick_tile(n, (1024, 512, 384, 256, 128))
    rows = lambda i: (i, 0)
    in_specs = [pl.BlockSpec((tm, d), rows)]
    in_specs += [pl.BlockSpec((tm, a.shape[1]), rows) for a in acts]
    in_specs += [_const_spec(w.shape) for w in wts] + [_const_spec(g.shape)]
    return pl.pallas_call(
        functools.partial(_post_kernel, n_in=len(acts)), grid=(n // tm,), in_specs=in_specs,
        out_specs=pl.BlockSpec((tm, d), rows), out_shape=jax.ShapeDtypeStruct((n, d), F32),
        compiler_params=_cparams("parallel"), name="mixer_out",
    )(h2, *acts, *wts, g)


def _mlp_kernel(h_ref, g1_ref, wup_ref, wdn_ref, g2_ref, o_ref, *, ff_chunk):
    x = h_ref[...]
    hn = _rms(x, g1_ref[...]).astype(BF16)
    d_ff = wup_ref.shape[1]
    y = None
    for c0 in range(0, d_ff, ff_chunk):
        u = jnp.maximum(_dot(hn, wup_ref[:, c0:c0 + ff_chunk]), 0.0)
        part = _dot((u * u).astype(BF16), wdn_ref[c0:c0 + ff_chunk, :])
        y = part if y is None else y + part
    o_ref[...] = x + _rms(y, g2_ref[...])


def _mlp(h2, g1, wup, wdn, g2):
    n, d = h2.shape
    tm = _pick_tile(n, (512, 256, 128))
    rows = lambda i: (i, 0)
    in_specs = [pl.BlockSpec((tm, d), rows), _const_spec(g1.shape), _const_spec(wup.shape),
                _const_spec(wdn.shape), _const_spec(g2.shape)]
    return pl.pallas_call(
        functools.partial(_mlp_kernel, ff_chunk=1024), grid=(n // tm,), in_specs=in_specs,
        out_specs=pl.BlockSpec((tm, d), rows), out_shape=jax.ShapeDtypeStruct((n, d), F32),
        compiler_params=_cparams("parallel"), name="mlp",
    )(h2, g1, wup, wdn, g2)


def _mlp_last(h3, g1, wup, wdn, g2, lead):
    b, tp, d = h3.shape
    seq = tp - lead
    tm = _pick_tile(seq, (512, 256, 128))
    in_specs = [pl.BlockSpec((pl.Element(tm), pl.Element(d)), lambda bi, i: (pl.multiple_of(bi * tp + lead + i * tm, CHUNK), 0)),
                _const_spec(g1.shape), _const_spec(wup.shape), _const_spec(wdn.shape), _const_spec(g2.shape)]
    return pl.pallas_call(
        functools.partial(_mlp_kernel, ff_chunk=1024), grid=(b, seq // tm), in_specs=in_specs,
        out_specs=pl.BlockSpec((None, tm, d), lambda bi, i: (bi, i, 0)),
        out_shape=jax.ShapeDtypeStruct((b, seq, d), F32),
        compiler_params=_cparams("parallel", "parallel"), name="mlp_last",
    )(h3.reshape(b * tp, d), g1, wup, wdn, g2)


SEG_LEN = CHUNK // SUBLANES


def _interleave_matrix():
    p = np.zeros((CHUNK, CHUNK), np.float32)
    for j in range(SEG_LEN):
        for s in range(SUBLANES):
            p[SUBLANES * j + s, SEG_LEN * s + j] = 1.0
    return p


def _reorder_rows(perm, x):
    tiles = [_dot(perm, x[t0:t0 + CHUNK]).astype(BF16) for t0 in range(0, x.shape[0], CHUNK)]
    return tiles[0] if len(tiles) == 1 else jnp.concatenate(tiles, axis=0)


def _causal_conv_interleaved(pre, tail_scr, w_ref, b_ref, cols):
    k = w_ref.shape[0]
    halo = (k - 1) * SUBLANES
    width = pre.shape[1]
    first_sub = lax.broadcasted_iota(jnp.int32, (k - 1, SUBLANES, width), 1) == 0
    prev_last = tail_scr[:, cols]
    outs = []
    for t0 in range(0, pre.shape[0], CHUNK):
        tile = pre[t0:t0 + CHUNK]
        last = tile[CHUNK - halo:]
        fix = jnp.where(first_sub, pltpu.roll(prev_last.reshape(k - 1, SUBLANES, width), 1, 1),
                        pltpu.roll(last.reshape(k - 1, SUBLANES, width), 1, 1)).reshape(halo, width)
        out = b_ref[:, cols]
        for i in range(k):
            shift = k - 1 - i
            if shift == 0:
                xs = tile
            else:
                xs = jnp.concatenate([fix[halo - shift * SUBLANES:], tile[:CHUNK - shift * SUBLANES]], axis=0)
            out = out + xs * w_ref[i:i + 1, cols]
        outs.append(out)
        prev_last = last
    tail_scr[:, cols] = prev_last
    return jnp.concatenate(outs, axis=0)


def _proj_odd_kernel(h_ref, g_ref, wx_ref, wy_ref, cw_ref, cb_ref, perm_ref, x_ref, gate_ref, tail_scr):
    @pl.when(pl.program_id(0) == 0)
    def _():
        tail_scr[...] = jnp.zeros_like(tail_scr)

    hn = _reorder_rows(perm_ref[...], _rms(h_ref[...], g_ref[...]).astype(BF16))
    for c0 in range(0, LRU_WIDTH, LRU_CONV_COLS):
        cols = slice(c0, c0 + LRU_CONV_COLS)
        x_ref[:, cols] = _causal_conv_interleaved(_dot(hn, wx_ref[:, cols]), tail_scr, cw_ref, cb_ref,
                                                  cols).astype(BF16)
        y = _dot(hn, wy_ref[:, cols])
        gelu = 0.5 * y * (1.0 + jnp.tanh(np.sqrt(2.0 / np.pi).astype(np.float32) * (y + 0.044715 * (y * y * y))))
        gate_ref[:, cols] = gelu.astype(BF16)


def _proj_odd(h2, g, wx, wy, cw, cb, perm):
    n, d = h2.shape
    tm = _pick_tile(n, (1024, 512, 384, 256, 128))
    rows = lambda i: (i, 0)
    w = wx.shape[1]
    return pl.pallas_call(
        _proj_odd_kernel, grid=(n // tm,),
        in_specs=[pl.BlockSpec((tm, d), rows)] + [_const_spec(a.shape) for a in (g, wx, wy, cw, cb, perm)],
        out_specs=(pl.BlockSpec((tm, w), rows), pl.BlockSpec((tm, w), rows)),
        out_shape=(jax.ShapeDtypeStruct((n, w), BF16), jax.ShapeDtypeStruct((n, w), BF16)),
        scratch_shapes=[pltpu.VMEM(((cw.shape[0] - 1) * SUBLANES, w), F32)],
        compiler_params=_cparams("arbitrary"), name="proj_odd",
    )(h2, g, wx, wy, cw, cb, perm)


LRU_BATCH_BLOCK = 4


def _rglru_kernel(x_ref, gate_ref, wa_ref, ba_ref, wi_ref, bi_ref, lam_ref, unperm_ref, o_ref, h_scr):
    t = pl.program_id(1)

    @pl.when(t == 0)
    def _():
        h_scr[...] = jnp.zeros_like(h_scr)

    for bb in range(x_ref.shape[0]):
        _rglru_tile(t, x_ref.at[bb], gate_ref.at[bb], wa_ref, ba_ref, wi_ref, bi_ref, lam_ref, unperm_ref,
                    o_ref.at[bb], h_scr.at[bb])


def _rglru_tile(t, x_ref, gate_ref, wa_ref, ba_ref, wi_ref, bi_ref, lam_ref, unperm_ref, o_ref, h_scr):
    L = x_ref.shape[0]
    xr = x_ref[...].astype(F32)
    rs, gs = [], []
    for nb in range(LRU_BLOCKS):
        sl = slice(nb * LRU_BLOCK, (nb + 1) * LRU_BLOCK)
        xb = x_ref[:, sl]
        rs.append(_dot(xb, wa_ref[nb]))
        gs.append(_dot(xb, wi_ref[nb]))
    r = _sigmoid(jnp.concatenate(rs, axis=1) + ba_ref[...])
    ig = _sigmoid(jnp.concatenate(gs, axis=1) + bi_ref[...])
    log_a = (-LRU_C) * r * _softplus(-lam_ref[...])
    a = jnp.exp(log_a)
    row = lax.broadcasted_iota(jnp.int32, (L, 1), 0)
    valid = (t * L + SEG_LEN * (row % SUBLANES) + row // SUBLANES) >= PAD
    om = 1.0 - a * a
    root = jnp.where(om > 0.0, om * lax.rsqrt(om), 0.0)
    u = jnp.where(valid, root * (ig * xr), 0.0)

    slab = lambda v, j: v[j * SUBLANES:(j + 1) * SUBLANES, :]
    h_loc = [slab(u, 0)]
    prod = [slab(a, 0)]
    for j in range(1, SEG_LEN):
        h_loc.append(slab(a, j) * h_loc[-1] + slab(u, j))
        prod.append(slab(a, j) * prod[-1])
    carry_in = h_scr[0:1, :]
    carries = []
    for s in range(SUBLANES):
        carries.append(carry_in)
        carry_in = h_loc[-1][s:s + 1, :] + prod[-1][s:s + 1, :] * carry_in
    h_scr[...] = jnp.broadcast_to(carry_in, h_scr.shape)
    carry = jnp.concatenate(carries, axis=0)
    hs = jnp.concatenate([h_loc[j] + prod[j] * carry for j in range(SEG_LEN)], axis=0)
    o_ref[...] = _reorder_rows(unperm_ref[...], (hs * gate_ref[...].astype(F32)).astype(BF16))


def _rglru(x3, gate3, wa, ba, wi, bi, lam, unperm):
    b, tp, w = x3.shape
    tt = CHUNK
    gb = _pick_tile(b, (LRU_BATCH_BLOCK, 1))
    row3 = lambda bi_, ti: (bi_, ti, 0)
    in_specs = [pl.BlockSpec((gb, tt, w), row3), pl.BlockSpec((gb, tt, w), row3)]
    in_specs += [_const_spec(a.shape) for a in (wa, ba, wi, bi, lam, unperm)]
    return pl.pallas_call(
        _rglru_kernel, grid=(b // gb, tp // tt), in_specs=in_specs,
        out_specs=pl.BlockSpec((gb, tt, w), row3),
        out_shape=jax.ShapeDtypeStruct((b, tp, w), BF16),
        scratch_shapes=[pltpu.VMEM((gb, SUBLANES, w), F32)],
        compiler_params=_cparams("parallel", "arbitrary"), name="rglru",
    )(x3, gate3, wa, ba, wi, bi, lam, unperm)


def _rope_tables(tp):
    f32 = np.float32
    inv = f32(ROPE_BASE) ** (-np.arange(0, MLA_ROPE, 2, dtype=f32) / f32(MLA_ROPE))
    pos = np.maximum(np.arange(tp, dtype=f32) - f32(PAD), f32(0))
    ang = (pos[:, None] * inv[None, :]).astype(f32)
    cos, sin = np.cos(ang).astype(f32), np.sin(ang).astype(f32)
    zeros_lo = np.zeros((tp, MLA_NOPE), f32)
    zeros_hi = np.zeros((tp, LANES - MLA_NOPE - MLA_ROPE), f32)
    ck = np.concatenate([zeros_lo, cos, cos, zeros_hi], axis=1)
    sk = np.concatenate([zeros_lo, sin, sin, zeros_hi], axis=1)
    scale = f32((MLA_NOPE + MLA_ROPE) ** -0.5 * np.log2(np.e))
    cq = np.concatenate([np.ones((tp, MLA_NOPE), f32), cos, cos, zeros_hi], axis=1) * scale
    sq = sk * scale
    return tuple(jnp.asarray(t) for t in (cq, sq, ck, sk))


def _even_weights(w_in, w_q_up, w_kv_up):
    offs = np.cumsum((SSD_D_INNER, SSD_CONV_CH, SSD_HEADS, MLA_Q_RANK, MLA_KV_RANK, MLA_ROPE))
    wz = w_in[:, :offs[0]]
    wxbc = w_in[:, offs[0]:offs[1]]
    wdt = w_in[:, offs[1]:offs[2]]
    wcq = w_in[:, offs[2]:offs[3]]
    wckv = w_in[:, offs[3]:offs[4]]
    wkr = w_in[:, offs[4]:offs[5]]
    hi_pad = LANES - MLA_NOPE - MLA_ROPE
    wdt_p = jnp.pad(wdt, ((0, 0), (0, LANES - SSD_HEADS)))
    wkr_p = jnp.pad(wkr, ((0, 0), (MLA_NOPE, hi_pad)))
    wq = w_q_up.reshape(MLA_Q_RANK, MLA_HEADS, MLA_NOPE + MLA_ROPE)
    wq_p = jnp.pad(wq, ((0, 0), (0, 0), (0, hi_pad))).reshape(MLA_Q_RANK, MLA_HEADS * LANES)
    wkv = w_kv_up.reshape(MLA_KV_RANK, MLA_HEADS, MLA_NOPE + MLA_V)
    wk_p = jnp.pad(wkv[..., :MLA_NOPE], ((0, 0), (0, 0), (0, LANES - MLA_NOPE))).reshape(
        MLA_KV_RANK, MLA_HEADS * LANES)
    wv = wkv[..., MLA_NOPE:].reshape(MLA_KV_RANK, MLA_HEADS * MLA_V)
    bf = lambda a: a.astype(BF16)
    return dict(wz=bf(wz), wxbc=bf(wxbc), wdt=bf(wdt_p), wkr_p=bf(wkr_p), wcq=bf(wcq),
                wckv=bf(wckv), wq_p=bf(wq_p), wk_p=bf(wk_p), wv=bf(wv))


def _row(a):
    return a.reshape(1, -1).astype(F32)


def kernel(x, meta_tokens, mix_pre_g, mix_post_g, mlp_pre_g, mlp_post_g, w_up, w_down, w_in, ssd_conv_w,
           ssd_conv_b, ssd_dt_bias, ssd_a_log, ssd_d, ssd_norm_g, mla_q_norm_g, mla_w_q_up, mla_kv_norm_g,
           mla_w_kv_up, w_out_ab, rg_w_x, rg_w_y, rg_conv_w, rg_conv_b, rg_w_a, rg_b_a, rg_w_i, rg_b_i,
           rg_lambda, rg_w_out):
    b, seq, d = x.shape
    depth = mix_pre_g.shape[0]
    tp = PAD + N_META + seq
    n = b * tp
    meta = jnp.broadcast_to(meta_tokens[None].astype(x.dtype), (b, N_META, d))
    h = jnp.concatenate([jnp.zeros((b, PAD, d), x.dtype), meta, x], axis=1).reshape(n, d)
    tables = _rope_tables(tp)

    for layer in range(depth):
        if layer % 2 == 0:
            e = layer // 2
            w = _even_weights(w_in[e], mla_w_q_up[e], mla_w_kv_up[e])
            wts = (w["wz"], w["wxbc"], w["wdt"], w["wkr_p"], w["wcq"], _row(mla_q_norm_g[e]),
                   w["wckv"], _row(mla_kv_norm_g[e]), w["wq_p"], w["wk_p"], w["wv"],
                   ssd_conv_w[e].astype(F32), _row(ssd_conv_b[e]))
            z, xbc, dt, q, k, v = _proj_even(h.reshape(b, tp, d), _row(mix_pre_g[layer]), wts, tables)
            lane_pad = lambda a: jnp.pad(_row(a), ((0, 0), (0, LANES - SSD_HEADS)))
            y_ssd = _ssd(xbc.reshape(b, tp, SSD_CONV_CH), dt.reshape(b, tp, LANES), z.reshape(b, tp, SSD_D_INNER),
                         lane_pad(ssd_dt_bias[e]), lane_pad(ssd_a_log[e]),
                         _row(jnp.repeat(ssd_d[e], SSD_HEAD_DIM)), _row(ssd_norm_g[e]))
            y_att = _attention(q, k, v, b, tp)
            wo = w_out_ab[e].astype(BF16)
            h = _post(h, (y_ssd.reshape(n, SSD_D_INNER), y_att), (wo[:SSD_D_INNER], wo[SSD_D_INNER:]),
                      _row(mix_post_g[layer]))
        else:
            o = layer // 2
            perm = _interleave_matrix()
            xr, gate = _proj_odd(h, _row(mix_pre_g[layer]), rg_w_x[o].astype(BF16), rg_w_y[o].astype(BF16),
                                 rg_conv_w[o].astype(F32), _row(rg_conv_b[o]), jnp.asarray(perm, BF16))
            y = _rglru(xr.reshape(b, tp, LRU_WIDTH), gate.reshape(b, tp, LRU_WIDTH), rg_w_a[o].astype(BF16),
                       _row(rg_b_a[o]), rg_w_i[o].astype(BF16), _row(rg_b_i[o]), _row(rg_lambda[o]),
                       jnp.asarray(perm.T, BF16))
            h = _post(h, (y.reshape(n, LRU_WIDTH),), (rg_w_out[o].astype(BF16),), _row(mix_post_g[layer]))
        mlp_args = (_row(mlp_pre_g[layer]), w_up[layer].astype(BF16), w_down[layer].astype(BF16),
                    _row(mlp_post_g[layer]))
        if layer == depth - 1:
            return _mlp_last(h.reshape(b, tp, d), *mlp_args, PAD + N_META)
        h = _mlp(h, *mlp_args)
```

```python
import functools

import jax
import jax.numpy as jnp
import numpy as np
from jax import lax
from jax.experimental import pallas as pl
from jax.experimental.pallas import tpu as pltpu

F32 = jnp.float32
BF16 = jnp.bfloat16

N_META = 16
CHUNK = 128
PAD = CHUNK - N_META
EPS = 1e-6
SSD_HEADS = 16
SSD_HEAD_DIM = 64
SSD_D_INNER = SSD_HEADS * SSD_HEAD_DIM
SSD_GROUPS = 2
SSD_STATE = 128
SSD_BC = SSD_GROUPS * SSD_STATE
SSD_CONV_CH = SSD_D_INNER + 2 * SSD_BC
MLA_HEADS = 16
MLA_NOPE = 64
MLA_ROPE = 32
MLA_V = 64
MLA_Q_RANK = 384
MLA_KV_RANK = 256
ROPE_BASE = 10000.0
LRU_BLOCKS = 10
LRU_BLOCK = 128
LRU_WIDTH = LRU_BLOCKS * LRU_BLOCK
LRU_C = 8.0
LRU_CONV_COLS = 256
LANES = 128
SUBLANES = 8
VMEM_LIMIT_BYTES = 56 * 1024 * 1024
NEG_BIG = -1e30


def _cparams(*sem):
    return pltpu.CompilerParams(dimension_semantics=sem, vmem_limit_bytes=VMEM_LIMIT_BYTES)


def _pick_tile(n, candidates):
    for c in candidates:
        if n % c == 0:
            return c
    raise ValueError(f"no tile in {candidates} divides {n}")


def _const_spec(shape):
    nd = len(shape)
    return pl.BlockSpec(shape, lambda *_: (0,) * nd)


def _rms(x, g):
    ms = jnp.mean(x * x, axis=-1, keepdims=True)
    return x * lax.rsqrt(ms + EPS) * g


def _sigmoid(x):
    return 1.0 / (1.0 + jnp.exp(-x))


def _softplus(x):
    return jnp.maximum(x, 0.0) + jnp.log(1.0 + jnp.exp(-jnp.abs(x)))


def _dot(a, b):
    return jnp.dot(a, b, preferred_element_type=F32)


def _dot_nt(a, b):
    return lax.dot_general(a, b, (((1,), (1,)), ((), ())), preferred_element_type=F32)


def _shift_rows(cur, tail, k):
    if k == 0:
        return cur
    rolled = pltpu.roll(cur, k, 0)
    row = lax.broadcasted_iota(jnp.int32, tail.shape, 0)
    top = jnp.where(row < k, pltpu.roll(tail, k, 0), rolled[0:SUBLANES])
    return jnp.concatenate([top, rolled[SUBLANES:]], axis=0)


def _causal_conv(cur, tail_scr, w_ref, b_ref, cols):
    k = w_ref.shape[0]
    tail = tail_scr[:, cols]
    out = b_ref[:, cols]
    for i in range(k):
        out = out + _shift_rows(cur, tail, k - 1 - i) * w_ref[i:i + 1, cols]
    tail_scr[:, cols] = cur[cur.shape[0] - SUBLANES:]
    return out


CONV_COLS = 256
Z_COLS = 256


def _rotate_half(x):
    half = MLA_ROPE // 2
    lane = lax.broadcasted_iota(jnp.int32, x.shape, 1) % LANES
    from_right = pltpu.roll(x, x.shape[1] - half, 1)
    from_left = pltpu.roll(x, half, 1)
    return jnp.where(lane < MLA_NOPE + half, -from_right, from_left)


def _proj_even_kernel(h_ref, g_ref, wz_ref, wxbc_ref, wdt_ref, wkrp_ref, wcq_ref, qg_ref,
                      wckv_ref, kvg_ref, wqp_ref, wkp_ref, wv_ref, cw_ref, cb_ref,
                      cq_ref, sq_ref, ck_ref, sk_ref,
                      z_ref, xbc_ref, dt_ref, q_ref, k_ref, v_ref, tail_scr):
    @pl.when(pl.program_id(1) == 0)
    def _():
        tail_scr[...] = jnp.zeros_like(tail_scr)

    hn = _rms(h_ref[...], g_ref[...]).astype(BF16)

    def conv_slab(c0):
        cols = slice(c0, c0 + CONV_COLS)
        conv = _causal_conv(_dot(hn, wxbc_ref[:, cols]), tail_scr, cw_ref, cb_ref, cols)
        xbc_ref[:, cols] = (conv * _sigmoid(conv)).astype(BF16)

    def z_slab(c0):
        cols = slice(c0, c0 + Z_COLS)
        z_ref[:, cols] = _dot(hn, wz_ref[:, cols]).astype(BF16)

    cqn = _rms(_dot(hn, wcq_ref[...]), qg_ref[...]).astype(BF16)
    ckvn = _rms(_dot(hn, wckv_ref[...]), kvg_ref[...]).astype(BF16)
    kr_raw = _dot(hn, wkrp_ref[...])
    kr = kr_raw * ck_ref[...] + _rotate_half(kr_raw) * sk_ref[...]
    dt_ref[...] = _dot(hn, wdt_ref[...])
    cq_t = jnp.concatenate([cq_ref[...]] * 2, axis=1)
    sq_t = jnp.concatenate([sq_ref[...]] * 2, axis=1)
    kr2 = jnp.concatenate([kr] * 2, axis=1)

    def head_pair(hd):
        sl = slice(hd * LANES, (hd + 2) * LANES)
        q_raw = _dot(cqn, wqp_ref[:, sl])
        qh = (q_raw * cq_t + _rotate_half(q_raw) * sq_t).astype(BF16)
        kh = (_dot(ckvn, wkp_ref[:, sl]) + kr2).astype(BF16)
        for j in range(2):
            q_ref[hd + j] = qh[:, j * LANES:(j + 1) * LANES]
            k_ref[hd + j] = kh[:, j * LANES:(j + 1) * LANES]

    def v_proj(_):
        v_ref[...] = _dot(ckvn, wv_ref[...]).astype(BF16)

    convs = [(conv_slab, c0) for c0 in range(0, SSD_CONV_CH, CONV_COLS)]
    pairs = [(head_pair, hd) for hd in range(0, MLA_HEADS, 2)]
    heavy = []
    while convs or pairs:
        heavy += convs[:1] + pairs[:1]
        convs, pairs = convs[1:], pairs[1:]
    light = [(v_proj, None)] + [(z_slab, c0) for c0 in range(0, SSD_D_INNER, Z_COLS)]
    order = []
    for i, item in enumerate(heavy):
        order.append(item)
        if (i + 1) % 3 == 0 and len(light) > 1:
            order.append(light.pop(0))
    order += light
    for fn, arg in order:
        fn(arg)


def _proj_even(h3, g, wts, tables):
    b, tp, d = h3.shape
    tm = _pick_tile(tp, (384, 128))
    nt = tp // tm
    n = b * tp
    row3 = lambda bi, i: (bi, i, 0)
    flat = lambda bi, i: (bi * nt + i, 0)
    tab = lambda bi, i: (i, 0)
    in_specs = [pl.BlockSpec((None, tm, d), row3), _const_spec(g.shape)]
    in_specs += [_const_spec(w.shape) for w in wts]
    in_specs += [pl.BlockSpec((tm, LANES), tab) for _ in tables]
    out_shape = (
        jax.ShapeDtypeStruct((n, SSD_D_INNER), BF16),
        jax.ShapeDtypeStruct((n, SSD_CONV_CH), BF16),
        jax.ShapeDtypeStruct((n, LANES), F32),
        jax.ShapeDtypeStruct((MLA_HEADS, n, LANES), BF16),
        jax.ShapeDtypeStruct((MLA_HEADS, n, LANES), BF16),
        jax.ShapeDtypeStruct((n, MLA_HEADS * MLA_V), BF16),
    )
    head3 = lambda bi, i: (0, bi * nt + i, 0)
    out_specs = (
        pl.BlockSpec((tm, SSD_D_INNER), flat),
        pl.BlockSpec((tm, SSD_CONV_CH), flat),
        pl.BlockSpec((tm, LANES), flat),
        pl.BlockSpec((MLA_HEADS, tm, LANES), head3),
        pl.BlockSpec((MLA_HEADS, tm, LANES), head3),
        pl.BlockSpec((tm, MLA_HEADS * MLA_V), flat),
    )
    return pl.pallas_call(
        _proj_even_kernel, grid=(b, nt), in_specs=in_specs, out_specs=out_specs, out_shape=out_shape,
        scratch_shapes=[pltpu.VMEM((SUBLANES, SSD_CONV_CH), F32)],
        compiler_params=_cparams("parallel", "arbitrary"), name="proj_even",
    )(h3, g, *wts, *tables)


def _cumsum_rows(x):
    n = x.shape[0]
    row = lax.broadcasted_iota(jnp.int32, x.shape, 0)
    d = 1
    while d < n:
        x = x + jnp.where(row >= d, pltpu.roll(x, d, 0), 0.0)
        d *= 2
    return x


SSD_BATCH_BLOCK = 4


def _ssd_chunk(c, xbc, dt_raw, zf, dtb, alog, dsk, ng, state_ref, y_scr):
    L = CHUNK
    heads_per_group = SSD_HEADS // SSD_GROUPS
    row = lax.broadcasted_iota(jnp.int32, (L, 1), 0)
    valid = (c * L + row) >= PAD
    act = xbc.astype(F32)
    xs = act[:, :SSD_D_INNER]
    bm = act[:, SSD_D_INNER:SSD_D_INNER + SSD_BC]
    cm = act[:, SSD_D_INNER + SSD_BC:]

    dtv = jnp.where(valid, _softplus(dt_raw + dtb), 0.0)
    a_cs = _cumsum_rows(dtv * (-jnp.exp(alog)))
    a_cs_t = a_cs.T
    dt_t = dtv.T
    tot = a_cs[L - 1:L, :]
    w_t = jnp.exp(a_cs_t[:, L - 1:L] - a_cs_t) * dt_t
    shifted_t = a_cs_t - jnp.log(dt_t)
    cd = jnp.exp(tot)
    causal = lax.broadcasted_iota(jnp.int32, (L, L), 0) >= lax.broadcasted_iota(jnp.int32, (L, L), 1)
    lo = lax.broadcasted_iota(jnp.int32, (L, LANES), 1) < SSD_HEAD_DIM

    for g in range(SSD_GROUPS):
        cc = cm[:, g * SSD_STATE:(g + 1) * SSD_STATE]
        bc = bm[:, g * SSD_STATE:(g + 1) * SSD_STATE]
        bc_t = bc.T
        cb = _dot_nt(cc.astype(BF16), bc.astype(BF16))
        for hp in range(heads_per_group // 2):
            pair = g * (heads_per_group // 2) + hp
            sl = slice(pair * LANES, (pair + 1) * LANES)
            xs_pair = xs[:, sl]
            st_pair = state_ref[:, sl]
            xs_b = xs_pair.astype(BF16)
            rhs = jnp.concatenate([xs_b, st_pair.astype(BF16)], axis=0)
            ys, sts, cds = [], [], []
            for hh in range(2):
                h = 2 * pair + hh
                a_col = jnp.broadcast_to(a_cs[:, h:h + 1], (L, L))
                m = cb * jnp.exp(jnp.where(causal, a_col - shifted_t[h:h + 1, :], -jnp.inf))
                lhs = jnp.concatenate([m, cc * jnp.exp(a_col)], axis=1).astype(BF16)
                ys.append(_dot(lhs, rhs))
                sts.append(_dot((bc_t * w_t[h:h + 1, :]).astype(BF16), xs_b))
                cds.append(cd[:, h:h + 1])
            y_scr[:, sl] = jnp.where(lo, ys[0], ys[1]) + dsk[:, sl] * xs_pair
            state_ref[:, sl] = jnp.where(lo, cds[0], cds[1]) * st_pair + jnp.where(lo, sts[0], sts[1])

    return _rms(y_scr[...] * (zf * _sigmoid(zf)), ng).astype(BF16)


def _ssd_kernel(xbc_ref, dt_ref, z_ref, dtb_ref, alog_ref, dsk_ref, ng_ref, y_ref, state_scr, y_scr):
    c = pl.program_id(1)

    @pl.when(c == 0)
    def _():
        state_scr[...] = jnp.zeros_like(state_scr)

    for bb in range(xbc_ref.shape[0]):
        y_ref[bb] = _ssd_chunk(c, xbc_ref[bb], dt_ref[bb], z_ref[bb].astype(F32), dtb_ref[...], alog_ref[...],
                               dsk_ref[...], ng_ref[...], state_scr.at[bb], y_scr.at[bb])


def _ssd(xbc3, dt3, z3, dtb, alog, dsk, ng):
    b, tp, _ = xbc3.shape
    nc = tp // CHUNK
    gb = _pick_tile(b, (SSD_BATCH_BLOCK, 1))
    row3 = lambda bi, ci: (bi, ci, 0)
    in_specs = [
        pl.BlockSpec((gb, CHUNK, SSD_CONV_CH), row3),
        pl.BlockSpec((gb, CHUNK, LANES), row3),
        pl.BlockSpec((gb, CHUNK, SSD_D_INNER), row3),
    ] + [_const_spec(a.shape) for a in (dtb, alog, dsk, ng)]
    return pl.pallas_call(
        _ssd_kernel, grid=(b // gb, nc), in_specs=in_specs,
        out_specs=pl.BlockSpec((gb, CHUNK, SSD_D_INNER), row3),
        out_shape=jax.ShapeDtypeStruct((b, tp, SSD_D_INNER), BF16),
        scratch_shapes=[pltpu.VMEM((gb, SSD_STATE, SSD_D_INNER), F32),
                        pltpu.VMEM((gb, CHUNK, SSD_D_INNER), F32)],
        compiler_params=_cparams("parallel", "arbitrary"), name="ssd",
    )(xbc3, dt3, z3, dtb, alog, dsk, ng)


ATTN_KV_BLOCK = 256
ATTN_V_ROWS = MLA_V + 16


def _attn_kernel(q_ref, k_ref, v_ref, o_ref, vt_scr, s_scr, bmax_scr, p_scr, alpha_scr, m_scr, acc_scr, *, tq):
    kb = ATTN_KV_BLOCK
    qi = pl.program_id(2)
    q0 = qi * tq
    end = q0 + tq
    n_int = q0 // kb
    n_blocks = (end + kb - 1) // kb
    last_off = end - kb

    @pl.when(qi == 0)
    def _():
        ones = jnp.ones((ATTN_V_ROWS - MLA_V, CHUNK), BF16)
        for c in range(vt_scr.shape[0]):
            vt = v_ref[c * CHUNK:(c + 1) * CHUNK, :].astype(F32).T.astype(BF16)
            for hh in range(2):
                vt_scr[c, hh, :MLA_V, :] = vt[hh * MLA_V:(hh + 1) * MLA_V, :]
                vt_scr[c, hh, MLA_V:, :] = ones

    m_scr[...] = jnp.full_like(m_scr, NEG_BIG)
    acc_scr[...] = jnp.zeros_like(acc_scr)

    def offset(bk):
        return pl.multiple_of(jnp.minimum(bk * kb, last_off), CHUNK)

    levels = tuple(range(0, tq - kb + 1, kb))

    def level(bk):
        return jnp.minimum(jnp.maximum(offset(bk) - q0, 0) // kb, len(levels) - 1) * kb

    def qk(hh, bk, masked, cs):
        off = offset(bk)
        s = _dot_nt(k_ref[hh, pl.ds(off, kb), :], q_ref[hh, cs:, :])
        if masked:
            key = lax.broadcasted_iota(jnp.int32, s.shape, 0)
            keep = (key - lax.broadcasted_iota(jnp.int32, s.shape, 1)) <= (q0 + cs - off)
            first_key = jnp.where(bk == 0, PAD, bk * kb - off)
            s = jnp.where(keep, s, NEG_BIG)
            s = jnp.where(key >= first_key, s, NEG_BIG)
        s_scr[hh, :, cs:] = s
        bmax_scr[hh, :, cs:] = jnp.broadcast_to(jnp.max(s, axis=0, keepdims=True), (SUBLANES, tq - cs))

    def softmax(hh, cs):
        m_prev = m_scr[hh, 0:1, cs:]
        m_new = jnp.maximum(m_prev, bmax_scr[hh, 0:1, cs:])
        alpha_scr[hh, :, cs:] = jnp.broadcast_to(jnp.exp2(m_prev - m_new), (SUBLANES, tq - cs))
        m_scr[hh, :, cs:] = jnp.broadcast_to(m_new, (SUBLANES, tq - cs))
        p_scr[hh, :, cs:] = jnp.exp2(s_scr[hh, :, cs:] - m_new).astype(BF16)

    def pv(hh, bk, cs):
        c0 = offset(bk) // CHUNK
        vt = jnp.concatenate([vt_scr[c0 + c, hh] for c in range(kb // CHUNK)], axis=1)
        acc_scr[hh, :, cs:] = alpha_scr[hh, 0:1, cs:] * acc_scr[hh, :, cs:] + _dot(vt, p_scr[hh, :, cs:])

    def step(bk, masked, cs_prev, cs):
        softmax(1, cs_prev)
        qk(0, bk, masked, cs)
        pv(0, bk - 1, cs_prev)
        softmax(0, cs)
        qk(1, bk, masked, cs)
        pv(1, bk - 1, cs_prev)

    def plain_step(bk, carry):
        step(bk, False, 0, 0)
        return carry

    def masked_step(bk, carry):
        lvl_prev, lvl = level(bk - 1), level(bk)
        for a in levels:
            for b in (a, a + kb):
                if b in levels:
                    @pl.when(jnp.logical_and(lvl_prev == a, lvl == b))
                    def _():
                        step(bk, True, a, b)
        return carry

    qk(0, 0, True, 0)
    softmax(0, 0)
    qk(1, 0, True, 0)
    lax.fori_loop(1, n_int, plain_step, 0)
    lax.fori_loop(jnp.maximum(n_int, 1), n_blocks, masked_step, 0)
    lvl_last = level(n_blocks - 1)
    for a in levels:
        @pl.when(lvl_last == a)
        def _():
            softmax(1, a)
            pv(0, n_blocks - 1, a)
            pv(1, n_blocks - 1, a)

    out_t = jnp.concatenate([acc_scr[hh, :MLA_V, :] / acc_scr[hh, MLA_V:MLA_V + 1, :] for hh in range(2)],
                            axis=0)
    qpos = q0 + lax.broadcasted_iota(jnp.int32, (1, tq), 1)
    out_t = jnp.where(qpos >= PAD, out_t, 0.0)
    for c in range(tq // CHUNK):
        o_ref[c * CHUNK:(c + 1) * CHUNK, :] = out_t[:, c * CHUNK:(c + 1) * CHUNK].T.astype(BF16)


def _attention(q, k, v, b, tp):
    tq = _pick_tile(tp, (1408, 384, 256))
    nq = tp // tq
    n = b * tp
    in_specs = [
        pl.BlockSpec((2, tq, LANES), lambda bi, hp, qi: (hp, bi * nq + qi, 0)),
        pl.BlockSpec((2, tp, LANES), lambda bi, hp, qi: (hp, bi, 0)),
        pl.BlockSpec((tp, LANES), lambda bi, hp, qi: (bi, hp)),
    ]
    return pl.pallas_call(
        functools.partial(_attn_kernel, tq=tq), grid=(b, MLA_HEADS // 2, nq), in_specs=in_specs,
        out_specs=pl.BlockSpec((tq, LANES), lambda bi, hp, qi: (bi * nq + qi, hp)),
        out_shape=jax.ShapeDtypeStruct((n, MLA_HEADS * MLA_V), BF16),
        scratch_shapes=[pltpu.VMEM((tp // CHUNK, 2, ATTN_V_ROWS, CHUNK), BF16),
                        pltpu.VMEM((2, ATTN_KV_BLOCK, tq), F32), pltpu.VMEM((2, SUBLANES, tq), F32),
                        pltpu.VMEM((2, ATTN_KV_BLOCK, tq), BF16),
                        pltpu.VMEM((2, SUBLANES, tq), F32), pltpu.VMEM((2, SUBLANES, tq), F32),
                        pltpu.VMEM((2, ATTN_V_ROWS, tq), F32)],
        compiler_params=_cparams("parallel", "parallel", "arbitrary"), name="attention",
    )(q, k, v)


POST_ROW_BLOCK = 256


def _post_kernel(*refs, n_in):
    h_ref = refs[0]
    a_refs = refs[1:1 + n_in]
    w_refs = refs[1 + n_in:1 + 2 * n_in]
    g_ref = refs[1 + 2 * n_in]
    o_ref = refs[2 + 2 * n_in]
    rows_per_block = min(POST_ROW_BLOCK, h_ref.shape[0])
    for r0 in range(0, h_ref.shape[0], rows_per_block):
        rows = slice(r0, r0 + rows_per_block)
        m = _dot(a_refs[0][rows, :], w_refs[0][...])
        for a_ref, w_ref in zip(a_refs[1:], w_refs[1:]):
            m = m + _dot(a_ref[rows, :], w_ref[...])
        o_ref[rows, :] = h_ref[rows, :] + _rms(m, g_ref[...])


def _post(h2, acts, wts, g):
    n, d = h2.shape
    tm = _pick_tile(n, (1024, 512, 384, 256, 128))
    rows = lambda i: (i, 0)
    in_specs = [pl.BlockSpec((tm, d), rows)]
    in_specs += [pl.BlockSpec((tm, a.shape[1]), rows) for a in acts]
    in_specs += [_const_spec(w.shape) for w in wts] + [_const_spec(g.shape)]
    return pl.pallas_call(
        functools.partial(_post_kernel, n_in=len(acts)), grid=(n // tm,), in_specs=in_specs,
        out_specs=pl.BlockSpec((tm, d), rows), out_shape=jax.ShapeDtypeStruct((n, d), F32),
        compiler_params=_cparams("parallel"), name="mixer_out",
    )(h2, *acts, *wts, g)


def _mlp_kernel(h_ref, g1_ref, wup_ref, wdn_ref, g2_ref, o_ref, *, ff_chunk):
    x = h_ref[...]
    hn = _rms(x, g1_ref[...]).astype(BF16)
    d_ff = wup_ref.shape[1]
    y = None
    for c0 in range(0, d_ff, ff_chunk):
        u = jnp.maximum(_dot(hn, wup_ref[:, c0:c0 + ff_chunk]), 0.0)
        part = _dot((u * u).astype(BF16), wdn_ref[c0:c0 + ff_chunk, :])
        y = part if y is None else y + part
    o_ref[...] = x + _rms(y, g2_ref[...])


def _mlp(h2, g1, wup, wdn, g2):
    n, d = h2.shape
    tm = _pick_tile(n, (512, 256, 128))
    rows = lambda i: (i, 0)
    in_specs = [pl.BlockSpec((tm, d), rows), _const_spec(g1.shape), _const_spec(wup.shape),
                _const_spec(wdn.shape), _const_spec(g2.shape)]
    return pl.pallas_call(
        functools.partial(_mlp_kernel, ff_chunk=1024), grid=(n // tm,), in_specs=in_specs,
        out_specs=pl.BlockSpec((tm, d), rows), out_shape=jax.ShapeDtypeStruct((n, d), F32),
        compiler_params=_cparams("parallel"), name="mlp",
    )(h2, g1, wup, wdn, g2)


def _mlp_last(h3, g1, wup, wdn, g2, lead):
    b, tp, d = h3.shape
    seq = tp - lead
    tm = _pick_tile(seq, (512, 256, 128))
    in_specs = [pl.BlockSpec((pl.Element(tm), pl.Element(d)), lambda bi, i: (pl.multiple_of(bi * tp + lead + i * tm, CHUNK), 0)),
                _const_spec(g1.shape), _const_spec(wup.shape), _const_spec(wdn.shape), _const_spec(g2.shape)]
    return pl.pallas_call(
        functools.partial(_mlp_kernel, ff_chunk=1024), grid=(b, seq // tm), in_specs=in_specs,
        out_specs=pl.BlockSpec((None, tm, d), lambda bi, i: (bi, i, 0)),
        out_shape=jax.ShapeDtypeStruct((b, seq, d), F32),
        compiler_params=_cparams("parallel", "parallel"), name="mlp_last",
    )(h3.reshape(b * tp, d), g1, wup, wdn, g2)


SEG_LEN = CHUNK // SUBLANES


def _interleave_matrix():
    p = np.zeros((CHUNK, CHUNK), np.float32)
    for j in range(SEG_LEN):
        for s in range(SUBLANES):
            p[SUBLANES * j + s, SEG_LEN * s + j] = 1.0
    return p


def _reorder_rows(perm, x):
    tiles = [_dot(perm, x[t0:t0 + CHUNK]).astype(BF16) for t0 in range(0, x.shape[0], CHUNK)]
    return tiles[0] if len(tiles) == 1 else jnp.concatenate(tiles, axis=0)


def _causal_conv_interleaved(pre, tail_scr, w_ref, b_ref, cols):
    k = w_ref.shape[0]
    halo = (k - 1) * SUBLANES
    width = pre.shape[1]
    first_sub = lax.broadcasted_iota(jnp.int32, (k - 1, SUBLANES, width), 1) == 0
    prev_last = tail_scr[:, cols]
    outs = []
    for t0 in range(0, pre.shape[0], CHUNK):
        tile = pre[t0:t0 + CHUNK]
        last = tile[CHUNK - halo:]
        fix = jnp.where(first_sub, pltpu.roll(prev_last.reshape(k - 1, SUBLANES, width), 1, 1),
                        pltpu.roll(last.reshape(k - 1, SUBLANES, width), 1, 1)).reshape(halo, width)
        out = b_ref[:, cols]
        for i in range(k):
            shift = k - 1 - i
            if shift == 0:
                xs = tile
            else:
                xs = jnp.concatenate([fix[halo - shift * SUBLANES:], tile[:CHUNK - shift * SUBLANES]], axis=0)
            out = out + xs * w_ref[i:i + 1, cols]
        outs.append(out)
        prev_last = last
    tail_scr[:, cols] = prev_last
    return jnp.concatenate(outs, axis=0)


def _proj_odd_kernel(h_ref, g_ref, wx_ref, wy_ref, cw_ref, cb_ref, perm_ref, x_ref, gate_ref, tail_scr):
    @pl.when(pl.program_id(0) == 0)
    def _():
        tail_scr[...] = jnp.zeros_like(tail_scr)

    hn = _reorder_rows(perm_ref[...], _rms(h_ref[...], g_ref[...]).astype(BF16))
    for c0 in range(0, LRU_WIDTH, LRU_CONV_COLS):
        cols = slice(c0, c0 + LRU_CONV_COLS)
        x_ref[:, cols] = _causal_conv_interleaved(_dot(hn, wx_ref[:, cols]), tail_scr, cw_ref, cb_ref,
                                                  cols).astype(BF16)
        y = _dot(hn, wy_ref[:, cols])
        gelu = 0.5 * y * (1.0 + jnp.tanh(np.sqrt(2.0 / np.pi).astype(np.float32) * (y + 0.044715 * (y * y * y))))
        gate_ref[:, cols] = gelu.astype(BF16)


def _proj_odd(h2, g, wx, wy, cw, cb, perm):
    n, d = h2.shape
    tm = _pick_tile(n, (1024, 512, 384, 256, 128))
    rows = lambda i: (i, 0)
    w = wx.shape[1]
    return pl.pallas_call(
        _proj_odd_kernel, grid=(n // tm,),
        in_specs=[pl.BlockSpec((tm, d), rows)] + [_const_spec(a.shape) for a in (g, wx, wy, cw, cb, perm)],
        out_specs=(pl.BlockSpec((tm, w), rows), pl.BlockSpec((tm, w), rows)),
        out_shape=(jax.ShapeDtypeStruct((n, w), BF16), jax.ShapeDtypeStruct((n, w), BF16)),
        scratch_shapes=[pltpu.VMEM(((cw.shape[0] - 1) * SUBLANES, w), F32)],
        compiler_params=_cparams("arbitrary"), name="proj_odd",
    )(h2, g, wx, wy, cw, cb, perm)


LRU_BATCH_BLOCK = 4


def _rglru_kernel(x_ref, gate_ref, wa_ref, ba_ref, wi_ref, bi_ref, lam_ref, unperm_ref, o_ref, h_scr):
    t = pl.program_id(1)

    @pl.when(t == 0)
    def _():
        h_scr[...] = jnp.zeros_like(h_scr)

    for bb in range(x_ref.shape[0]):
        _rglru_tile(t, x_ref.at[bb], gate_ref.at[bb], wa_ref, ba_ref, wi_ref, bi_ref, lam_ref, unperm_ref,
                    o_ref.at[bb], h_scr.at[bb])


def _rglru_tile(t, x_ref, gate_ref, wa_ref, ba_ref, wi_ref, bi_ref, lam_ref, unperm_ref, o_ref, h_scr):
    L = x_ref.shape[0]
    xr = x_ref[...].astype(F32)
    rs, gs = [], []
    for nb in range(LRU_BLOCKS):
        sl = slice(nb * LRU_BLOCK, (nb + 1) * LRU_BLOCK)
        xb = x_ref[:, sl]
        rs.append(_dot(xb, wa_ref[nb]))
        gs.append(_dot(xb, wi_ref[nb]))
    r = _sigmoid(jnp.concatenate(rs, axis=1) + ba_ref[...])
    ig = _sigmoid(jnp.concatenate(gs, axis=1) + bi_ref[...])
    log_a = (-LRU_C) * r * _softplus(-lam_ref[...])
    a = jnp.exp(log_a)
    row = lax.broadcasted_iota(jnp.int32, (L, 1), 0)
    valid = (t * L + SEG_LEN * (row % SUBLANES) + row // SUBLANES) >= PAD
    om = 1.0 - a * a
    root = jnp.where(om > 0.0, om * lax.rsqrt(om), 0.0)
    u = jnp.where(valid, root * (ig * xr), 0.0)

    slab = lambda v, j: v[j * SUBLANES:(j + 1) * SUBLANES, :]
    h_loc = [slab(u, 0)]
    prod = [slab(a, 0)]
    for j in range(1, SEG_LEN):
        h_loc.append(slab(a, j) * h_loc[-1] + slab(u, j))
        prod.append(slab(a, j) * prod[-1])
    carry_in = h_scr[0:1, :]
    carries = []
    for s in range(SUBLANES):
        carries.append(carry_in)
        carry_in = h_loc[-1][s:s + 1, :] + prod[-1][s:s + 1, :] * carry_in
    h_scr[...] = jnp.broadcast_to(carry_in, h_scr.shape)
    carry = jnp.concatenate(carries, axis=0)
    hs = jnp.concatenate([h_loc[j] + prod[j] * carry for j in range(SEG_LEN)], axis=0)
    o_ref[...] = _reorder_rows(unperm_ref[...], (hs * gate_ref[...].astype(F32)).astype(BF16))


def _rglru(x3, gate3, wa, ba, wi, bi, lam, unperm):
    b, tp, w = x3.shape
    tt = CHUNK
    gb = _pick_tile(b, (LRU_BATCH_BLOCK, 1))
    row3 = lambda bi_, ti: (bi_, ti, 0)
    in_specs = [pl.BlockSpec((gb, tt, w), row3), pl.BlockSpec((gb, tt, w), row3)]
    in_specs += [_const_spec(a.shape) for a in (wa, ba, wi, bi, lam, unperm)]
    return pl.pallas_call(
        _rglru_kernel, grid=(b // gb, tp // tt), in_specs=in_specs,
        out_specs=pl.BlockSpec((gb, tt, w), row3),
        out_shape=jax.ShapeDtypeStruct((b, tp, w), BF16),
        scratch_shapes=[pltpu.VMEM((gb, SUBLANES, w), F32)],
        compiler_params=_cparams("parallel", "arbitrary"), name="rglru",
    )(x3, gate3, wa, ba, wi, bi, lam, unperm)


def _rope_tables(tp):
    f32 = np.float32
    inv = f32(ROPE_BASE) ** (-np.arange(0, MLA_ROPE, 2, dtype=f32) / f32(MLA_ROPE))
    pos = np.maximum(np.arange(tp, dtype=f32) - f32(PAD), f32(0))
    ang = (pos[:, None] * inv[None, :]).astype(f32)
    cos, sin = np.cos(ang).astype(f32), np.sin(ang).astype(f32)
    zeros_lo = np.zeros((tp, MLA_NOPE), f32)
    zeros_hi = np.zeros((tp, LANES - MLA_NOPE - MLA_ROPE), f32)
    ck = np.concatenate([zeros_lo, cos, cos, zeros_hi], axis=1)
    sk = np.concatenate([zeros_lo, sin, sin, zeros_hi], axis=1)
    scale = f32((MLA_NOPE + MLA_ROPE) ** -0.5 * np.log2(np.e))
    cq = np.concatenate([np.ones((tp, MLA_NOPE), f32), cos, cos, zeros_hi], axis=1) * scale
    sq = sk * scale
    return tuple(jnp.asarray(t) for t in (cq, sq, ck, sk))


def _even_weights(w_in, w_q_up, w_kv_up):
    offs = np.cumsum((SSD_D_INNER, SSD_CONV_CH, SSD_HEADS, MLA_Q_RANK, MLA_KV_RANK, MLA_ROPE))
    wz = w_in[:, :offs[0]]
    wxbc = w_in[:, offs[0]:offs[1]]
    wdt = w_in[:, offs[1]:offs[2]]
    wcq = w_in[:, offs[2]:offs[3]]
    wckv = w_in[:, offs[3]:offs[4]]
    wkr = w_in[:, offs[4]:offs[5]]
    hi_pad = LANES - MLA_NOPE - MLA_ROPE
    wdt_p = jnp.pad(wdt, ((0, 0), (0, LANES - SSD_HEADS)))
    wkr_p = jnp.pad(wkr, ((0, 0), (MLA_NOPE, hi_pad)))
    wq = w_q_up.reshape(MLA_Q_RANK, MLA_HEADS, MLA_NOPE + MLA_ROPE)
    wq_p = jnp.pad(wq, ((0, 0), (0, 0), (0, hi_pad))).reshape(MLA_Q_RANK, MLA_HEADS * LANES)
    wkv = w_kv_up.reshape(MLA_KV_RANK, MLA_HEADS, MLA_NOPE + MLA_V)
    wk_p = jnp.pad(wkv[..., :MLA_NOPE], ((0, 0), (0, 0), (0, LANES - MLA_NOPE))).reshape(
        MLA_KV_RANK, MLA_HEADS * LANES)
    wv = wkv[..., MLA_NOPE:].reshape(MLA_KV_RANK, MLA_HEADS * MLA_V)
    bf = lambda a: a.astype(BF16)
    return dict(wz=bf(wz), wxbc=bf(wxbc), wdt=bf(wdt_p), wkr_p=bf(wkr_p), wcq=bf(wcq),
                wckv=bf(wckv), wq_p=bf(wq_p), wk_p=bf(wk_p), wv=bf(wv))


def _row(a):
    return a.reshape(1, -1).astype(F32)


def kernel(x, meta_tokens, mix_pre_g, mix_post_g, mlp_pre_g, mlp_post_g, w_up, w_down, w_in, ssd_conv_w,
           ssd_conv_b, ssd_dt_bias, ssd_a_log, ssd_d, ssd_norm_g, mla_q_norm_g, mla_w_q_up, mla_kv_norm_g,
           mla_w_kv_up, w_out_ab, rg_w_x, rg_w_y, rg_conv_w, rg_conv_b, rg_w_a, rg_b_a, rg_w_i, rg_b_i,
           rg_lambda, rg_w_out):
    b, seq, d = x.shape
    depth = mix_pre_g.shape[0]
    tp = PAD + N_META + seq
    n = b * tp
    meta = jnp.broadcast_to(meta_tokens[None].astype(x.dtype), (b, N_META, d))
    h = jnp.concatenate([jnp.zeros((b, PAD, d), x.dtype), meta, x], axis=1).reshape(n, d)
    tables = _rope_tables(tp)

    for layer in range(depth):
        if layer % 2 == 0:
            e = layer // 2
            w = _even_weights(w_in[e], mla_w_q_up[e], mla_w_kv_up[e])
            wts = (w["wz"], w["wxbc"], w["wdt"], w["wkr_p"], w["wcq"], _row(mla_q_norm_g[e]),
                   w["wckv"], _row(mla_kv_norm_g[e]), w["wq_p"], w["wk_p"], w["wv"],
                   ssd_conv_w[e].astype(F32), _row(ssd_conv_b[e]))
            z, xbc, dt, q, k, v = _proj_even(h.reshape(b, tp, d), _row(mix_pre_g[layer]), wts, tables)
            lane_pad = lambda a: jnp.pad(_row(a), ((0, 0), (0, LANES - SSD_HEADS)))
            y_ssd = _ssd(xbc.reshape(b, tp, SSD_CONV_CH), dt.reshape(b, tp, LANES), z.reshape(b, tp, SSD_D_INNER),
                         lane_pad(ssd_dt_bias[e]), lane_pad(ssd_a_log[e]),
                         _row(jnp.repeat(ssd_d[e], SSD_HEAD_DIM)), _row(ssd_norm_g[e]))
            y_att = _attention(q, k, v, b, tp)
            wo = w_out_ab[e].astype(BF16)
            h = _post(h, (y_ssd.reshape(n, SSD_D_INNER), y_att), (wo[:SSD_D_INNER], wo[SSD_D_INNER:]),
                      _row(mix_post_g[layer]))
        else:
            o = layer // 2
            perm = _interleave_matrix()
            xr, gate = _proj_odd(h, _row(mix_pre_g[layer]), rg_w_x[o].astype(BF16), rg_w_y[o].astype(BF16),
                                 rg_conv_w[o].astype(F32), _row(rg_conv_b[o]), jnp.asarray(perm, BF16))
            y = _rglru(xr.reshape(b, tp, LRU_WIDTH), gate.reshape(b, tp, LRU_WIDTH), rg_w_a[o].astype(BF16),
                       _row(rg_b_a[o]), rg_w_i[o].astype(BF16), _row(rg_b_i[o]), _row(rg_lambda[o]),
                       jnp.asarray(perm.T, BF16))
            h = _post(h, (y.reshape(n, LRU_WIDTH),), (rg_w_out[o].astype(BF16),), _row(mix_post_g[layer]))
        mlp_args = (_row(mlp_pre_g[layer]), w_up[layer].astype(BF16), w_down[layer].astype(BF16),
                    _row(mlp_post_g[layer]))
        if layer == depth - 1:
            return _mlp_last(h.reshape(b, tp, d), *mlp_args, PAD + N_META)
        h = _mlp(h, *mlp_args)
```

```python
import functools

import jax
import jax.numpy as jnp
import numpy as np
from jax import lax
from jax.experimental import pallas as pl
from jax.experimental.pallas import tpu as pltpu

F32 = jnp.float32
BF16 = jnp.bfloat16

N_META = 16
CHUNK = 128
PAD = CHUNK - N_META
EPS = 1e-6
SSD_HEADS = 16
SSD_HEAD_DIM = 64
SSD_D_INNER = SSD_HEADS * SSD_HEAD_DIM
SSD_GROUPS = 2
SSD_STATE = 128
SSD_BC = SSD_GROUPS * SSD_STATE
SSD_CONV_CH = SSD_D_INNER + 2 * SSD_BC
MLA_HEADS = 16
MLA_NOPE = 64
MLA_ROPE = 32
MLA_V = 64
MLA_Q_RANK = 384
MLA_KV_RANK = 256
ROPE_BASE = 10000.0
LRU_BLOCKS = 10
LRU_BLOCK = 128
LRU_WIDTH = LRU_BLOCKS * LRU_BLOCK
LRU_C = 8.0
LRU_CONV_COLS = 256
LANES = 128
SUBLANES = 8
VMEM_LIMIT_BYTES = 56 * 1024 * 1024
NEG_BIG = -1e30


def _cparams(*sem):
    return pltpu.CompilerParams(dimension_semantics=sem, vmem_limit_bytes=VMEM_LIMIT_BYTES)


def _pick_tile(n, candidates):
    for c in candidates:
        if n % c == 0:
            return c
    raise ValueError(f"no tile in {candidates} divides {n}")


def _const_spec(shape):
    nd = len(shape)
    return pl.BlockSpec(shape, lambda *_: (0,) * nd)


def _rms(x, g):
    ms = jnp.mean(x * x, axis=-1, keepdims=True)
    return x * lax.rsqrt(ms + EPS) * g


def _sigmoid(x):
    return 1.0 / (1.0 + jnp.exp(-x))


def _softplus(x):
    return jnp.maximum(x, 0.0) + jnp.log(1.0 + jnp.exp(-jnp.abs(x)))


def _dot(a, b):
    return jnp.dot(a, b, preferred_element_type=F32)


def _dot_nt(a, b):
    return lax.dot_general(a, b, (((1,), (1,)), ((), ())), preferred_element_type=F32)


def _shift_rows(cur, tail, k):
    if k == 0:
        return cur
    rolled = pltpu.roll(cur, k, 0)
    row = lax.broadcasted_iota(jnp.int32, tail.shape, 0)
    top = jnp.where(row < k, pltpu.roll(tail, k, 0), rolled[0:SUBLANES])
    return jnp.concatenate([top, rolled[SUBLANES:]], axis=0)


def _causal_conv(cur, tail_scr, w_ref, b_ref, cols):
    k = w_ref.shape[0]
    tail = tail_scr[:, cols]
    out = b_ref[:, cols]
    for i in range(k):
        out = out + _shift_rows(cur, tail, k - 1 - i) * w_ref[i:i + 1, cols]
    tail_scr[:, cols] = cur[cur.shape[0] - SUBLANES:]
    return out


CONV_COLS = 256
Z_COLS = 256


def _rotate_half(x):
    half = MLA_ROPE // 2
    lane = lax.broadcasted_iota(jnp.int32, x.shape, 1) % LANES
    from_right = pltpu.roll(x, x.shape[1] - half, 1)
    from_left = pltpu.roll(x, half, 1)
    return jnp.where(lane < MLA_NOPE + half, -from_right, from_left)


def _proj_even_kernel(h_ref, g_ref, wz_ref, wxbc_ref, wdt_ref, wkrp_ref, wcq_ref, qg_ref,
                      wckv_ref, kvg_ref, wqp_ref, wkp_ref, wv_ref, cw_ref, cb_ref,
                      cq_ref, sq_ref, ck_ref, sk_ref,
                      z_ref, xbc_ref, dt_ref, q_ref, k_ref, v_ref, tail_scr):
    @pl.when(pl.program_id(1) == 0)
    def _():
        tail_scr[...] = jnp.zeros_like(tail_scr)

    hn = _rms(h_ref[...], g_ref[...]).astype(BF16)

    def conv_slab(c0):
        cols = slice(c0, c0 + CONV_COLS)
        conv = _causal_conv(_dot(hn, wxbc_ref[:, cols]), tail_scr, cw_ref, cb_ref, cols)
        xbc_ref[:, cols] = (conv * _sigmoid(conv)).astype(BF16)

    def z_slab(c0):
        cols = slice(c0, c0 + Z_COLS)
        z_ref[:, cols] = _dot(hn, wz_ref[:, cols]).astype(BF16)

    cqn = _rms(_dot(hn, wcq_ref[...]), qg_ref[...]).astype(BF16)
    ckvn = _rms(_dot(hn, wckv_ref[...]), kvg_ref[...]).astype(BF16)
    kr_raw = _dot(hn, wkrp_ref[...])
    kr = kr_raw * ck_ref[...] + _rotate_half(kr_raw) * sk_ref[...]
    dt_ref[...] = _dot(hn, wdt_ref[...])
    cq_t = jnp.concatenate([cq_ref[...]] * 2, axis=1)
    sq_t = jnp.concatenate([sq_ref[...]] * 2, axis=1)
    kr2 = jnp.concatenate([kr] * 2, axis=1)

    def head_pair(hd):
        sl = slice(hd * LANES, (hd + 2) * LANES)
        q_raw = _dot(cqn, wqp_ref[:, sl])
        qh = (q_raw * cq_t + _rotate_half(q_raw) * sq_t).astype(BF16)
        kh = (_dot(ckvn, wkp_ref[:, sl]) + kr2).astype(BF16)
        for j in range(2):
            q_ref[hd + j] = qh[:, j * LANES:(j + 1) * LANES]
            k_ref[hd + j] = kh[:, j * LANES:(j + 1) * LANES]

    def v_proj(_):
        v_ref[...] = _dot(ckvn, wv_ref[...]).astype(BF16)

    convs = [(conv_slab, c0) for c0 in range(0, SSD_CONV_CH, CONV_COLS)]
    pairs = [(head_pair, hd) for hd in range(0, MLA_HEADS, 2)]
    heavy = []
    while convs or pairs:
        heavy += convs[:1] + pairs[:1]
        convs, pairs = convs[1:], pairs[1:]
    light = [(v_proj, None)] + [(z_slab, c0) for c0 in range(0, SSD_D_INNER, Z_COLS)]
    order = []
    for i, item in enumerate(heavy):
        order.append(item)
        if (i + 1) % 3 == 0 and len(light) > 1:
            order.append(light.pop(0))
    order += light
    for fn, arg in order:
        fn(arg)


def _proj_even(h3, g, wts, tables):
    b, tp, d = h3.shape
    tm = _pick_tile(tp, (384, 128))
    nt = tp // tm
    n = b * tp
    row3 = lambda bi, i: (bi, i, 0)
    flat = lambda bi, i: (bi * nt + i, 0)
    tab = lambda bi, i: (i, 0)
    in_specs = [pl.BlockSpec((None, tm, d), row3), _const_spec(g.shape)]
    in_specs += [_const_spec(w.shape) for w in wts]
    in_specs += [pl.BlockSpec((tm, LANES), tab) for _ in tables]
    out_shape = (
        jax.ShapeDtypeStruct((n, SSD_D_INNER), BF16),
        jax.ShapeDtypeStruct((n, SSD_CONV_CH), BF16),
        jax.ShapeDtypeStruct((n, LANES), F32),
        jax.ShapeDtypeStruct((MLA_HEADS, n, LANES), BF16),
        jax.ShapeDtypeStruct((MLA_HEADS, n, LANES), BF16),
        jax.ShapeDtypeStruct((n, MLA_HEADS * MLA_V), BF16),
    )
    head3 = lambda bi, i: (0, bi * nt + i, 0)
    out_specs = (
        pl.BlockSpec((tm, SSD_D_INNER), flat),
        pl.BlockSpec((tm, SSD_CONV_CH), flat),
        pl.BlockSpec((tm, LANES), flat),
        pl.BlockSpec((MLA_HEADS, tm, LANES), head3),
        pl.BlockSpec((MLA_HEADS, tm, LANES), head3),
        pl.BlockSpec((tm, MLA_HEADS * MLA_V), flat),
    )
    return pl.pallas_call(
        _proj_even_kernel, grid=(b, nt), in_specs=in_specs, out_specs=out_specs, out_shape=out_shape,
        scratch_shapes=[pltpu.VMEM((SUBLANES, SSD_CONV_CH), F32)],
        compiler_params=_cparams("parallel", "arbitrary"), name="proj_even",
    )(h3, g, *wts, *tables)


def _cumsum_rows(x):
    n = x.shape[0]
    row = lax.broadcasted_iota(jnp.int32, x.shape, 0)
    d = 1
    while d < n:
        x = x + jnp.where(row >= d, pltpu.roll(x, d, 0), 0.0)
        d *= 2
    return x


SSD_BATCH_BLOCK = 4


def _ssd_chunk(c, xbc, dt_raw, zf, dtb, alog, dsk, ng, state_ref, y_scr):
    L = CHUNK
    heads_per_group = SSD_HEADS // SSD_GROUPS
    row = lax.broadcasted_iota(jnp.int32, (L, 1), 0)
    valid = (c * L + row) >= PAD
    act = xbc.astype(F32)
    xs = act[:, :SSD_D_INNER]
    bm = act[:, SSD_D_INNER:SSD_D_INNER + SSD_BC]
    cm = act[:, SSD_D_INNER + SSD_BC:]

    dtv = jnp.where(valid, _softplus(dt_raw + dtb), 0.0)
    a_cs = _cumsum_rows(dtv * (-jnp.exp(alog)))
    a_cs_t = a_cs.T
    dt_t = dtv.T
    tot = a_cs[L - 1:L, :]
    w_t = jnp.exp(a_cs_t[:, L - 1:L] - a_cs_t) * dt_t
    shifted_t = a_cs_t - jnp.log(dt_t)
    cd = jnp.exp(tot)
    causal = lax.broadcasted_iota(jnp.int32, (L, L), 0) >= lax.broadcasted_iota(jnp.int32, (L, L), 1)
    lo = lax.broadcasted_iota(jnp.int32, (L, LANES), 1) < SSD_HEAD_DIM

    for g in range(SSD_GROUPS):
        cc = cm[:, g * SSD_STATE:(g + 1) * SSD_STATE]
        bc = bm[:, g * SSD_STATE:(g + 1) * SSD_STATE]
        bc_t = bc.T
        cb = _dot_nt(cc.astype(BF16), bc.astype(BF16))
        for hp in range(heads_per_group // 2):
            pair = g * (heads_per_group // 2) + hp
            sl = slice(pair * LANES, (pair + 1) * LANES)
            xs_pair = xs[:, sl]
            st_pair = state_ref[:, sl]
            xs_b = xs_pair.astype(BF16)
            rhs = jnp.concatenate([xs_b, st_pair.astype(BF16)], axis=0)
            ys, sts, cds = [], [], []
            for hh in range(2):
                h = 2 * pair + hh
                a_col = jnp.broadcast_to(a_cs[:, h:h + 1], (L, L))
                m = cb * jnp.exp(jnp.where(causal, a_col - shifted_t[h:h + 1, :], -jnp.inf))
                lhs = jnp.concatenate([m, cc * jnp.exp(a_col)], axis=1).astype(BF16)
                ys.append(_dot(lhs, rhs))
                sts.append(_dot((bc_t * w_t[h:h + 1, :]).astype(BF16), xs_b))
                cds.append(cd[:, h:h + 1])
            y_scr[:, sl] = jnp.where(lo, ys[0], ys[1]) + dsk[:, sl] * xs_pair
            state_ref[:, sl] = jnp.where(lo, cds[0], cds[1]) * st_pair + jnp.where(lo, sts[0], sts[1])

    return _rms(y_scr[...] * (zf * _sigmoid(zf)), ng).astype(BF16)


def _ssd_kernel(xbc_ref, dt_ref, z_ref, dtb_ref, alog_ref, dsk_ref, ng_ref, y_ref, state_scr, y_scr):
    c = pl.program_id(1)

    @pl.when(c == 0)
    def _():
        state_scr[...] = jnp.zeros_like(state_scr)

    for bb in range(xbc_ref.shape[0]):
        y_ref[bb] = _ssd_chunk(c, xbc_ref[bb], dt_ref[bb], z_ref[bb].astype(F32), dtb_ref[...], alog_ref[...],
                               dsk_ref[...], ng_ref[...], state_scr.at[bb], y_scr.at[bb])


def _ssd(xbc3, dt3, z3, dtb, alog, dsk, ng):
    b, tp, _ = xbc3.shape
    nc = tp // CHUNK
    gb = _pick_tile(b, (SSD_BATCH_BLOCK, 1))
    row3 = lambda bi, ci: (bi, ci, 0)
    in_specs = [
        pl.BlockSpec((gb, CHUNK, SSD_CONV_CH), row3),
        pl.BlockSpec((gb, CHUNK, LANES), row3),
        pl.BlockSpec((gb, CHUNK, SSD_D_INNER), row3),
    ] + [_const_spec(a.shape) for a in (dtb, alog, dsk, ng)]
    return pl.pallas_call(
        _ssd_kernel, grid=(b // gb, nc), in_specs=in_specs,
        out_specs=pl.BlockSpec((gb, CHUNK, SSD_D_INNER), row3),
        out_shape=jax.ShapeDtypeStruct((b, tp, SSD_D_INNER), BF16),
        scratch_shapes=[pltpu.VMEM((gb, SSD_STATE, SSD_D_INNER), F32),
                        pltpu.VMEM((gb, CHUNK, SSD_D_INNER), F32)],
        compiler_params=_cparams("parallel", "arbitrary"), name="ssd",
    )(xbc3, dt3, z3, dtb, alog, dsk, ng)


ATTN_KV_BLOCK = 256
ATTN_V_ROWS = MLA_V + 16
ATTN_COL_SLAB = 256


def _attn_kernel(q_ref, k_ref, v_ref, o_ref, vt_scr, s_scr, bmax_scr, m_scr, acc_scr, *, tq):
    kb = ATTN_KV_BLOCK
    qi = pl.program_id(2)
    q0 = qi * tq
    end = q0 + tq
    n_int = q0 // kb
    n_blocks = (end + kb - 1) // kb
    last_off = end - kb

    @pl.when(qi == 0)
    def _():
        ones = jnp.ones((ATTN_V_ROWS - MLA_V, CHUNK), BF16)
        for c in range(vt_scr.shape[0]):
            vt = v_ref[c * CHUNK:(c + 1) * CHUNK, :].astype(F32).T.astype(BF16)
            for hh in range(2):
                vt_scr[c, hh, :MLA_V, :] = vt[hh * MLA_V:(hh + 1) * MLA_V, :]
                vt_scr[c, hh, MLA_V:, :] = ones

    m_scr[...] = jnp.full_like(m_scr, NEG_BIG)
    acc_scr[...] = jnp.zeros_like(acc_scr)

    def offset(bk):
        return pl.multiple_of(jnp.minimum(bk * kb, last_off), CHUNK)

    levels = tuple(range(0, tq - kb + 1, kb))

    def level(bk):
        return jnp.minimum(jnp.maximum(offset(bk) - q0, 0) // kb, len(levels) - 1) * kb

    def col_slabs(cs):
        cuts = sorted({cs} | {c for c in range(tq, cs, -ATTN_COL_SLAB)})
        if len(cuts) > 2 and cuts[1] - cuts[0] < ATTN_COL_SLAB // 2:
            del cuts[1]
        return [slice(a, b) for a, b in zip(cuts[:-1], cuts[1:])]

    def qk(hh, bk, masked, cols):
        off = offset(bk)
        s = _dot_nt(k_ref[hh, pl.ds(off, kb), :], q_ref[hh, cols, :])
        if masked:
            key = lax.broadcasted_iota(jnp.int32, s.shape, 0)
            keep = (key - lax.broadcasted_iota(jnp.int32, s.shape, 1)) <= (q0 + cols.start - off)
            first_key = jnp.where(bk == 0, PAD, bk * kb - off)
            s = jnp.where(keep, s, NEG_BIG)
            s = jnp.where(key >= first_key, s, NEG_BIG)
        s_scr[hh, :, cols] = s
        bmax_scr[hh, :, cols] = jnp.broadcast_to(jnp.max(s, axis=0, keepdims=True), (SUBLANES, s.shape[1]))

    def softmax_pv(hh, bk, cols):
        m_prev = m_scr[hh, 0:1, cols]
        m_new = jnp.maximum(m_prev, bmax_scr[hh, 0:1, cols])
        alpha = jnp.exp2(m_prev - m_new)
        m_scr[hh, :, cols] = jnp.broadcast_to(m_new, (SUBLANES, m_new.shape[1]))
        p = jnp.exp2(s_scr[hh, :, cols] - m_new).astype(BF16)
        c0 = offset(bk) // CHUNK
        vt = jnp.concatenate([vt_scr[c0 + c, hh] for c in range(kb // CHUNK)], axis=1)
        acc_scr[hh, :, cols] = alpha * acc_scr[hh, :, cols] + _dot(vt, p)

    def step(bk, masked, cs_prev, cs):
        prev_slabs, cur_slabs = col_slabs(cs_prev), col_slabs(cs)
        for j in range(max(len(prev_slabs), len(cur_slabs))):
            if j < len(prev_slabs):
                softmax_pv(1, bk - 1, prev_slabs[j])
            if j < len(cur_slabs):
                qk(0, bk, masked, cur_slabs[j])
        for cols in cur_slabs:
            softmax_pv(0, bk, cols)
            qk(1, bk, masked, cols)

    def plain_step(bk, carry):
        step(bk, False, 0, 0)
        return carry

    def masked_step(bk, carry):
        lvl_prev, lvl = level(bk - 1), level(bk)
        for a in levels:
            for b in (a, a + kb):
                if b in levels:
                    @pl.when(jnp.logical_and(lvl_prev == a, lvl == b))
                    def _():
                        step(bk, True, a, b)
        return carry

    for cols in col_slabs(0):
        qk(0, 0, True, cols)
    for cols in col_slabs(0):
        softmax_pv(0, 0, cols)
        qk(1, 0, True, cols)
    lax.fori_loop(1, n_int, plain_step, 0)
    lax.fori_loop(jnp.maximum(n_int, 1), n_blocks, masked_step, 0)
    lvl_last = level(n_blocks - 1)
    for a in levels:
        @pl.when(lvl_last == a)
        def _():
            for cols in col_slabs(a):
                softmax_pv(1, n_blocks - 1, cols)

    out_t = jnp.concatenate([acc_scr[hh, :MLA_V, :] / acc_scr[hh, MLA_V:MLA_V + 1, :] for hh in range(2)],
                            axis=0)
    qpos = q0 + lax.broadcasted_iota(jnp.int32, (1, tq), 1)
    out_t = jnp.where(qpos >= PAD, out_t, 0.0)
    for c in range(tq // CHUNK):
        o_ref[c * CHUNK:(c + 1) * CHUNK, :] = out_t[:, c * CHUNK:(c + 1) * CHUNK].T.astype(BF16)


def _attention(q, k, v, b, tp):
    tq = _pick_tile(tp, (1408, 384, 256))
    nq = tp // tq
    n = b * tp
    in_specs = [
        pl.BlockSpec((2, tq, LANES), lambda bi, hp, qi: (hp, bi * nq + qi, 0)),
        pl.BlockSpec((2, tp, LANES), lambda bi, hp, qi: (hp, bi, 0)),
        pl.BlockSpec((tp, LANES), lambda bi, hp, qi: (bi, hp)),
    ]
    return pl.pallas_call(
        functools.partial(_attn_kernel, tq=tq), grid=(b, MLA_HEADS // 2, nq), in_specs=in_specs,
        out_specs=pl.BlockSpec((tq, LANES), lambda bi, hp, qi: (bi * nq + qi, hp)),
        out_shape=jax.ShapeDtypeStruct((n, MLA_HEADS * MLA_V), BF16),
        scratch_shapes=[pltpu.VMEM((tp // CHUNK, 2, ATTN_V_ROWS, CHUNK), BF16),
                        pltpu.VMEM((2, ATTN_KV_BLOCK, tq), F32), pltpu.VMEM((2, SUBLANES, tq), F32),
                        pltpu.VMEM((2, SUBLANES, tq), F32), pltpu.VMEM((2, ATTN_V_ROWS, tq), F32)],
        compiler_params=_cparams("parallel", "parallel", "arbitrary"), name="attention",
    )(q, k, v)


POST_ROW_BLOCK = 256


def _post_kernel(*refs, n_in):
    h_ref = refs[0]
    a_refs = refs[1:1 + n_in]
    w_refs = refs[1 + n_in:1 + 2 * n_in]
    g_ref = refs[1 + 2 * n_in]
    o_ref = refs[2 + 2 * n_in]
    rows_per_block = min(POST_ROW_BLOCK, h_ref.shape[0])
    for r0 in range(0, h_ref.shape[0], rows_per_block):
        rows = slice(r0, r0 + rows_per_block)
        m = _dot(a_refs[0][rows, :], w_refs[0][...])
        for a_ref, w_ref in zip(a_refs[1:], w_refs[1:]):
            m = m + _dot(a_ref[rows, :], w_ref[...])
        o_ref[rows, :] = h_ref[rows, :] + _rms(m, g_ref[...])


def _post(h2, acts, wts, g):
    n, d = h2.shape
    tm = _pick_tile(n, (1024, 512, 384, 256, 128))
    rows = lambda i: (i, 0)
    in_specs = [pl.BlockSpec((tm, d), rows)]
    in_specs += [pl.BlockSpec((tm, a.shape[1]), rows) for a in acts]
    in_specs += [_const_spec(w.shape) for w in wts] + [_const_spec(g.shape)]
    return pl.pallas_call(
        functools.partial(_post_kernel, n_in=len(acts)), grid=(n // tm,), in_specs=in_specs,
        out_specs=pl.BlockSpec((tm, d), rows), out_shape=jax.ShapeDtypeStruct((n, d), F32),
        compiler_params=_cparams("parallel"), name="mixer_out",
    )(h2, *acts, *wts, g)


def _mlp_kernel(h_ref, g1_ref, wup_ref, wdn_ref, g2_ref, o_ref, *, ff_chunk):
    x = h_ref[...]
    hn = _rms(x, g1_ref[...]).astype(BF16)
    d_ff = wup_ref.shape[1]
    y = None
    for c0 in range(0, d_ff, ff_chunk):
        u = jnp.maximum(_dot(hn, wup_ref[:, c0:c0 + ff_chunk]), 0.0)
        part = _dot((u * u).astype(BF16), wdn_ref[c0:c0 + ff_chunk, :])
        y = part if y is None else y + part
    o_ref[...] = x + _rms(y, g2_ref[...])


def _mlp(h2, g1, wup, wdn, g2):
    n, d = h2.shape
    tm = _pick_tile(n, (512, 256, 128))
    rows = lambda i: (i, 0)
    in_specs = [pl.BlockSpec((tm, d), rows), _const_spec(g1.shape), _const_spec(wup.shape),
                _const_spec(wdn.shape), _const_spec(g2.shape)]
    return pl.pallas_call(
        functools.partial(_mlp_kernel, ff_chunk=1024), grid=(n // tm,), in_specs=in_specs,
        out_specs=pl.BlockSpec((tm, d), rows), out_shape=jax.ShapeDtypeStruct((n, d), F32),
        compiler_params=_cparams("parallel"), name="mlp",
    )(h2, g1, wup, wdn, g2)


def _mlp_last(h3, g1, wup, wdn, g2, lead):
    b, tp, d = h3.shape
    seq = tp - lead
    tm = _pick_tile(seq, (512, 256, 128))
    in_specs = [pl.BlockSpec((pl.Element(tm), pl.Element(d)), lambda bi, i: (pl.multiple_of(bi * tp + lead + i * tm, CHUNK), 0)),
                _const_spec(g1.shape), _const_spec(wup.shape), _const_spec(wdn.shape), _const_spec(g2.shape)]
    return pl.pallas_call(
        functools.partial(_mlp_kernel, ff_chunk=1024), grid=(b, seq // tm), in_specs=in_specs,
        out_specs=pl.BlockSpec((None, tm, d), lambda bi, i: (bi, i, 0)),
        out_shape=jax.ShapeDtypeStruct((b, seq, d), F32),
        compiler_params=_cparams("parallel", "parallel"), name="mlp_last",
    )(h3.reshape(b * tp, d), g1, wup, wdn, g2)


SEG_LEN = CHUNK // SUBLANES


def _interleave_matrix():
    p = np.zeros((CHUNK, CHUNK), np.float32)
    for j in range(SEG_LEN):
        for s in range(SUBLANES):
            p[SUBLANES * j + s, SEG_LEN * s + j] = 1.0
    return p


def _reorder_rows(perm, x):
    tiles = [_dot(perm, x[t0:t0 + CHUNK]).astype(BF16) for t0 in range(0, x.shape[0], CHUNK)]
    return tiles[0] if len(tiles) == 1 else jnp.concatenate(tiles, axis=0)


def _causal_conv_interleaved(pre, tail_scr, w_ref, b_ref, cols):
    k = w_ref.shape[0]
    halo = (k - 1) * SUBLANES
    width = pre.shape[1]
    first_sub = lax.broadcasted_iota(jnp.int32, (k - 1, SUBLANES, width), 1) == 0
    prev_last = tail_scr[:, cols]
    outs = []
    for t0 in range(0, pre.shape[0], CHUNK):
        tile = pre[t0:t0 + CHUNK]
        last = tile[CHUNK - halo:]
        fix = jnp.where(first_sub, pltpu.roll(prev_last.reshape(k - 1, SUBLANES, width), 1, 1),
                        pltpu.roll(last.reshape(k - 1, SUBLANES, width), 1, 1)).reshape(halo, width)
        out = b_ref[:, cols]
        for i in range(k):
            shift = k - 1 - i
            if shift == 0:
                xs = tile
            else:
                xs = jnp.concatenate([fix[halo - shift * SUBLANES:], tile[:CHUNK - shift * SUBLANES]], axis=0)
            out = out + xs * w_ref[i:i + 1, cols]
        outs.append(out)
        prev_last = last
    tail_scr[:, cols] = prev_last
    return jnp.concatenate(outs, axis=0)


def _proj_odd_kernel(h_ref, g_ref, wx_ref, wy_ref, cw_ref, cb_ref, perm_ref, x_ref, gate_ref, tail_scr):
    @pl.when(pl.program_id(0) == 0)
    def _():
        tail_scr[...] = jnp.zeros_like(tail_scr)

    hn = _reorder_rows(perm_ref[...], _rms(h_ref[...], g_ref[...]).astype(BF16))
    for c0 in range(0, LRU_WIDTH, LRU_CONV_COLS):
        cols = slice(c0, c0 + LRU_CONV_COLS)
        x_ref[:, cols] = _causal_conv_interleaved(_dot(hn, wx_ref[:, cols]), tail_scr, cw_ref, cb_ref,
                                                  cols).astype(BF16)
        y = _dot(hn, wy_ref[:, cols])
        gelu = 0.5 * y * (1.0 + jnp.tanh(np.sqrt(2.0 / np.pi).astype(np.float32) * (y + 0.044715 * (y * y * y))))
        gate_ref[:, cols] = gelu.astype(BF16)


def _proj_odd(h2, g, wx, wy, cw, cb, perm):
    n, d = h2.shape
    tm = _pick_tile(n, (1024, 512, 384, 256, 128))
    rows = lambda i: (i, 0)
    w = wx.shape[1]
    return pl.pallas_call(
        _proj_odd_kernel, grid=(n // tm,),
        in_specs=[pl.BlockSpec((tm, d), rows)] + [_const_spec(a.shape) for a in (g, wx, wy, cw, cb, perm)],
        out_specs=(pl.BlockSpec((tm, w), rows), pl.BlockSpec((tm, w), rows)),
        out_shape=(jax.ShapeDtypeStruct((n, w), BF16), jax.ShapeDtypeStruct((n, w), BF16)),
        scratch_shapes=[pltpu.VMEM(((cw.shape[0] - 1) * SUBLANES, w), F32)],
        compiler_params=_cparams("arbitrary"), name="proj_odd",
    )(h2, g, wx, wy, cw, cb, perm)


LRU_BATCH_BLOCK = 4


def _rglru_kernel(x_ref, gate_ref, wa_ref, ba_ref, wi_ref, bi_ref, lam_ref, unperm_ref, o_ref, h_scr):
    t = pl.program_id(1)

    @pl.when(t == 0)
    def _():
        h_scr[...] = jnp.zeros_like(h_scr)

    for bb in range(x_ref.shape[0]):
        _rglru_tile(t, x_ref.at[bb], gate_ref.at[bb], wa_ref, ba_ref, wi_ref, bi_ref, lam_ref, unperm_ref,
                    o_ref.at[bb], h_scr.at[bb])


def _rglru_tile(t, x_ref, gate_ref, wa_ref, ba_ref, wi_ref, bi_ref, lam_ref, unperm_ref, o_ref, h_scr):
    L = x_ref.shape[0]
    xr = x_ref[...].astype(F32)
    rs, gs = [], []
    for nb in range(LRU_BLOCKS):
        sl = slice(nb * LRU_BLOCK, (nb + 1) * LRU_BLOCK)
        xb = x_ref[:, sl]
        rs.append(_dot(xb, wa_ref[nb]))
        gs.append(_dot(xb, wi_ref[nb]))
    r = _sigmoid(jnp.concatenate(rs, axis=1) + ba_ref[...])
    ig = _sigmoid(jnp.concatenate(gs, axis=1) + bi_ref[...])
    log_a = (-LRU_C) * r * _softplus(-lam_ref[...])
    a = jnp.exp(log_a)
    row = lax.broadcasted_iota(jnp.int32, (L, 1), 0)
    valid = (t * L + SEG_LEN * (row % SUBLANES) + row // SUBLANES) >= PAD
    om = 1.0 - a * a
    root = jnp.where(om > 0.0, om * lax.rsqrt(om), 0.0)
    u = jnp.where(valid, root * (ig * xr), 0.0)

    slab = lambda v, j: v[j * SUBLANES:(j + 1) * SUBLANES, :]
    h_loc = [slab(u, 0)]
    prod = [slab(a, 0)]
    for j in range(1, SEG_LEN):
        h_loc.append(slab(a, j) * h_loc[-1] + slab(u, j))
        prod.append(slab(a, j) * prod[-1])
    carry_in = h_scr[0:1, :]
    carries = []
    for s in range(SUBLANES):
        carries.append(carry_in)
        carry_in = h_loc[-1][s:s + 1, :] + prod[-1][s:s + 1, :] * carry_in
    h_scr[...] = jnp.broadcast_to(carry_in, h_scr.shape)
    carry = jnp.concatenate(carries, axis=0)
    hs = jnp.concatenate([h_loc[j] + prod[j] * carry for j in range(SEG_LEN)], axis=0)
    o_ref[...] = _reorder_rows(unperm_ref[...], (hs * gate_ref[...].astype(F32)).astype(BF16))


def _rglru(x3, gate3, wa, ba, wi, bi, lam, unperm):
    b, tp, w = x3.shape
    tt = CHUNK
    gb = _pick_tile(b, (LRU_BATCH_BLOCK, 1))
    row3 = lambda bi_, ti: (bi_, ti, 0)
    in_specs = [pl.BlockSpec((gb, tt, w), row3), pl.BlockSpec((gb, tt, w), row3)]
    in_specs += [_const_spec(a.shape) for a in (wa, ba, wi, bi, lam, unperm)]
    return pl.pallas_call(
        _rglru_kernel, grid=(b // gb, tp // tt), in_specs=in_specs,
        out_specs=pl.BlockSpec((gb, tt, w), row3),
        out_shape=jax.ShapeDtypeStruct((b, tp, w), BF16),
        scratch_shapes=[pltpu.VMEM((gb, SUBLANES, w), F32)],
        compiler_params=_cparams("parallel", "arbitrary"), name="rglru",
    )(x3, gate3, wa, ba, wi, bi, lam, unperm)


def _rope_tables(tp):
    f32 = np.float32
    inv = f32(ROPE_BASE) ** (-np.arange(0, MLA_ROPE, 2, dtype=f32) / f32(MLA_ROPE))
    pos = np.maximum(np.arange(tp, dtype=f32) - f32(PAD), f32(0))
    ang = (pos[:, None] * inv[None, :]).astype(f32)
    cos, sin = np.cos(ang).astype(f32), np.sin(ang).astype(f32)
    zeros_lo = np.zeros((tp, MLA_NOPE), f32)
    zeros_hi = np.zeros((tp, LANES - MLA_NOPE - MLA_ROPE), f32)
    ck = np.concatenate([zeros_lo, cos, cos, zeros_hi], axis=1)
    sk = np.concatenate([zeros_lo, sin, sin, zeros_hi], axis=1)
    scale = f32((MLA_NOPE + MLA_ROPE) ** -0.5 * np.log2(np.e))
    cq = np.concatenate([np.ones((tp, MLA_NOPE), f32), cos, cos, zeros_hi], axis=1) * scale
    sq = sk * scale
    return tuple(jnp.asarray(t) for t in (cq, sq, ck, sk))


def _even_weights(w_in, w_q_up, w_kv_up):
    offs = np.cumsum((SSD_D_INNER, SSD_CONV_CH, SSD_HEADS, MLA_Q_RANK, MLA_KV_RANK, MLA_ROPE))
    wz = w_in[:, :offs[0]]
    wxbc = w_in[:, offs[0]:offs[1]]
    wdt = w_in[:, offs[1]:offs[2]]
    wcq = w_in[:, offs[2]:offs[3]]
    wckv = w_in[:, offs[3]:offs[4]]
    wkr = w_in[:, offs[4]:offs[5]]
    hi_pad = LANES - MLA_NOPE - MLA_ROPE
    wdt_p = jnp.pad(wdt, ((0, 0), (0, LANES - SSD_HEADS)))
    wkr_p = jnp.pad(wkr, ((0, 0), (MLA_NOPE, hi_pad)))
    wq = w_q_up.reshape(MLA_Q_RANK, MLA_HEADS, MLA_NOPE + MLA_ROPE)
    wq_p = jnp.pad(wq, ((0, 0), (0, 0), (0, hi_pad))).reshape(MLA_Q_RANK, MLA_HEADS * LANES)
    wkv = w_kv_up.reshape(MLA_KV_RANK, MLA_HEADS, MLA_NOPE + MLA_V)
    wk_p = jnp.pad(wkv[..., :MLA_NOPE], ((0, 0), (0, 0), (0, LANES - MLA_NOPE))).reshape(
        MLA_KV_RANK, MLA_HEADS * LANES)
    wv = wkv[..., MLA_NOPE:].reshape(MLA_KV_RANK, MLA_HEADS * MLA_V)
    bf = lambda a: a.astype(BF16)
    return dict(wz=bf(wz), wxbc=bf(wxbc), wdt=bf(wdt_p), wkr_p=bf(wkr_p), wcq=bf(wcq),
                wckv=bf(wckv), wq_p=bf(wq_p), wk_p=bf(wk_p), wv=bf(wv))


def _row(a):
    return a.reshape(1, -1).astype(F32)


def kernel(x, meta_tokens, mix_pre_g, mix_post_g, mlp_pre_g, mlp_post_g, w_up, w_down, w_in, ssd_conv_w,
           ssd_conv_b, ssd_dt_bias, ssd_a_log, ssd_d, ssd_norm_g, mla_q_norm_g, mla_w_q_up, mla_kv_norm_g,
           mla_w_kv_up, w_out_ab, rg_w_x, rg_w_y, rg_conv_w, rg_conv_b, rg_w_a, rg_b_a, rg_w_i, rg_b_i,
           rg_lambda, rg_w_out):
    b, seq, d = x.shape
    depth = mix_pre_g.shape[0]
    tp = PAD + N_META + seq
    n = b * tp
    meta = jnp.broadcast_to(meta_tokens[None].astype(x.dtype), (b, N_META, d))
    h = jnp.concatenate([jnp.zeros((b, PAD, d), x.dtype), meta, x], axis=1).reshape(n, d)
    tables = _rope_tables(tp)

    for layer in range(depth):
        if layer % 2 == 0:
            e = layer // 2
            w = _even_weights(w_in[e], mla_w_q_up[e], mla_w_kv_up[e])
            wts = (w["wz"], w["wxbc"], w["wdt"], w["wkr_p"], w["wcq"], _row(mla_q_norm_g[e]),
                   w["wckv"], _row(mla_kv_norm_g[e]), w["wq_p"], w["wk_p"], w["wv"],
                   ssd_conv_w[e].astype(F32), _row(ssd_conv_b[e]))
            z, xbc, dt, q, k, v = _proj_even(h.reshape(b, tp, d), _row(mix_pre_g[layer]), wts, tables)
            lane_pad = lambda a: jnp.pad(_row(a), ((0, 0), (0, LANES - SSD_HEADS)))
            y_ssd = _ssd(xbc.reshape(b, tp, SSD_CONV_CH), dt.reshape(b, tp, LANES), z.reshape(b, tp, SSD_D_INNER),
                         lane_pad(ssd_dt_bias[e]), lane_pad(ssd_a_log[e]),
                         _row(jnp.repeat(ssd_d[e], SSD_HEAD_DIM)), _row(ssd_norm_g[e]))
            y_att = _attention(q, k, v, b, tp)
            wo = w_out_ab[e].astype(BF16)
            h = _post(h, (y_ssd.reshape(n, SSD_D_INNER), y_att), (wo[:SSD_D_INNER], wo[SSD_D_INNER:]),
                      _row(mix_post_g[layer]))
        else:
            o = layer // 2
            perm = _interleave_matrix()
            xr, gate = _proj_odd(h, _row(mix_pre_g[layer]), rg_w_x[o].astype(BF16), rg_w_y[o].astype(BF16),
                                 rg_conv_w[o].astype(F32), _row(rg_conv_b[o]), jnp.asarray(perm, BF16))
            y = _rglru(xr.reshape(b, tp, LRU_WIDTH), gate.reshape(b, tp, LRU_WIDTH), rg_w_a[o].astype(BF16),
                       _row(rg_b_a[o]), rg_w_i[o].astype(BF16), _row(rg_b_i[o]), _row(rg_lambda[o]),
                       jnp.asarray(perm.T, BF16))
            h = _post(h, (y.reshape(n, LRU_WIDTH),), (rg_w_out[o].astype(BF16),), _row(mix_post_g[layer]))
        mlp_args = (_row(mlp_pre_g[layer]), w_up[layer].astype(BF16), w_down[layer].astype(BF16),
                    _row(mlp_post_g[layer]))
        if layer == depth - 1:
            return _mlp_last(h.reshape(b, tp, d), *mlp_args, PAD + N_META)
        h = _mlp(h, *mlp_args)
```

```python
import functools

import jax
import jax.numpy as jnp
import numpy as np
from jax import lax
from jax.experimental import pallas as pl
from jax.experimental.pallas import tpu as pltpu

F32 = jnp.float32
BF16 = jnp.bfloat16

N_META = 16
CHUNK = 128
PAD = CHUNK - N_META
EPS = 1e-6
SSD_HEADS = 16
SSD_HEAD_DIM = 64
SSD_D_INNER = SSD_HEADS * SSD_HEAD_DIM
SSD_GROUPS = 2
SSD_STATE = 128
SSD_BC = SSD_GROUPS * SSD_STATE
SSD_CONV_CH = SSD_D_INNER + 2 * SSD_BC
MLA_HEADS = 16
MLA_NOPE = 64
MLA_ROPE = 32
MLA_V = 64
MLA_Q_RANK = 384
MLA_KV_RANK = 256
ROPE_BASE = 10000.0
LRU_BLOCKS = 10
LRU_BLOCK = 128
LRU_WIDTH = LRU_BLOCKS * LRU_BLOCK
LRU_C = 8.0
LRU_CONV_COLS = 256
LANES = 128
SUBLANES = 8
VMEM_LIMIT_BYTES = 56 * 1024 * 1024
NEG_BIG = -1e30


def _cparams(*sem):
    return pltpu.CompilerParams(dimension_semantics=sem, vmem_limit_bytes=VMEM_LIMIT_BYTES)


def _pick_tile(n, candidates):
    for c in candidates:
        if n % c == 0:
            return c
    raise ValueError(f"no tile in {candidates} divides {n}")


def _const_spec(shape):
    nd = len(shape)
    return pl.BlockSpec(shape, lambda *_: (0,) * nd)


def _rms(x, g):
    ms = jnp.mean(x * x, axis=-1, keepdims=True)
    return x * lax.rsqrt(ms + EPS) * g


def _sigmoid(x):
    return 1.0 / (1.0 + jnp.exp(-x))


def _softplus(x):
    return jnp.maximum(x, 0.0) + jnp.log(1.0 + jnp.exp(-jnp.abs(x)))


def _dot(a, b):
    return jnp.dot(a, b, preferred_element_type=F32)


def _dot_nt(a, b):
    return lax.dot_general(a, b, (((1,), (1,)), ((), ())), preferred_element_type=F32)


def _shift_rows(cur, tail, k):
    if k == 0:
        return cur
    rolled = pltpu.roll(cur, k, 0)
    row = lax.broadcasted_iota(jnp.int32, tail.shape, 0)
    top = jnp.where(row < k, pltpu.roll(tail, k, 0), rolled[0:SUBLANES])
    return jnp.concatenate([top, rolled[SUBLANES:]], axis=0)


def _causal_conv(cur, tail_scr, w_ref, b_ref, cols):
    k = w_ref.shape[0]
    tail = tail_scr[:, cols]
    out = b_ref[:, cols]
    for i in range(k):
        out = out + _shift_rows(cur, tail, k - 1 - i) * w_ref[i:i + 1, cols]
    tail_scr[:, cols] = cur[cur.shape[0] - SUBLANES:]
    return out


CONV_COLS = 256
Z_COLS = 256


def _rotate_half(x):
    half = MLA_ROPE // 2
    lane = lax.broadcasted_iota(jnp.int32, x.shape, 1) % LANES
    from_right = pltpu.roll(x, x.shape[1] - half, 1)
    from_left = pltpu.roll(x, half, 1)
    return jnp.where(lane < MLA_NOPE + half, -from_right, from_left)


def _proj_even_kernel(h_ref, g_ref, wz_ref, wxbc_ref, wdt_ref, wkrp_ref, wcq_ref, qg_ref,
                      wckv_ref, kvg_ref, wqp_ref, wkp_ref, wv_ref, cw_ref, cb_ref,
                      cq_ref, sq_ref, ck_ref, sk_ref,
                      z_ref, xbc_ref, dt_ref, q_ref, k_ref, v_ref, tail_scr):
    @pl.when(pl.program_id(1) == 0)
    def _():
        tail_scr[...] = jnp.zeros_like(tail_scr)

    hn = _rms(h_ref[...], g_ref[...]).astype(BF16)

    def conv_slab(c0):
        cols = slice(c0, c0 + CONV_COLS)
        conv = _causal_conv(_dot(hn, wxbc_ref[:, cols]), tail_scr, cw_ref, cb_ref, cols)
        xbc_ref[:, cols] = (conv * _sigmoid(conv)).astype(BF16)

    def z_slab(c0):
        cols = slice(c0, c0 + Z_COLS)
        z_ref[:, cols] = _dot(hn, wz_ref[:, cols]).astype(BF16)

    cqn = _rms(_dot(hn, wcq_ref[...]), qg_ref[...]).astype(BF16)
    ckvn = _rms(_dot(hn, wckv_ref[...]), kvg_ref[...]).astype(BF16)
    kr_raw = _dot(hn, wkrp_ref[...])
    kr = kr_raw * ck_ref[...] + _rotate_half(kr_raw) * sk_ref[...]
    dt_ref[...] = _dot(hn, wdt_ref[...])
    cq_t = jnp.concatenate([cq_ref[...]] * 2, axis=1)
    sq_t = jnp.concatenate([sq_ref[...]] * 2, axis=1)
    kr2 = jnp.concatenate([kr] * 2, axis=1)

    def head_pair(hd):
        sl = slice(hd * LANES, (hd + 2) * LANES)
        q_raw = _dot(cqn, wqp_ref[:, sl])
        qh = (q_raw * cq_t + _rotate_half(q_raw) * sq_t).astype(BF16)
        kh = (_dot(ckvn, wkp_ref[:, sl]) + kr2).astype(BF16)
        for j in range(2):
            q_ref[hd + j] = qh[:, j * LANES:(j + 1) * LANES]
            k_ref[hd + j] = kh[:, j * LANES:(j + 1) * LANES]

    def v_proj(_):
        v_ref[...] = _dot(ckvn, wv_ref[...]).astype(BF16)

    convs = [(conv_slab, c0) for c0 in range(0, SSD_CONV_CH, CONV_COLS)]
    pairs = [(head_pair, hd) for hd in range(0, MLA_HEADS, 2)]
    heavy = []
    while convs or pairs:
        heavy += convs[:1] + pairs[:1]
        convs, pairs = convs[1:], pairs[1:]
    light = [(v_proj, None)] + [(z_slab, c0) for c0 in range(0, SSD_D_INNER, Z_COLS)]
    order = []
    for i, item in enumerate(heavy):
        order.append(item)
        if (i + 1) % 3 == 0 and len(light) > 1:
            order.append(light.pop(0))
    order += light
    for fn, arg in order:
        fn(arg)


def _proj_even(h3, g, wts, tables):
    b, tp, d = h3.shape
    tm = _pick_tile(tp, (384, 128))
    nt = tp // tm
    n = b * tp
    row3 = lambda bi, i: (bi, i, 0)
    flat = lambda bi, i: (bi * nt + i, 0)
    tab = lambda bi, i: (i, 0)
    in_specs = [pl.BlockSpec((None, tm, d), row3), _const_spec(g.shape)]
    in_specs += [_const_spec(w.shape) for w in wts]
    in_specs += [pl.BlockSpec((tm, LANES), tab) for _ in tables]
    out_shape = (
        jax.ShapeDtypeStruct((n, SSD_D_INNER), BF16),
        jax.ShapeDtypeStruct((n, SSD_CONV_CH), BF16),
        jax.ShapeDtypeStruct((n, LANES), F32),
        jax.ShapeDtypeStruct((MLA_HEADS, n, LANES), BF16),
        jax.ShapeDtypeStruct((MLA_HEADS, n, LANES), BF16),
        jax.ShapeDtypeStruct((n, MLA_HEADS * MLA_V), BF16),
    )
    head3 = lambda bi, i: (0, bi * nt + i, 0)
    out_specs = (
        pl.BlockSpec((tm, SSD_D_INNER), flat),
        pl.BlockSpec((tm, SSD_CONV_CH), flat),
        pl.BlockSpec((tm, LANES), flat),
        pl.BlockSpec((MLA_HEADS, tm, LANES), head3),
        pl.BlockSpec((MLA_HEADS, tm, LANES), head3),
        pl.BlockSpec((tm, MLA_HEADS * MLA_V), flat),
    )
    return pl.pallas_call(
        _proj_even_kernel, grid=(b, nt), in_specs=in_specs, out_specs=out_specs, out_shape=out_shape,
        scratch_shapes=[pltpu.VMEM((SUBLANES, SSD_CONV_CH), F32)],
        compiler_params=_cparams("parallel", "arbitrary"), name="proj_even",
    )(h3, g, *wts, *tables)


def _cumsum_rows(x):
    n = x.shape[0]
    row = lax.broadcasted_iota(jnp.int32, x.shape, 0)
    d = 1
    while d < n:
        x = x + jnp.where(row >= d, pltpu.roll(x, d, 0), 0.0)
        d *= 2
    return x


SSD_BATCH_BLOCK = 4


def _ssd_chunk(c, xbc, dt_raw, zf, dtb, alog, dsk, ng, state_ref, y_scr):
    L = CHUNK
    heads_per_group = SSD_HEADS // SSD_GROUPS
    row = lax.broadcasted_iota(jnp.int32, (L, 1), 0)
    valid = (c * L + row) >= PAD
    act = xbc.astype(F32)
    xs = act[:, :SSD_D_INNER]
    bm = act[:, SSD_D_INNER:SSD_D_INNER + SSD_BC]
    cm = act[:, SSD_D_INNER + SSD_BC:]

    dtv = jnp.where(valid, _softplus(dt_raw + dtb), 0.0)
    a_cs = _cumsum_rows(dtv * (-jnp.exp(alog)))
    a_cs_t = a_cs.T
    dt_t = dtv.T
    tot = a_cs[L - 1:L, :]
    w_t = jnp.exp(a_cs_t[:, L - 1:L] - a_cs_t) * dt_t
    shifted_t = a_cs_t - jnp.log(dt_t)
    cd = jnp.exp(tot)
    causal = lax.broadcasted_iota(jnp.int32, (L, L), 0) >= lax.broadcasted_iota(jnp.int32, (L, L), 1)
    lo = lax.broadcasted_iota(jnp.int32, (L, LANES), 1) < SSD_HEAD_DIM

    for g in range(SSD_GROUPS):
        cc = cm[:, g * SSD_STATE:(g + 1) * SSD_STATE]
        bc = bm[:, g * SSD_STATE:(g + 1) * SSD_STATE]
        bc_t = bc.T
        cb = _dot_nt(cc.astype(BF16), bc.astype(BF16))
        for hp in range(heads_per_group // 2):
            pair = g * (heads_per_group // 2) + hp
            sl = slice(pair * LANES, (pair + 1) * LANES)
            xs_pair = xs[:, sl]
            st_pair = state_ref[:, sl]
            xs_b = xs_pair.astype(BF16)
            rhs = jnp.concatenate([xs_b, st_pair.astype(BF16)], axis=0)
            ys, sts, cds = [], [], []
            for hh in range(2):
                h = 2 * pair + hh
                a_col = jnp.broadcast_to(a_cs[:, h:h + 1], (L, L))
                m = cb * jnp.exp(jnp.where(causal, a_col - shifted_t[h:h + 1, :], -jnp.inf))
                lhs = jnp.concatenate([m, cc * jnp.exp(a_col)], axis=1).astype(BF16)
                ys.append(_dot(lhs, rhs))
                sts.append(_dot((bc_t * w_t[h:h + 1, :]).astype(BF16), xs_b))
                cds.append(cd[:, h:h + 1])
            y_scr[:, sl] = jnp.where(lo, ys[0], ys[1]) + dsk[:, sl] * xs_pair
            state_ref[:, sl] = jnp.where(lo, cds[0], cds[1]) * st_pair + jnp.where(lo, sts[0], sts[1])

    return _rms(y_scr[...] * (zf * _sigmoid(zf)), ng).astype(BF16)


def _ssd_kernel(xbc_ref, dt_ref, z_ref, dtb_ref, alog_ref, dsk_ref, ng_ref, y_ref, state_scr, y_scr):
    c = pl.program_id(1)

    @pl.when(c == 0)
    def _():
        state_scr[...] = jnp.zeros_like(state_scr)

    for bb in range(xbc_ref.shape[0]):
        y_ref[bb] = _ssd_chunk(c, xbc_ref[bb], dt_ref[bb], z_ref[bb].astype(F32), dtb_ref[...], alog_ref[...],
                               dsk_ref[...], ng_ref[...], state_scr.at[bb], y_scr.at[bb])


def _ssd(xbc3, dt3, z3, dtb, alog, dsk, ng):
    b, tp, _ = xbc3.shape
    nc = tp // CHUNK
    gb = _pick_tile(b, (SSD_BATCH_BLOCK, 1))
    row3 = lambda bi, ci: (bi, ci, 0)
    in_specs = [
        pl.BlockSpec((gb, CHUNK, SSD_CONV_CH), row3),
        pl.BlockSpec((gb, CHUNK, LANES), row3),
        pl.BlockSpec((gb, CHUNK, SSD_D_INNER), row3),
    ] + [_const_spec(a.shape) for a in (dtb, alog, dsk, ng)]
    return pl.pallas_call(
        _ssd_kernel, grid=(b // gb, nc), in_specs=in_specs,
        out_specs=pl.BlockSpec((gb, CHUNK, SSD_D_INNER), row3),
        out_shape=jax.ShapeDtypeStruct((b, tp, SSD_D_INNER), BF16),
        scratch_shapes=[pltpu.VMEM((gb, SSD_STATE, SSD_D_INNER), F32),
                        pltpu.VMEM((gb, CHUNK, SSD_D_INNER), F32)],
        compiler_params=_cparams("parallel", "arbitrary"), name="ssd",
    )(xbc3, dt3, z3, dtb, alog, dsk, ng)


ATTN_KV_BLOCK = 256
ATTN_V_ROWS = MLA_V + 16
PLAIN_UNROLLS = (4, 2, 1)
ATTN_COL_SLAB = 256


def _attn_kernel(q_ref, k_ref, v_ref, o_ref, vt_scr, s_scr, bmax_scr, m_scr, acc_scr, *, tq, nq):
    kb = ATTN_KV_BLOCK
    qi = pl.program_id(2)
    q0 = qi * tq
    end = q0 + tq
    n_int = q0 // kb
    n_blocks = (end + kb - 1) // kb
    last_off = end - kb

    @pl.when(qi == 0)
    def _():
        ones = jnp.ones((ATTN_V_ROWS - MLA_V, CHUNK), BF16)
        for c in range(vt_scr.shape[0]):
            vt = v_ref[c * CHUNK:(c + 1) * CHUNK, :].astype(F32).T.astype(BF16)
            for hh in range(2):
                vt_scr[c, hh, :MLA_V, :] = vt[hh * MLA_V:(hh + 1) * MLA_V, :]
                vt_scr[c, hh, MLA_V:, :] = ones

    m_scr[...] = jnp.full_like(m_scr, NEG_BIG)
    acc_scr[...] = jnp.zeros_like(acc_scr)

    def offset(bk):
        return pl.multiple_of(jnp.minimum(bk * kb, last_off), CHUNK)

    levels = tuple(range(0, tq - kb + 1, kb))

    def level(bk):
        return jnp.minimum(jnp.maximum(offset(bk) - q0, 0) // kb, len(levels) - 1) * kb

    def col_slabs(cs):
        cuts = sorted({cs} | {c for c in range(tq, cs, -ATTN_COL_SLAB)})
        if len(cuts) > 2 and cuts[1] - cuts[0] < ATTN_COL_SLAB // 2:
            del cuts[1]
        return [slice(a, b) for a, b in zip(cuts[:-1], cuts[1:])]

    def qk(hh, bk, masked, cols):
        off = offset(bk)
        s = _dot_nt(k_ref[hh, pl.ds(off, kb), :], q_ref[hh, cols, :])
        if masked:
            key = lax.broadcasted_iota(jnp.int32, s.shape, 0)
            keep = (key - lax.broadcasted_iota(jnp.int32, s.shape, 1)) <= (q0 + cols.start - off)
            first_key = jnp.where(bk == 0, PAD, bk * kb - off)
            s = jnp.where(keep, s, NEG_BIG)
            s = jnp.where(key >= first_key, s, NEG_BIG)
        s_scr[hh, :, cols] = s
        bmax_scr[hh, :, cols] = jnp.broadcast_to(jnp.max(s, axis=0, keepdims=True), (SUBLANES, s.shape[1]))

    def softmax_pv(hh, bk, cols):
        m_prev = m_scr[hh, 0:1, cols]
        m_new = jnp.maximum(m_prev, bmax_scr[hh, 0:1, cols])
        alpha = jnp.exp2(m_prev - m_new)
        m_scr[hh, :, cols] = jnp.broadcast_to(m_new, (SUBLANES, m_new.shape[1]))
        p = jnp.exp2(s_scr[hh, :, cols] - m_new).astype(BF16)
        c0 = offset(bk) // CHUNK
        vt = jnp.concatenate([vt_scr[c0 + c, hh] for c in range(kb // CHUNK)], axis=1)
        acc_scr[hh, :, cols] = alpha * acc_scr[hh, :, cols] + _dot(vt, p)

    def step(bk, masked, cs_prev, cs):
        prev_slabs, cur_slabs = col_slabs(cs_prev), col_slabs(cs)
        for j in range(max(len(prev_slabs), len(cur_slabs))):
            if j < len(cur_slabs):
                qk(0, bk, masked, cur_slabs[j])
            if j < len(prev_slabs):
                softmax_pv(1, bk - 1, prev_slabs[j])
        for j, cols in enumerate(cur_slabs):
            qk(1, bk, masked, cols)
            softmax_pv(0, bk, cols)

    def plain_run(first, count, unroll):
        def trip(i, carry):
            for u in range(unroll):
                step(first + unroll * i + u, False, 0, 0)
            return carry

        trips = count // unroll
        lax.fori_loop(0, trips, trip, 0)
        return first + unroll * trips, count - unroll * trips

    def diagonal_run(residue, skip):
        n_diag = -(-(residue + tq) // kb)
        lvls = [min(max(min(i * kb - residue, tq - kb), 0) // kb, len(levels) - 1) * kb for i in range(n_diag)]
        for i in range(skip, n_diag):
            step(n_int + i, True, lvls[i - 1] if i > 0 else 0, lvls[i])

    for cols in col_slabs(0):
        qk(0, 0, True, cols)
    for cols in col_slabs(0):
        qk(1, 0, True, cols)
        softmax_pv(0, 0, cols)
    first, left = 1, jnp.maximum(n_int - 1, 0)
    for unroll in PLAIN_UNROLLS:
        first, left = plain_run(first, left, unroll)
    @pl.when(qi == 0)
    def _():
        diagonal_run(0, 1)

    for residue in sorted({(i * tq) % kb for i in range(1, nq)}):
        @pl.when(jnp.logical_and(qi > 0, q0 % kb == residue))
        def _():
            diagonal_run(residue, 0)

    lvl_last = level(n_blocks - 1)
    for a in levels:
        @pl.when(lvl_last == a)
        def _():
            for cols in col_slabs(a):
                softmax_pv(1, n_blocks - 1, cols)

    out_t = jnp.concatenate([acc_scr[hh, :MLA_V, :] / acc_scr[hh, MLA_V:MLA_V + 1, :] for hh in range(2)],
                            axis=0)
    qpos = q0 + lax.broadcasted_iota(jnp.int32, (1, tq), 1)
    out_t = jnp.where(qpos >= PAD, out_t, 0.0)
    for c in range(tq // CHUNK):
        o_ref[c * CHUNK:(c + 1) * CHUNK, :] = out_t[:, c * CHUNK:(c + 1) * CHUNK].T.astype(BF16)


def _attention(q, k, v, b, tp):
    tq = _pick_tile(tp, (1408, 384, 256))
    nq = tp // tq
    n = b * tp
    in_specs = [
        pl.BlockSpec((2, tq, LANES), lambda bi, hp, qi: (hp, bi * nq + qi, 0)),
        pl.BlockSpec((2, tp, LANES), lambda bi, hp, qi: (hp, bi, 0)),
        pl.BlockSpec((tp, LANES), lambda bi, hp, qi: (bi, hp)),
    ]
    return pl.pallas_call(
        functools.partial(_attn_kernel, tq=tq, nq=nq), grid=(b, MLA_HEADS // 2, nq), in_specs=in_specs,
        out_specs=pl.BlockSpec((tq, LANES), lambda bi, hp, qi: (bi * nq + qi, hp)),
        out_shape=jax.ShapeDtypeStruct((n, MLA_HEADS * MLA_V), BF16),
        scratch_shapes=[pltpu.VMEM((tp // CHUNK, 2, ATTN_V_ROWS, CHUNK), BF16),
                        pltpu.VMEM((2, ATTN_KV_BLOCK, tq), F32), pltpu.VMEM((2, SUBLANES, tq), F32),
                        pltpu.VMEM((2, SUBLANES, tq), F32), pltpu.VMEM((2, ATTN_V_ROWS, tq), F32)],
        compiler_params=_cparams("parallel", "parallel", "arbitrary"), name="attention",
    )(q, k, v)


POST_ROW_BLOCK = 256


def _post_kernel(*refs, n_in):
    h_ref = refs[0]
    a_refs = refs[1:1 + n_in]
    w_refs = refs[1 + n_in:1 + 2 * n_in]
    g_ref = refs[1 + 2 * n_in]
    o_ref = refs[2 + 2 * n_in]
    rows_per_block = min(POST_ROW_BLOCK, h_ref.shape[0])
    for r0 in range(0, h_ref.shape[0], rows_per_block):
        rows = slice(r0, r0 + rows_per_block)
        m = _dot(a_refs[0][rows, :], w_refs[0][...])
        for a_ref, w_ref in zip(a_refs[1:], w_refs[1:]):
            m = m + _dot(a_ref[rows, :], w_ref[...])
        o_ref[rows, :] = h_ref[rows, :] + _rms(m, g_ref[...])


def _post(h2, acts, wts, g):
    n, d = h2.shape
    tm = _pick_tile(n, (1024, 512, 384, 256, 128))
    rows = lambda i: (i, 0)
    in_specs = [pl.BlockSpec((tm, d), rows)]
    in_specs += [pl.BlockSpec((tm, a.shape[1]), rows) for a in acts]
    in_specs += [_const_spec(w.shape) for w in wts] + [_const_spec(g.shape)]
    return pl.pallas_call(
        functools.partial(_post_kernel, n_in=len(acts)), grid=(n // tm,), in_specs=in_specs,
        out_specs=pl.BlockSpec((tm, d), rows), out_shape=jax.ShapeDtypeStruct((n, d), F32),
        compiler_params=_cparams("parallel"), name="mixer_out",
    )(h2, *acts, *wts, g)


def _mlp_kernel(h_ref, g1_ref, wup_ref, wdn_ref, g2_ref, o_ref, *, ff_chunk):
    x = h_ref[...]
    hn = _rms(x, g1_ref[...]).astype(BF16)
    d_ff = wup_ref.shape[1]
    y = None
    for c0 in range(0, d_ff, ff_chunk):
        u = jnp.maximum(_dot(hn, wup_ref[:, c0:c0 + ff_chunk]), 0.0)
        part = _dot((u * u).astype(BF16), wdn_ref[c0:c0 + ff_chunk, :])
        y = part if y is None else y + part
    o_ref[...] = x + _rms(y, g2_ref[...])


def _mlp(h2, g1, wup, wdn, g2):
    n, d = h2.shape
    tm = _pick_tile(n, (512, 256, 128))
    rows = lambda i: (i, 0)
    in_specs = [pl.BlockSpec((tm, d), rows), _const_spec(g1.shape), _const_spec(wup.shape),
                _const_spec(wdn.shape), _const_spec(g2.shape)]
    return pl.pallas_call(
        functools.partial(_mlp_kernel, ff_chunk=1024), grid=(n // tm,), in_specs=in_specs,
        out_specs=pl.BlockSpec((tm, d), rows), out_shape=jax.ShapeDtypeStruct((n, d), F32),
        compiler_params=_cparams("parallel"), name="mlp",
    )(h2, g1, wup, wdn, g2)


def _mlp_last(h3, g1, wup, wdn, g2, lead):
    b, tp, d = h3.shape
    seq = tp - lead
    tm = _pick_tile(seq, (512, 256, 128))
    in_specs = [pl.BlockSpec((pl.Element(tm), pl.Element(d)), lambda bi, i: (pl.multiple_of(bi * tp + lead + i * tm, CHUNK), 0)),
                _const_spec(g1.shape), _const_spec(wup.shape), _const_spec(wdn.shape), _const_spec(g2.shape)]
    return pl.pallas_call(
        functools.partial(_mlp_kernel, ff_chunk=1024), grid=(b, seq // tm), in_specs=in_specs,
        out_specs=pl.BlockSpec((None, tm, d), lambda bi, i: (bi, i, 0)),
        out_shape=jax.ShapeDtypeStruct((b, seq, d), F32),
        compiler_params=_cparams("parallel", "parallel"), name="mlp_last",
    )(h3.reshape(b * tp, d), g1, wup, wdn, g2)


SEG_LEN = CHUNK // SUBLANES


def _interleave_matrix():
    p = np.zeros((CHUNK, CHUNK), np.float32)
    for j in range(SEG_LEN):
        for s in range(SUBLANES):
            p[SUBLANES * j + s, SEG_LEN * s + j] = 1.0
    return p


def _reorder_rows(perm, x):
    tiles = [_dot(perm, x[t0:t0 + CHUNK]).astype(BF16) for t0 in range(0, x.shape[0], CHUNK)]
    return tiles[0] if len(tiles) == 1 else jnp.concatenate(tiles, axis=0)


def _causal_conv_interleaved(pre, tail_scr, w_ref, b_ref, cols):
    k = w_ref.shape[0]
    halo = (k - 1) * SUBLANES
    width = pre.shape[1]
    first_sub = lax.broadcasted_iota(jnp.int32, (k - 1, SUBLANES, width), 1) == 0
    prev_last = tail_scr[:, cols]
    outs = []
    for t0 in range(0, pre.shape[0], CHUNK):
        tile = pre[t0:t0 + CHUNK]
        last = tile[CHUNK - halo:]
        fix = jnp.where(first_sub, pltpu.roll(prev_last.reshape(k - 1, SUBLANES, width), 1, 1),
                        pltpu.roll(last.reshape(k - 1, SUBLANES, width), 1, 1)).reshape(halo, width)
        out = b_ref[:, cols]
        for i in range(k):
            shift = k - 1 - i
            if shift == 0:
                xs = tile
            else:
                xs = jnp.concatenate([fix[halo - shift * SUBLANES:], tile[:CHUNK - shift * SUBLANES]], axis=0)
            out = out + xs * w_ref[i:i + 1, cols]
        outs.append(out)
        prev_last = last
    tail_scr[:, cols] = prev_last
    return jnp.concatenate(outs, axis=0)


def _proj_odd_kernel(h_ref, g_ref, wx_ref, wy_ref, cw_ref, cb_ref, perm_ref, x_ref, gate_ref, tail_scr):
    @pl.when(pl.program_id(0) == 0)
    def _():
        tail_scr[...] = jnp.zeros_like(tail_scr)

    hn = _reorder_rows(perm_ref[...], _rms(h_ref[...], g_ref[...]).astype(BF16))
    for c0 in range(0, LRU_WIDTH, LRU_CONV_COLS):
        cols = slice(c0, c0 + LRU_CONV_COLS)
        x_ref[:, cols] = _causal_conv_interleaved(_dot(hn, wx_ref[:, cols]), tail_scr, cw_ref, cb_ref,
                                                  cols).astype(BF16)
        y = _dot(hn, wy_ref[:, cols])
        gelu = 0.5 * y * (1.0 + jnp.tanh(np.sqrt(2.0 / np.pi).astype(np.float32) * (y + 0.044715 * (y * y * y))))
        gate_ref[:, cols] = gelu.astype(BF16)


def _proj_odd(h2, g, wx, wy, cw, cb, perm):
    n, d = h2.shape
    tm = _pick_tile(n, (1024, 512, 384, 256, 128))
    rows = lambda i: (i, 0)
    w = wx.shape[1]
    return pl.pallas_call(
        _proj_odd_kernel, grid=(n // tm,),
        in_specs=[pl.BlockSpec((tm, d), rows)] + [_const_spec(a.shape) for a in (g, wx, wy, cw, cb, perm)],
        out_specs=(pl.BlockSpec((tm, w), rows), pl.BlockSpec((tm, w), rows)),
        out_shape=(jax.ShapeDtypeStruct((n, w), BF16), jax.ShapeDtypeStruct((n, w), BF16)),
        scratch_shapes=[pltpu.VMEM(((cw.shape[0] - 1) * SUBLANES, w), F32)],
        compiler_params=_cparams("arbitrary"), name="proj_odd",
    )(h2, g, wx, wy, cw, cb, perm)


LRU_BATCH_BLOCK = 4


def _rglru_kernel(x_ref, gate_ref, wa_ref, ba_ref, wi_ref, bi_ref, lam_ref, unperm_ref, o_ref, h_scr):
    t = pl.program_id(1)

    @pl.when(t == 0)
    def _():
        h_scr[...] = jnp.zeros_like(h_scr)

    for bb in range(x_ref.shape[0]):
        _rglru_tile(t, x_ref.at[bb], gate_ref.at[bb], wa_ref, ba_ref, wi_ref, bi_ref, lam_ref, unperm_ref,
                    o_ref.at[bb], h_scr.at[bb])


def _rglru_tile(t, x_ref, gate_ref, wa_ref, ba_ref, wi_ref, bi_ref, lam_ref, unperm_ref, o_ref, h_scr):
    L = x_ref.shape[0]
    xr = x_ref[...].astype(F32)
    rs, gs = [], []
    for nb in range(LRU_BLOCKS):
        sl = slice(nb * LRU_BLOCK, (nb + 1) * LRU_BLOCK)
        xb = x_ref[:, sl]
        rs.append(_dot(xb, wa_ref[nb]))
        gs.append(_dot(xb, wi_ref[nb]))
    r = _sigmoid(jnp.concatenate(rs, axis=1) + ba_ref[...])
    ig = _sigmoid(jnp.concatenate(gs, axis=1) + bi_ref[...])
    log_a = (-LRU_C) * r * _softplus(-lam_ref[...])
    a = jnp.exp(log_a)
    row = lax.broadcasted_iota(jnp.int32, (L, 1), 0)
    valid = (t * L + SEG_LEN * (row % SUBLANES) + row // SUBLANES) >= PAD
    om = 1.0 - a * a
    root = jnp.where(om > 0.0, om * lax.rsqrt(om), 0.0)
    u = jnp.where(valid, root * (ig * xr), 0.0)

    slab = lambda v, j: v[j * SUBLANES:(j + 1) * SUBLANES, :]
    h_loc = [slab(u, 0)]
    prod = [slab(a, 0)]
    for j in range(1, SEG_LEN):
        h_loc.append(slab(a, j) * h_loc[-1] + slab(u, j))
        prod.append(slab(a, j) * prod[-1])
    carry_in = h_scr[0:1, :]
    carries = []
    for s in range(SUBLANES):
        carries.append(carry_in)
        carry_in = h_loc[-1][s:s + 1, :] + prod[-1][s:s + 1, :] * carry_in
    h_scr[...] = jnp.broadcast_to(carry_in, h_scr.shape)
    carry = jnp.concatenate(carries, axis=0)
    hs = jnp.concatenate([h_loc[j] + prod[j] * carry for j in range(SEG_LEN)], axis=0)
    o_ref[...] = _reorder_rows(unperm_ref[...], (hs * gate_ref[...].astype(F32)).astype(BF16))


def _rglru(x3, gate3, wa, ba, wi, bi, lam, unperm):
    b, tp, w = x3.shape
    tt = CHUNK
    gb = _pick_tile(b, (LRU_BATCH_BLOCK, 1))
    row3 = lambda bi_, ti: (bi_, ti, 0)
    in_specs = [pl.BlockSpec((gb, tt, w), row3), pl.BlockSpec((gb, tt, w), row3)]
    in_specs += [_const_spec(a.shape) for a in (wa, ba, wi, bi, lam, unperm)]
    return pl.pallas_call(
        _rglru_kernel, grid=(b // gb, tp // tt), in_specs=in_specs,
        out_specs=pl.BlockSpec((gb, tt, w), row3),
        out_shape=jax.ShapeDtypeStruct((b, tp, w), BF16),
        scratch_shapes=[pltpu.VMEM((gb, SUBLANES, w), F32)],
        compiler_params=_cparams("parallel", "arbitrary"), name="rglru",
    )(x3, gate3, wa, ba, wi, bi, lam, unperm)


def _rope_tables(tp):
    f32 = np.float32
    inv = f32(ROPE_BASE) ** (-np.arange(0, MLA_ROPE, 2, dtype=f32) / f32(MLA_ROPE))
    pos = np.maximum(np.arange(tp, dtype=f32) - f32(PAD), f32(0))
    ang = (pos[:, None] * inv[None, :]).astype(f32)
    cos, sin = np.cos(ang).astype(f32), np.sin(ang).astype(f32)
    zeros_lo = np.zeros((tp, MLA_NOPE), f32)
    zeros_hi = np.zeros((tp, LANES - MLA_NOPE - MLA_ROPE), f32)
    ck = np.concatenate([zeros_lo, cos, cos, zeros_hi], axis=1)
    sk = np.concatenate([zeros_lo, sin, sin, zeros_hi], axis=1)
    scale = f32((MLA_NOPE + MLA_ROPE) ** -0.5 * np.log2(np.e))
    cq = np.concatenate([np.ones((tp, MLA_NOPE), f32), cos, cos, zeros_hi], axis=1) * scale
    sq = sk * scale
    return tuple(jnp.asarray(t) for t in (cq, sq, ck, sk))


def _even_weights(w_in, w_q_up, w_kv_up):
    offs = np.cumsum((SSD_D_INNER, SSD_CONV_CH, SSD_HEADS, MLA_Q_RANK, MLA_KV_RANK, MLA_ROPE))
    wz = w_in[:, :offs[0]]
    wxbc = w_in[:, offs[0]:offs[1]]
    wdt = w_in[:, offs[1]:offs[2]]
    wcq = w_in[:, offs[2]:offs[3]]
    wckv = w_in[:, offs[3]:offs[4]]
    wkr = w_in[:, offs[4]:offs[5]]
    hi_pad = LANES - MLA_NOPE - MLA_ROPE
    wdt_p = jnp.pad(wdt, ((0, 0), (0, LANES - SSD_HEADS)))
    wkr_p = jnp.pad(wkr, ((0, 0), (MLA_NOPE, hi_pad)))
    wq = w_q_up.reshape(MLA_Q_RANK, MLA_HEADS, MLA_NOPE + MLA_ROPE)
    wq_p = jnp.pad(wq, ((0, 0), (0, 0), (0, hi_pad))).reshape(MLA_Q_RANK, MLA_HEADS * LANES)
    wkv = w_kv_up.reshape(MLA_KV_RANK, MLA_HEADS, MLA_NOPE + MLA_V)
    wk_p = jnp.pad(wkv[..., :MLA_NOPE], ((0, 0), (0, 0), (0, LANES - MLA_NOPE))).reshape(
        MLA_KV_RANK, MLA_HEADS * LANES)
    wv = wkv[..., MLA_NOPE:].reshape(MLA_KV_RANK, MLA_HEADS * MLA_V)
    bf = lambda a: a.astype(BF16)
    return dict(wz=bf(wz), wxbc=bf(wxbc), wdt=bf(wdt_p), wkr_p=bf(wkr_p), wcq=bf(wcq),
                wckv=bf(wckv), wq_p=bf(wq_p), wk_p=bf(wk_p), wv=bf(wv))


def _row(a):
    return a.reshape(1, -1).astype(F32)


def kernel(x, meta_tokens, mix_pre_g, mix_post_g, mlp_pre_g, mlp_post_g, w_up, w_down, w_in, ssd_conv_w,
           ssd_conv_b, ssd_dt_bias, ssd_a_log, ssd_d, ssd_norm_g, mla_q_norm_g, mla_w_q_up, mla_kv_norm_g,
           mla_w_kv_up, w_out_ab, rg_w_x, rg_w_y, rg_conv_w, rg_conv_b, rg_w_a, rg_b_a, rg_w_i, rg_b_i,
           rg_lambda, rg_w_out):
    b, seq, d = x.shape
    depth = mix_pre_g.shape[0]
    tp = PAD + N_META + seq
    n = b * tp
    meta = jnp.broadcast_to(meta_tokens[None].astype(x.dtype), (b, N_META, d))
    h = jnp.concatenate([jnp.zeros((b, PAD, d), x.dtype), meta, x], axis=1).reshape(n, d)
    tables = _rope_tables(tp)

    for layer in range(depth):
        if layer % 2 == 0:
            e = layer // 2
            w = _even_weights(w_in[e], mla_w_q_up[e], mla_w_kv_up[e])
            wts = (w["wz"], w["wxbc"], w["wdt"], w["wkr_p"], w["wcq"], _row(mla_q_norm_g[e]),
                   w["wckv"], _row(mla_kv_norm_g[e]), w["wq_p"], w["wk_p"], w["wv"],
                   ssd_conv_w[e].astype(F32), _row(ssd_conv_b[e]))
            z, xbc, dt, q, k, v = _proj_even(h.reshape(b, tp, d), _row(mix_pre_g[layer]), wts, tables)
            lane_pad = lambda a: jnp.pad(_row(a), ((0, 0), (0, LANES - SSD_HEADS)))
            y_ssd = _ssd(xbc.reshape(b, tp, SSD_CONV_CH), dt.reshape(b, tp, LANES), z.reshape(b, tp, SSD_D_INNER),
                         lane_pad(ssd_dt_bias[e]), lane_pad(ssd_a_log[e]),
                         _row(jnp.repeat(ssd_d[e], SSD_HEAD_DIM)), _row(ssd_norm_g[e]))
            y_att = _attention(q, k, v, b, tp)
            wo = w_out_ab[e].astype(BF16)
            h = _post(h, (y_ssd.reshape(n, SSD_D_INNER), y_att), (wo[:SSD_D_INNER], wo[SSD_D_INNER:]),
                      _row(mix_post_g[layer]))
        else:
            o = layer // 2
            perm = _interleave_matrix()
            xr, gate = _proj_odd(h, _row(mix_pre_g[layer]), rg_w_x[o].astype(BF16), rg_w_y[o].astype(BF16),
                                 rg_conv_w[o].astype(F32), _row(rg_conv_b[o]), jnp.asarray(perm, BF16))
            y = _rglru(xr.reshape(b, tp, LRU_WIDTH), gate.reshape(b, tp, LRU_WIDTH), rg_w_a[o].astype(BF16),
                       _row(rg_b_a[o]), rg_w_i[o].astype(BF16), _row(rg_b_i[o]), _row(rg_lambda[o]),
                       jnp.asarray(perm.T, BF16))
            h = _post(h, (y.reshape(n, LRU_WIDTH),), (rg_w_out[o].astype(BF16),), _row(mix_post_g[layer]))
        mlp_args = (_row(mlp_pre_g[layer]), w_up[layer].astype(BF16), w_down[layer].astype(BF16),
                    _row(mlp_post_g[layer]))
        if layer == depth - 1:
            return _mlp_last(h.reshape(b, tp, d), *mlp_args, PAD + N_META)
        h = _mlp(h, *mlp_args)
```

```python
import functools

import jax
import jax.numpy as jnp
import numpy as np
from jax import lax
from jax.experimental import pallas as pl
from jax.experimental.pallas import tpu as pltpu

F32 = jnp.float32
BF16 = jnp.bfloat16

N_META = 16
CHUNK = 128
PAD = CHUNK - N_META
EPS = 1e-6
SSD_HEADS = 16
SSD_HEAD_DIM = 64
SSD_D_INNER = SSD_HEADS * SSD_HEAD_DIM
SSD_GROUPS = 2
SSD_STATE = 128
SSD_BC = SSD_GROUPS * SSD_STATE
SSD_CONV_CH = SSD_D_INNER + 2 * SSD_BC
MLA_HEADS = 16
MLA_NOPE = 64
MLA_ROPE = 32
MLA_V = 64
MLA_Q_RANK = 384
MLA_KV_RANK = 256
ROPE_BASE = 10000.0
LRU_BLOCKS = 10
LRU_BLOCK = 128
LRU_WIDTH = LRU_BLOCKS * LRU_BLOCK
LRU_C = 8.0
LRU_CONV_COLS = 256
LANES = 128
SUBLANES = 8
VMEM_LIMIT_BYTES = 56 * 1024 * 1024
NEG_BIG = -1e30


def _cparams(*sem):
    return pltpu.CompilerParams(dimension_semantics=sem, vmem_limit_bytes=VMEM_LIMIT_BYTES)


def _pick_tile(n, candidates):
    for c in candidates:
        if n % c == 0:
            return c
    raise ValueError(f"no tile in {candidates} divides {n}")


def _const_spec(shape):
    nd = len(shape)
    return pl.BlockSpec(shape, lambda *_: (0,) * nd)


def _rms(x, g):
    ms = jnp.mean(x * x, axis=-1, keepdims=True)
    return x * lax.rsqrt(ms + EPS) * g


def _sigmoid(x):
    return 1.0 / (1.0 + jnp.exp(-x))


def _softplus(x):
    return jnp.maximum(x, 0.0) + jnp.log(1.0 + jnp.exp(-jnp.abs(x)))


def _dot(a, b):
    return jnp.dot(a, b, preferred_element_type=F32)


def _dot_nt(a, b):
    return lax.dot_general(a, b, (((1,), (1,)), ((), ())), preferred_element_type=F32)


def _shift_rows(cur, tail, k):
    if k == 0:
        return cur
    rolled = pltpu.roll(cur, k, 0)
    row = lax.broadcasted_iota(jnp.int32, tail.shape, 0)
    top = jnp.where(row < k, pltpu.roll(tail, k, 0), rolled[0:SUBLANES])
    return jnp.concatenate([top, rolled[SUBLANES:]], axis=0)


def _causal_conv(cur, tail_scr, w_ref, b_ref, cols):
    k = w_ref.shape[0]
    tail = tail_scr[:, cols]
    out = b_ref[:, cols]
    for i in range(k):
        out = out + _shift_rows(cur, tail, k - 1 - i) * w_ref[i:i + 1, cols]
    tail_scr[:, cols] = cur[cur.shape[0] - SUBLANES:]
    return out


CONV_COLS = 256
Z_COLS = 256


def _rotate_half(x):
    half = MLA_ROPE // 2
    lane = lax.broadcasted_iota(jnp.int32, x.shape, 1) % LANES
    from_right = pltpu.roll(x, x.shape[1] - half, 1)
    from_left = pltpu.roll(x, half, 1)
    return jnp.where(lane < MLA_NOPE + half, -from_right, from_left)


def _proj_even_kernel(h_ref, g_ref, wz_ref, wxbc_ref, wdt_ref, wkrp_ref, wcq_ref, qg_ref,
                      wckv_ref, kvg_ref, wqp_ref, wkp_ref, wv_ref, cw_ref, cb_ref,
                      cq_ref, sq_ref, ck_ref, sk_ref,
                      z_ref, xbc_ref, dt_ref, q_ref, k_ref, v_ref, tail_scr):
    @pl.when(pl.program_id(1) == 0)
    def _():
        tail_scr[...] = jnp.zeros_like(tail_scr)

    hn = _rms(h_ref[...], g_ref[...]).astype(BF16)

    def conv_slab(c0):
        cols = slice(c0, c0 + CONV_COLS)
        conv = _causal_conv(_dot(hn, wxbc_ref[:, cols]), tail_scr, cw_ref, cb_ref, cols)
        xbc_ref[:, cols] = (conv * _sigmoid(conv)).astype(BF16)

    def z_slab(c0):
        cols = slice(c0, c0 + Z_COLS)
        z_ref[:, cols] = _dot(hn, wz_ref[:, cols]).astype(BF16)

    cqn = _rms(_dot(hn, wcq_ref[...]), qg_ref[...]).astype(BF16)
    ckvn = _rms(_dot(hn, wckv_ref[...]), kvg_ref[...]).astype(BF16)
    kr_raw = _dot(hn, wkrp_ref[...])
    kr = kr_raw * ck_ref[...] + _rotate_half(kr_raw) * sk_ref[...]
    dt_ref[...] = _dot(hn, wdt_ref[...])
    cq_t = jnp.concatenate([cq_ref[...]] * 2, axis=1)
    sq_t = jnp.concatenate([sq_ref[...]] * 2, axis=1)
    kr2 = jnp.concatenate([kr] * 2, axis=1)

    def head_pair(hd):
        sl = slice(hd * LANES, (hd + 2) * LANES)
        q_raw = _dot(cqn, wqp_ref[:, sl])
        qh = (q_raw * cq_t + _rotate_half(q_raw) * sq_t).astype(BF16)
        kh = (_dot(ckvn, wkp_ref[:, sl]) + kr2).astype(BF16)
        for j in range(2):
            q_ref[hd + j] = qh[:, j * LANES:(j + 1) * LANES]
            k_ref[hd + j] = kh[:, j * LANES:(j + 1) * LANES]

    def v_proj(_):
        v_ref[...] = _dot(ckvn, wv_ref[...]).astype(BF16)

    convs = [(conv_slab, c0) for c0 in range(0, SSD_CONV_CH, CONV_COLS)]
    pairs = [(head_pair, hd) for hd in range(0, MLA_HEADS, 2)]
    heavy = []
    while convs or pairs:
        heavy += convs[:1] + pairs[:1]
        convs, pairs = convs[1:], pairs[1:]
    light = [(v_proj, None)] + [(z_slab, c0) for c0 in range(0, SSD_D_INNER, Z_COLS)]
    order = []
    for i, item in enumerate(heavy):
        order.append(item)
        if (i + 1) % 3 == 0 and len(light) > 1:
            order.append(light.pop(0))
    order += light
    for fn, arg in order:
        fn(arg)


def _proj_even(h3, g, wts, tables):
    b, tp, d = h3.shape
    tm = _pick_tile(tp, (384, 128))
    nt = tp // tm
    n = b * tp
    row3 = lambda bi, i: (bi, i, 0)
    flat = lambda bi, i: (bi * nt + i, 0)
    tab = lambda bi, i: (i, 0)
    in_specs = [pl.BlockSpec((None, tm, d), row3), _const_spec(g.shape)]
    in_specs += [_const_spec(w.shape) for w in wts]
    in_specs += [pl.BlockSpec((tm, LANES), tab) for _ in tables]
    out_shape = (
        jax.ShapeDtypeStruct((n, SSD_D_INNER), BF16),
        jax.ShapeDtypeStruct((n, SSD_CONV_CH), BF16),
        jax.ShapeDtypeStruct((n, LANES), F32),
        jax.ShapeDtypeStruct((MLA_HEADS, n, LANES), BF16),
        jax.ShapeDtypeStruct((MLA_HEADS, n, LANES), BF16),
        jax.ShapeDtypeStruct((n, MLA_HEADS * MLA_V), BF16),
    )
    head3 = lambda bi, i: (0, bi * nt + i, 0)
    out_specs = (
        pl.BlockSpec((tm, SSD_D_INNER), flat),
        pl.BlockSpec((tm, SSD_CONV_CH), flat),
        pl.BlockSpec((tm, LANES), flat),
        pl.BlockSpec((MLA_HEADS, tm, LANES), head3),
        pl.BlockSpec((MLA_HEADS, tm, LANES), head3),
        pl.BlockSpec((tm, MLA_HEADS * MLA_V), flat),
    )
    return pl.pallas_call(
        _proj_even_kernel, grid=(b, nt), in_specs=in_specs, out_specs=out_specs, out_shape=out_shape,
        scratch_shapes=[pltpu.VMEM((SUBLANES, SSD_CONV_CH), F32)],
        compiler_params=_cparams("parallel", "arbitrary"), name="proj_even",
    )(h3, g, *wts, *tables)


def _cumsum_rows(x):
    n = x.shape[0]
    row = lax.broadcasted_iota(jnp.int32, x.shape, 0)
    d = 1
    while d < n:
        x = x + jnp.where(row >= d, pltpu.roll(x, d, 0), 0.0)
        d *= 2
    return x


SSD_BATCH_BLOCK = 4


def _ssd_chunk(c, xbc, dt_raw, zf, dtb, alog, dsk, ng, state_ref, y_scr):
    L = CHUNK
    heads_per_group = SSD_HEADS // SSD_GROUPS
    row = lax.broadcasted_iota(jnp.int32, (L, 1), 0)
    valid = (c * L + row) >= PAD
    act = xbc.astype(F32)
    xs = act[:, :SSD_D_INNER]
    bm = act[:, SSD_D_INNER:SSD_D_INNER + SSD_BC]
    cm = act[:, SSD_D_INNER + SSD_BC:]

    dtv = jnp.where(valid, _softplus(dt_raw + dtb), 0.0)
    a_cs = _cumsum_rows(dtv * (-jnp.exp(alog)))
    a_cs_t = a_cs.T
    dt_t = dtv.T
    tot = a_cs[L - 1:L, :]
    w_t = jnp.exp(a_cs_t[:, L - 1:L] - a_cs_t) * dt_t
    shifted_t = a_cs_t - jnp.log(dt_t)
    cd = jnp.exp(tot)
    causal = lax.broadcasted_iota(jnp.int32, (L, L), 0) >= lax.broadcasted_iota(jnp.int32, (L, L), 1)
    lo = lax.broadcasted_iota(jnp.int32, (L, LANES), 1) < SSD_HEAD_DIM

    for g in range(SSD_GROUPS):
        cc = cm[:, g * SSD_STATE:(g + 1) * SSD_STATE]
        bc = bm[:, g * SSD_STATE:(g + 1) * SSD_STATE]
        bc_t = bc.T
        cb = _dot_nt(cc.astype(BF16), bc.astype(BF16))
        for hp in range(heads_per_group // 2):
            pair = g * (heads_per_group // 2) + hp
            sl = slice(pair * LANES, (pair + 1) * LANES)
            xs_pair = xs[:, sl]
            st_pair = state_ref[:, sl]
            xs_b = xs_pair.astype(BF16)
            rhs = jnp.concatenate([xs_b, st_pair.astype(BF16)], axis=0)
            ys, sts, cds = [], [], []
            for hh in range(2):
                h = 2 * pair + hh
                a_col = jnp.broadcast_to(a_cs[:, h:h + 1], (L, L))
                m = cb * jnp.exp(jnp.where(causal, a_col - shifted_t[h:h + 1, :], -jnp.inf))
                lhs = jnp.concatenate([m, cc * jnp.exp(a_col)], axis=1).astype(BF16)
                ys.append(_dot(lhs, rhs))
                sts.append(_dot((bc_t * w_t[h:h + 1, :]).astype(BF16), xs_b))
                cds.append(cd[:, h:h + 1])
            y_scr[:, sl] = jnp.where(lo, ys[0], ys[1]) + dsk[:, sl] * xs_pair
            state_ref[:, sl] = jnp.where(lo, cds[0], cds[1]) * st_pair + jnp.where(lo, sts[0], sts[1])

    return _rms(y_scr[...] * (zf * _sigmoid(zf)), ng).astype(BF16)


def _ssd_kernel(xbc_ref, dt_ref, z_ref, dtb_ref, alog_ref, dsk_ref, ng_ref, y_ref, state_scr, y_scr):
    c = pl.program_id(1)

    @pl.when(c == 0)
    def _():
        state_scr[...] = jnp.zeros_like(state_scr)

    for bb in range(xbc_ref.shape[0]):
        y_ref[bb] = _ssd_chunk(c, xbc_ref[bb], dt_ref[bb], z_ref[bb].astype(F32), dtb_ref[...], alog_ref[...],
                               dsk_ref[...], ng_ref[...], state_scr.at[bb], y_scr.at[bb])


def _ssd(xbc3, dt3, z3, dtb, alog, dsk, ng):
    b, tp, _ = xbc3.shape
    nc = tp // CHUNK
    gb = _pick_tile(b, (SSD_BATCH_BLOCK, 1))
    row3 = lambda bi, ci: (bi, ci, 0)
    in_specs = [
        pl.BlockSpec((gb, CHUNK, SSD_CONV_CH), row3),
        pl.BlockSpec((gb, CHUNK, LANES), row3),
        pl.BlockSpec((gb, CHUNK, SSD_D_INNER), row3),
    ] + [_const_spec(a.shape) for a in (dtb, alog, dsk, ng)]
    return pl.pallas_call(
        _ssd_kernel, grid=(b // gb, nc), in_specs=in_specs,
        out_specs=pl.BlockSpec((gb, CHUNK, SSD_D_INNER), row3),
        out_shape=jax.ShapeDtypeStruct((b, tp, SSD_D_INNER), BF16),
        scratch_shapes=[pltpu.VMEM((gb, SSD_STATE, SSD_D_INNER), F32),
                        pltpu.VMEM((gb, CHUNK, SSD_D_INNER), F32)],
        compiler_params=_cparams("parallel", "arbitrary"), name="ssd",
    )(xbc3, dt3, z3, dtb, alog, dsk, ng)


ATTN_KV_BLOCK = 256
ATTN_V_ROWS = MLA_V + 16
PLAIN_UNROLLS = (4, 2, 1)
ATTN_COL_SLAB = 256


def _attn_kernel(q_ref, k_ref, v_ref, o_ref, vt_scr, s_scr, bmax_scr, m_scr, acc_scr, *, tq, nq):
    kb = ATTN_KV_BLOCK
    qi = pl.program_id(2)
    q0 = qi * tq
    end = q0 + tq
    n_int = q0 // kb
    n_blocks = (end + kb - 1) // kb
    last_off = end - kb

    @pl.when(qi == 0)
    def _():
        ones = jnp.ones((ATTN_V_ROWS - MLA_V, CHUNK), BF16)
        for c in range(vt_scr.shape[0]):
            vt = v_ref[c * CHUNK:(c + 1) * CHUNK, :].astype(F32).T.astype(BF16)
            for hh in range(2):
                vt_scr[c, hh, :MLA_V, :] = vt[hh * MLA_V:(hh + 1) * MLA_V, :]
                vt_scr[c, hh, MLA_V:, :] = ones

    m_scr[...] = jnp.full_like(m_scr, NEG_BIG)
    acc_scr[...] = jnp.zeros_like(acc_scr)

    def offset(bk):
        return pl.multiple_of(jnp.minimum(bk * kb, last_off), CHUNK)

    levels = tuple(range(0, tq - kb + 1, kb))

    def col_slabs(cs):
        cuts = sorted({cs} | {c for c in range(tq, cs, -ATTN_COL_SLAB)})
        if len(cuts) > 2 and cuts[1] - cuts[0] < ATTN_COL_SLAB // 2:
            del cuts[1]
        return [slice(a, b) for a, b in zip(cuts[:-1], cuts[1:])]

    def qk(hh, bk, masked, cols):
        off = offset(bk)
        s = _dot_nt(k_ref[hh, pl.ds(off, kb), :], q_ref[hh, cols, :])
        if masked:
            key = lax.broadcasted_iota(jnp.int32, s.shape, 0)
            keep = (key - lax.broadcasted_iota(jnp.int32, s.shape, 1)) <= (q0 + cols.start - off)
            first_key = jnp.where(bk == 0, PAD, bk * kb - off)
            s = jnp.where(keep, s, NEG_BIG)
            s = jnp.where(key >= first_key, s, NEG_BIG)
        s_scr[hh, :, cols] = s
        bmax_scr[hh, :, cols] = jnp.broadcast_to(jnp.max(s, axis=0, keepdims=True), (SUBLANES, s.shape[1]))

    def softmax_pv(hh, bk, cols):
        m_prev = m_scr[hh, 0:1, cols]
        m_new = jnp.maximum(m_prev, bmax_scr[hh, 0:1, cols])
        alpha = jnp.exp2(m_prev - m_new)
        m_scr[hh, :, cols] = jnp.broadcast_to(m_new, (SUBLANES, m_new.shape[1]))
        p = jnp.exp2(s_scr[hh, :, cols] - m_new).astype(BF16)
        c0 = offset(bk) // CHUNK
        vt = jnp.concatenate([vt_scr[c0 + c, hh] for c in range(kb // CHUNK)], axis=1)
        acc_scr[hh, :, cols] = alpha * acc_scr[hh, :, cols] + _dot(vt, p)

    def step(bk, masked, cs_prev, cs):
        prev_slabs, cur_slabs = col_slabs(cs_prev), col_slabs(cs)
        for j in range(max(len(prev_slabs), len(cur_slabs))):
            if j < len(cur_slabs):
                qk(0, bk, masked, cur_slabs[j])
            if j < len(prev_slabs):
                softmax_pv(1, bk - 1, prev_slabs[j])
        for j, cols in enumerate(cur_slabs):
            qk(1, bk, masked, cols)
            softmax_pv(0, bk, cols)

    def plain_run(first, count, unroll):
        def trip(i, carry):
            for u in range(unroll):
                step(first + unroll * i + u, False, 0, 0)
            return carry

        trips = count // unroll
        lax.fori_loop(0, trips, trip, 0)
        return first + unroll * trips, count - unroll * trips

    def diagonal_run(residue, skip):
        n_diag = -(-(residue + tq) // kb)
        lvls = [min(max(min(i * kb - residue, tq - kb), 0) // kb, len(levels) - 1) * kb for i in range(n_diag)]
        for i in range(skip, n_diag):
            step(n_int + i, True, lvls[i - 1] if i > 0 else 0, lvls[i])
        for cols in col_slabs(lvls[-1]):
            softmax_pv(1, n_blocks - 1, cols)

    def first_block():
        for cols in col_slabs(0):
            qk(0, 0, True, cols)
        for cols in col_slabs(0):
            qk(1, 0, True, cols)
            softmax_pv(0, 0, cols)

    @pl.when(qi == 0)
    def _():
        first_block()
        diagonal_run(0, 1)

    @pl.when(qi > 0)
    def _():
        first_block()

    first, left = 1, jnp.maximum(n_int - 1, 0)
    for unroll in PLAIN_UNROLLS:
        first, left = plain_run(first, left, unroll)
    for residue in sorted({(i * tq) % kb for i in range(1, nq)}):
        @pl.when(jnp.logical_and(qi > 0, q0 % kb == residue))
        def _():
            diagonal_run(residue, 0)

    out_t = jnp.concatenate([acc_scr[hh, :MLA_V, :] / acc_scr[hh, MLA_V:MLA_V + 1, :] for hh in range(2)],
                            axis=0)
    qpos = q0 + lax.broadcasted_iota(jnp.int32, (1, tq), 1)
    out_t = jnp.where(qpos >= PAD, out_t, 0.0)
    for c in range(tq // CHUNK):
        o_ref[c * CHUNK:(c + 1) * CHUNK, :] = out_t[:, c * CHUNK:(c + 1) * CHUNK].T.astype(BF16)


def _attention(q, k, v, b, tp):
    tq = _pick_tile(tp, (1408, 384, 256))
    nq = tp // tq
    n = b * tp
    in_specs = [
        pl.BlockSpec((2, tq, LANES), lambda bi, hp, qi: (hp, bi * nq + qi, 0)),
        pl.BlockSpec((2, tp, LANES), lambda bi, hp, qi: (hp, bi, 0)),
        pl.BlockSpec((tp, LANES), lambda bi, hp, qi: (bi, hp)),
    ]
    return pl.pallas_call(
        functools.partial(_attn_kernel, tq=tq, nq=nq), grid=(b, MLA_HEADS // 2, nq), in_specs=in_specs,
        out_specs=pl.BlockSpec((tq, LANES), lambda bi, hp, qi: (bi * nq + qi, hp)),
        out_shape=jax.ShapeDtypeStruct((n, MLA_HEADS * MLA_V), BF16),
        scratch_shapes=[pltpu.VMEM((tp // CHUNK, 2, ATTN_V_ROWS, CHUNK), BF16),
                        pltpu.VMEM((2, ATTN_KV_BLOCK, tq), F32), pltpu.VMEM((2, SUBLANES, tq), F32),
                        pltpu.VMEM((2, SUBLANES, tq), F32), pltpu.VMEM((2, ATTN_V_ROWS, tq), F32)],
        compiler_params=_cparams("parallel", "parallel", "arbitrary"), name="attention",
    )(q, k, v)


POST_ROW_BLOCK = 256


def _post_kernel(*refs, n_in):
    h_ref = refs[0]
    a_refs = refs[1:1 + n_in]
    w_refs = refs[1 + n_in:1 + 2 * n_in]
    g_ref = refs[1 + 2 * n_in]
    o_ref = refs[2 + 2 * n_in]
    rows_per_block = min(POST_ROW_BLOCK, h_ref.shape[0])
    for r0 in range(0, h_ref.shape[0], rows_per_block):
        rows = slice(r0, r0 + rows_per_block)
        m = _dot(a_refs[0][rows, :], w_refs[0][...])
        for a_ref, w_ref in zip(a_refs[1:], w_refs[1:]):
            m = m + _dot(a_ref[rows, :], w_ref[...])
        o_ref[rows, :] = h_ref[rows, :] + _rms(m, g_ref[...])


def _post(h2, acts, wts, g):
    n, d = h2.shape
    tm = _pick_tile(n, (1024, 512, 384, 256, 128))
    rows = lambda i: (i, 0)
    in_specs = [pl.BlockSpec((tm, d), rows)]
    in_specs += [pl.BlockSpec((tm, a.shape[1]), rows) for a in acts]
    in_specs += [_const_spec(w.shape) for w in wts] + [_const_spec(g.shape)]
    return pl.pallas_call(
        functools.partial(_post_kernel, n_in=len(acts)), grid=(n // tm,), in_specs=in_specs,
        out_specs=pl.BlockSpec((tm, d), rows), out_shape=jax.ShapeDtypeStruct((n, d), F32),
        compiler_params=_cparams("parallel"), name="mixer_out",
    )(h2, *acts, *wts, g)


def _mlp_kernel(h_ref, g1_ref, wup_ref, wdn_ref, g2_ref, o_ref, *, ff_chunk):
    x = h_ref[...]
    hn = _rms(x, g1_ref[...]).astype(BF16)
    d_ff = wup_ref.shape[1]
    y = None
    for c0 in range(0, d_ff, ff_chunk):
        u = jnp.maximum(_dot(hn, wup_ref[:, c0:c0 + ff_chunk]), 0.0)
        part = _dot((u * u).astype(BF16), wdn_ref[c0:c0 + ff_chunk, :])
        y = part if y is None else y + part
    o_ref[...] = x + _rms(y, g2_ref[...])


def _mlp(h2, g1, wup, wdn, g2):
    n, d = h2.shape
    tm = _pick_tile(n, (512, 256, 128))
    rows = lambda i: (i, 0)
    in_specs = [pl.BlockSpec((tm, d), rows), _const_spec(g1.shape), _const_spec(wup.shape),
                _const_spec(wdn.shape), _const_spec(g2.shape)]
    return pl.pallas_call(
        functools.partial(_mlp_kernel, ff_chunk=1024), grid=(n // tm,), in_specs=in_specs,
        out_specs=pl.BlockSpec((tm, d), rows), out_shape=jax.ShapeDtypeStruct((n, d), F32),
        compiler_params=_cparams("parallel"), name="mlp",
    )(h2, g1, wup, wdn, g2)


def _mlp_last(h3, g1, wup, wdn, g2, lead):
    b, tp, d = h3.shape
    seq = tp - lead
    tm = _pick_tile(seq, (512, 256, 128))
    in_specs = [pl.BlockSpec((pl.Element(tm), pl.Element(d)), lambda bi, i: (pl.multiple_of(bi * tp + lead + i * tm, CHUNK), 0)),
                _const_spec(g1.shape), _const_spec(wup.shape), _const_spec(wdn.shape), _const_spec(g2.shape)]
    return pl.pallas_call(
        functools.partial(_mlp_kernel, ff_chunk=1024), grid=(b, seq // tm), in_specs=in_specs,
        out_specs=pl.BlockSpec((None, tm, d), lambda bi, i: (bi, i, 0)),
        out_shape=jax.ShapeDtypeStruct((b, seq, d), F32),
        compiler_params=_cparams("parallel", "parallel"), name="mlp_last",
    )(h3.reshape(b * tp, d), g1, wup, wdn, g2)


SEG_LEN = CHUNK // SUBLANES


def _interleave_matrix():
    p = np.zeros((CHUNK, CHUNK), np.float32)
    for j in range(SEG_LEN):
        for s in range(SUBLANES):
            p[SUBLANES * j + s, SEG_LEN * s + j] = 1.0
    return p


def _reorder_rows(perm, x):
    tiles = [_dot(perm, x[t0:t0 + CHUNK]).astype(BF16) for t0 in range(0, x.shape[0], CHUNK)]
    return tiles[0] if len(tiles) == 1 else jnp.concatenate(tiles, axis=0)


def _causal_conv_interleaved(pre, tail_scr, w_ref, b_ref, cols):
    k = w_ref.shape[0]
    halo = (k - 1) * SUBLANES
    width = pre.shape[1]
    first_sub = lax.broadcasted_iota(jnp.int32, (k - 1, SUBLANES, width), 1) == 0
    prev_last = tail_scr[:, cols]
    outs = []
    for t0 in range(0, pre.shape[0], CHUNK):
        tile = pre[t0:t0 + CHUNK]
        last = tile[CHUNK - halo:]
        fix = jnp.where(first_sub, pltpu.roll(prev_last.reshape(k - 1, SUBLANES, width), 1, 1),
                        pltpu.roll(last.reshape(k - 1, SUBLANES, width), 1, 1)).reshape(halo, width)
        out = b_ref[:, cols]
        for i in range(k):
            shift = k - 1 - i
            if shift == 0:
                xs = tile
            else:
                xs = jnp.concatenate([fix[halo - shift * SUBLANES:], tile[:CHUNK - shift * SUBLANES]], axis=0)
            out = out + xs * w_ref[i:i + 1, cols]
        outs.append(out)
        prev_last = last
    tail_scr[:, cols] = prev_last
    return jnp.concatenate(outs, axis=0)


def _proj_odd_kernel(h_ref, g_ref, wx_ref, wy_ref, cw_ref, cb_ref, perm_ref, x_ref, gate_ref, tail_scr):
    @pl.when(pl.program_id(0) == 0)
    def _():
        tail_scr[...] = jnp.zeros_like(tail_scr)

    hn = _reorder_rows(perm_ref[...], _rms(h_ref[...], g_ref[...]).astype(BF16))
    for c0 in range(0, LRU_WIDTH, LRU_CONV_COLS):
        cols = slice(c0, c0 + LRU_CONV_COLS)
        x_ref[:, cols] = _causal_conv_interleaved(_dot(hn, wx_ref[:, cols]), tail_scr, cw_ref, cb_ref,
                                                  cols).astype(BF16)
        y = _dot(hn, wy_ref[:, cols])
        gelu = 0.5 * y * (1.0 + jnp.tanh(np.sqrt(2.0 / np.pi).astype(np.float32) * (y + 0.044715 * (y * y * y))))
        gate_ref[:, cols] = gelu.astype(BF16)


def _proj_odd(h2, g, wx, wy, cw, cb, perm):
    n, d = h2.shape
    tm = _pick_tile(n, (1024, 512, 384, 256, 128))
    rows = lambda i: (i, 0)
    w = wx.shape[1]
    return pl.pallas_call(
        _proj_odd_kernel, grid=(n // tm,),
        in_specs=[pl.BlockSpec((tm, d), rows)] + [_const_spec(a.shape) for a in (g, wx, wy, cw, cb, perm)],
        out_specs=(pl.BlockSpec((tm, w), rows), pl.BlockSpec((tm, w), rows)),
        out_shape=(jax.ShapeDtypeStruct((n, w), BF16), jax.ShapeDtypeStruct((n, w), BF16)),
        scratch_shapes=[pltpu.VMEM(((cw.shape[0] - 1) * SUBLANES, w), F32)],
        compiler_params=_cparams("arbitrary"), name="proj_odd",
    )(h2, g, wx, wy, cw, cb, perm)


LRU_BATCH_BLOCK = 4


def _rglru_kernel(x_ref, gate_ref, wa_ref, ba_ref, wi_ref, bi_ref, lam_ref, unperm_ref, o_ref, h_scr):
    t = pl.program_id(1)

    @pl.when(t == 0)
    def _():
        h_scr[...] = jnp.zeros_like(h_scr)

    for bb in range(x_ref.shape[0]):
        _rglru_tile(t, x_ref.at[bb], gate_ref.at[bb], wa_ref, ba_ref, wi_ref, bi_ref, lam_ref, unperm_ref,
                    o_ref.at[bb], h_scr.at[bb])


def _rglru_tile(t, x_ref, gate_ref, wa_ref, ba_ref, wi_ref, bi_ref, lam_ref, unperm_ref, o_ref, h_scr):
    L = x_ref.shape[0]
    xr = x_ref[...].astype(F32)
    rs, gs = [], []
    for nb in range(LRU_BLOCKS):
        sl = slice(nb * LRU_BLOCK, (nb + 1) * LRU_BLOCK)
        xb = x_ref[:, sl]
        rs.append(_dot(xb, wa_ref[nb]))
        gs.append(_dot(xb, wi_ref[nb]))
    r = _sigmoid(jnp.concatenate(rs, axis=1) + ba_ref[...])
    ig = _sigmoid(jnp.concatenate(gs, axis=1) + bi_ref[...])
    log_a = (-LRU_C) * r * _softplus(-lam_ref[...])
    a = jnp.exp(log_a)
    row = lax.broadcasted_iota(jnp.int32, (L, 1), 0)
    valid = (t * L + SEG_LEN * (row % SUBLANES) + row // SUBLANES) >= PAD
    om = 1.0 - a * a
    root = jnp.where(om > 0.0, om * lax.rsqrt(om), 0.0)
    u = jnp.where(valid, root * (ig * xr), 0.0)

    slab = lambda v, j: v[j * SUBLANES:(j + 1) * SUBLANES, :]
    h_loc = [slab(u, 0)]
    prod = [slab(a, 0)]
    for j in range(1, SEG_LEN):
        h_loc.append(slab(a, j) * h_loc[-1] + slab(u, j))
        prod.append(slab(a, j) * prod[-1])
    carry_in = h_scr[0:1, :]
    carries = []
    for s in range(SUBLANES):
        carries.append(carry_in)
        carry_in = h_loc[-1][s:s + 1, :] + prod[-1][s:s + 1, :] * carry_in
    h_scr[...] = jnp.broadcast_to(carry_in, h_scr.shape)
    carry = jnp.concatenate(carries, axis=0)
    hs = jnp.concatenate([h_loc[j] + prod[j] * carry for j in range(SEG_LEN)], axis=0)
    o_ref[...] = _reorder_rows(unperm_ref[...], (hs * gate_ref[...].astype(F32)).astype(BF16))


def _rglru(x3, gate3, wa, ba, wi, bi, lam, unperm):
    b, tp, w = x3.shape
    tt = CHUNK
    gb = _pick_tile(b, (LRU_BATCH_BLOCK, 1))
    row3 = lambda bi_, ti: (bi_, ti, 0)
    in_specs = [pl.BlockSpec((gb, tt, w), row3), pl.BlockSpec((gb, tt, w), row3)]
    in_specs += [_const_spec(a.shape) for a in (wa, ba, wi, bi, lam, unperm)]
    return pl.pallas_call(
        _rglru_kernel, grid=(b // gb, tp // tt), in_specs=in_specs,
        out_specs=pl.BlockSpec((gb, tt, w), row3),
        out_shape=jax.ShapeDtypeStruct((b, tp, w), BF16),
        scratch_shapes=[pltpu.VMEM((gb, SUBLANES, w), F32)],
        compiler_params=_cparams("parallel", "arbitrary"), name="rglru",
    )(x3, gate3, wa, ba, wi, bi, lam, unperm)


def _rope_tables(tp):
    f32 = np.float32
    inv = f32(ROPE_BASE) ** (-np.arange(0, MLA_ROPE, 2, dtype=f32) / f32(MLA_ROPE))
    pos = np.maximum(np.arange(tp, dtype=f32) - f32(PAD), f32(0))
    ang = (pos[:, None] * inv[None, :]).astype(f32)
    cos, sin = np.cos(ang).astype(f32), np.sin(ang).astype(f32)
    zeros_lo = np.zeros((tp, MLA_NOPE), f32)
    zeros_hi = np.zeros((tp, LANES - MLA_NOPE - MLA_ROPE), f32)
    ck = np.concatenate([zeros_lo, cos, cos, zeros_hi], axis=1)
    sk = np.concatenate([zeros_lo, sin, sin, zeros_hi], axis=1)
    scale = f32((MLA_NOPE + MLA_ROPE) ** -0.5 * np.log2(np.e))
    cq = np.concatenate([np.ones((tp, MLA_NOPE), f32), cos, cos, zeros_hi], axis=1) * scale
    sq = sk * scale
    return tuple(jnp.asarray(t) for t in (cq, sq, ck, sk))


def _even_weights(w_in, w_q_up, w_kv_up):
    offs = np.cumsum((SSD_D_INNER, SSD_CONV_CH, SSD_HEADS, MLA_Q_RANK, MLA_KV_RANK, MLA_ROPE))
    wz = w_in[:, :offs[0]]
    wxbc = w_in[:, offs[0]:offs[1]]
    wdt = w_in[:, offs[1]:offs[2]]
    wcq = w_in[:, offs[2]:offs[3]]
    wckv = w_in[:, offs[3]:offs[4]]
    wkr = w_in[:, offs[4]:offs[5]]
    hi_pad = LANES - MLA_NOPE - MLA_ROPE
    wdt_p = jnp.pad(wdt, ((0, 0), (0, LANES - SSD_HEADS)))
    wkr_p = jnp.pad(wkr, ((0, 0), (MLA_NOPE, hi_pad)))
    wq = w_q_up.reshape(MLA_Q_RANK, MLA_HEADS, MLA_NOPE + MLA_ROPE)
    wq_p = jnp.pad(wq, ((0, 0), (0, 0), (0, hi_pad))).reshape(MLA_Q_RANK, MLA_HEADS * LANES)
    wkv = w_kv_up.reshape(MLA_KV_RANK, MLA_HEADS, MLA_NOPE + MLA_V)
    wk_p = jnp.pad(wkv[..., :MLA_NOPE], ((0, 0), (0, 0), (0, LANES - MLA_NOPE))).reshape(
        MLA_KV_RANK, MLA_HEADS * LANES)
    wv = wkv[..., MLA_NOPE:].reshape(MLA_KV_RANK, MLA_HEADS * MLA_V)
    bf = lambda a: a.astype(BF16)
    return dict(wz=bf(wz), wxbc=bf(wxbc), wdt=bf(wdt_p), wkr_p=bf(wkr_p), wcq=bf(wcq),
                wckv=bf(wckv), wq_p=bf(wq_p), wk_p=bf(wk_p), wv=bf(wv))


def _row(a):
    return a.reshape(1, -1).astype(F32)


def kernel(x, meta_tokens, mix_pre_g, mix_post_g, mlp_pre_g, mlp_post_g, w_up, w_down, w_in, ssd_conv_w,
           ssd_conv_b, ssd_dt_bias, ssd_a_log, ssd_d, ssd_norm_g, mla_q_norm_g, mla_w_q_up, mla_kv_norm_g,
           mla_w_kv_up, w_out_ab, rg_w_x, rg_w_y, rg_conv_w, rg_conv_b, rg_w_a, rg_b_a, rg_w_i, rg_b_i,
           rg_lambda, rg_w_out):
    b, seq, d = x.shape
    depth = mix_pre_g.shape[0]
    tp = PAD + N_META + seq
    n = b * tp
    meta = jnp.broadcast_to(meta_tokens[None].astype(x.dtype), (b, N_META, d))
    h = jnp.concatenate([jnp.zeros((b, PAD, d), x.dtype), meta, x], axis=1).reshape(n, d)
    tables = _rope_tables(tp)

    for layer in range(depth):
        if layer % 2 == 0:
            e = layer // 2
            w = _even_weights(w_in[e], mla_w_q_up[e], mla_w_kv_up[e])
            wts = (w["wz"], w["wxbc"], w["wdt"], w["wkr_p"], w["wcq"], _row(mla_q_norm_g[e]),
                   w["wckv"], _row(mla_kv_norm_g[e]), w["wq_p"], w["wk_p"], w["wv"],
                   ssd_conv_w[e].astype(F32), _row(ssd_conv_b[e]))
            z, xbc, dt, q, k, v = _proj_even(h.reshape(b, tp, d), _row(mix_pre_g[layer]), wts, tables)
            lane_pad = lambda a: jnp.pad(_row(a), ((0, 0), (0, LANES - SSD_HEADS)))
            y_ssd = _ssd(xbc.reshape(b, tp, SSD_CONV_CH), dt.reshape(b, tp, LANES), z.reshape(b, tp, SSD_D_INNER),
                         lane_pad(ssd_dt_bias[e]), lane_pad(ssd_a_log[e]),
                         _row(jnp.repeat(ssd_d[e], SSD_HEAD_DIM)), _row(ssd_norm_g[e]))
            y_att = _attention(q, k, v, b, tp)
            wo = w_out_ab[e].astype(BF16)
            h = _post(h, (y_ssd.reshape(n, SSD_D_INNER), y_att), (wo[:SSD_D_INNER], wo[SSD_D_INNER:]),
                      _row(mix_post_g[layer]))
        else:
            o = layer // 2
            perm = _interleave_matrix()
            xr, gate = _proj_odd(h, _row(mix_pre_g[layer]), rg_w_x[o].astype(BF16), rg_w_y[o].astype(BF16),
                                 rg_conv_w[o].astype(F32), _row(rg_conv_b[o]), jnp.asarray(perm, BF16))
            y = _rglru(xr.reshape(b, tp, LRU_WIDTH), gate.reshape(b, tp, LRU_WIDTH), rg_w_a[o].astype(BF16),
                       _row(rg_b_a[o]), rg_w_i[o].astype(BF16), _row(rg_b_i[o]), _row(rg_lambda[o]),
                       jnp.asarray(perm.T, BF16))
            h = _post(h, (y.reshape(n, LRU_WIDTH),), (rg_w_out[o].astype(BF16),), _row(mix_post_g[layer]))
        mlp_args = (_row(mlp_pre_g[layer]), w_up[layer].astype(BF16), w_down[layer].astype(BF16),
                    _row(mlp_post_g[layer]))
        if layer == depth - 1:
            return _mlp_last(h.reshape(b, tp, d), *mlp_args, PAD + N_META)
        h = _mlp(h, *mlp_args)
```

```python
import functools

import jax
import jax.numpy as jnp
import numpy as np
from jax import lax
from jax.experimental import pallas as pl
from jax.experimental.pallas import tpu as pltpu

F32 = jnp.float32
BF16 = jnp.bfloat16

N_META = 16
CHUNK = 128
PAD = CHUNK - N_META
EPS = 1e-6
SSD_HEADS = 16
SSD_HEAD_DIM = 64
SSD_D_INNER = SSD_HEADS * SSD_HEAD_DIM
SSD_GROUPS = 2
SSD_STATE = 128
SSD_BC = SSD_GROUPS * SSD_STATE
SSD_CONV_CH = SSD_D_INNER + 2 * SSD_BC
MLA_HEADS = 16
MLA_NOPE = 64
MLA_ROPE = 32
MLA_V = 64
MLA_Q_RANK = 384
MLA_KV_RANK = 256
ROPE_BASE = 10000.0
LRU_BLOCKS = 10
LRU_BLOCK = 128
LRU_WIDTH = LRU_BLOCKS * LRU_BLOCK
LRU_C = 8.0
LRU_CONV_COLS = 256
LANES = 128
SUBLANES = 8
VMEM_LIMIT_BYTES = 56 * 1024 * 1024
NEG_BIG = -1e30


def _cparams(*sem):
    return pltpu.CompilerParams(dimension_semantics=sem, vmem_limit_bytes=VMEM_LIMIT_BYTES)


def _pick_tile(n, candidates):
    for c in candidates:
        if n % c == 0:
            return c
    raise ValueError(f"no tile in {candidates} divides {n}")


def _const_spec(shape):
    nd = len(shape)
    return pl.BlockSpec(shape, lambda *_: (0,) * nd)


def _rms(x, g):
    ms = jnp.mean(x * x, axis=-1, keepdims=True)
    return x * lax.rsqrt(ms + EPS) * g


def _sigmoid(x):
    return 1.0 / (1.0 + jnp.exp(-x))


def _softplus(x):
    return jnp.maximum(x, 0.0) + jnp.log(1.0 + jnp.exp(-jnp.abs(x)))


def _dot(a, b):
    return jnp.dot(a, b, preferred_element_type=F32)


def _dot_nt(a, b):
    return lax.dot_general(a, b, (((1,), (1,)), ((), ())), preferred_element_type=F32)


def _shift_rows(cur, tail, k):
    if k == 0:
        return cur
    rolled = pltpu.roll(cur, k, 0)
    row = lax.broadcasted_iota(jnp.int32, tail.shape, 0)
    top = jnp.where(row < k, pltpu.roll(tail, k, 0), rolled[0:SUBLANES])
    return jnp.concatenate([top, rolled[SUBLANES:]], axis=0)


def _causal_conv(cur, tail_scr, w_ref, b_ref, cols):
    k = w_ref.shape[0]
    tail = tail_scr[:, cols]
    out = b_ref[:, cols]
    for i in range(k):
        out = out + _shift_rows(cur, tail, k - 1 - i) * w_ref[i:i + 1, cols]
    tail_scr[:, cols] = cur[cur.shape[0] - SUBLANES:]
    return out


CONV_COLS = 256
Z_COLS = 256


def _rotate_half(x):
    half = MLA_ROPE // 2
    lane = lax.broadcasted_iota(jnp.int32, x.shape, 1) % LANES
    from_right = pltpu.roll(x, x.shape[1] - half, 1)
    from_left = pltpu.roll(x, half, 1)
    return jnp.where(lane < MLA_NOPE + half, -from_right, from_left)


def _proj_even_kernel(h_ref, g_ref, wz_ref, wxbc_ref, wdt_ref, wkrp_ref, wcq_ref, qg_ref,
                      wckv_ref, kvg_ref, wqp_ref, wkp_ref, wv_ref, cw_ref, cb_ref,
                      cq_ref, sq_ref, ck_ref, sk_ref,
                      z_ref, xbc_ref, dt_ref, q_ref, k_ref, v_ref, tail_scr):
    @pl.when(pl.program_id(1) == 0)
    def _():
        tail_scr[...] = jnp.zeros_like(tail_scr)

    hn = _rms(h_ref[...], g_ref[...]).astype(BF16)

    def conv_slab(c0):
        cols = slice(c0, c0 + CONV_COLS)
        conv = _causal_conv(_dot(hn, wxbc_ref[:, cols]), tail_scr, cw_ref, cb_ref, cols)
        xbc_ref[:, cols] = (conv * _sigmoid(conv)).astype(BF16)

    def z_slab(c0):
        cols = slice(c0, c0 + Z_COLS)
        z_ref[:, cols] = _dot(hn, wz_ref[:, cols]).astype(BF16)

    cqn = _rms(_dot(hn, wcq_ref[...]), qg_ref[...]).astype(BF16)
    ckvn = _rms(_dot(hn, wckv_ref[...]), kvg_ref[...]).astype(BF16)
    kr_raw = _dot(hn, wkrp_ref[...])
    kr = kr_raw * ck_ref[...] + _rotate_half(kr_raw) * sk_ref[...]
    dt_ref[...] = _dot(hn, wdt_ref[...])
    cq_t = jnp.concatenate([cq_ref[...]] * 2, axis=1)
    sq_t = jnp.concatenate([sq_ref[...]] * 2, axis=1)
    kr2 = jnp.concatenate([kr] * 2, axis=1)

    def head_pair(hd):
        sl = slice(hd * LANES, (hd + 2) * LANES)
        q_raw = _dot(cqn, wqp_ref[:, sl])
        qh = (q_raw * cq_t + _rotate_half(q_raw) * sq_t).astype(BF16)
        kh = (_dot(ckvn, wkp_ref[:, sl]) + kr2).astype(BF16)
        for j in range(2):
            q_ref[hd + j] = qh[:, j * LANES:(j + 1) * LANES]
            k_ref[hd + j] = kh[:, j * LANES:(j + 1) * LANES]

    def v_proj(_):
        v_ref[...] = _dot(ckvn, wv_ref[...]).astype(BF16)

    convs = [(conv_slab, c0) for c0 in range(0, SSD_CONV_CH, CONV_COLS)]
    pairs = [(head_pair, hd) for hd in range(0, MLA_HEADS, 2)]
    heavy = []
    while convs or pairs:
        heavy += convs[:1] + pairs[:1]
        convs, pairs = convs[1:], pairs[1:]
    light = [(v_proj, None)] + [(z_slab, c0) for c0 in range(0, SSD_D_INNER, Z_COLS)]
    order = []
    for i, item in enumerate(heavy):
        order.append(item)
        if (i + 1) % 4 == 0 and len(light) > 1:
            order.append(light.pop(0))
    order += light
    for fn, arg in order:
        fn(arg)


def _proj_even(h3, g, wts, tables):
    b, tp, d = h3.shape
    tm = _pick_tile(tp, (384, 128))
    nt = tp // tm
    n = b * tp
    row3 = lambda bi, i: (bi, i, 0)
    flat = lambda bi, i: (bi * nt + i, 0)
    tab = lambda bi, i: (i, 0)
    in_specs = [pl.BlockSpec((None, tm, d), row3), _const_spec(g.shape)]
    in_specs += [_const_spec(w.shape) for w in wts]
    in_specs += [pl.BlockSpec((tm, LANES), tab) for _ in tables]
    out_shape = (
        jax.ShapeDtypeStruct((n, SSD_D_INNER), BF16),
        jax.ShapeDtypeStruct((n, SSD_CONV_CH), BF16),
        jax.ShapeDtypeStruct((n, LANES), F32),
        jax.ShapeDtypeStruct((MLA_HEADS, n, LANES), BF16),
        jax.ShapeDtypeStruct((MLA_HEADS, n, LANES), BF16),
        jax.ShapeDtypeStruct((n, MLA_HEADS * MLA_V), BF16),
    )
    head3 = lambda bi, i: (0, bi * nt + i, 0)
    out_specs = (
        pl.BlockSpec((tm, SSD_D_INNER), flat),
        pl.BlockSpec((tm, SSD_CONV_CH), flat),
        pl.BlockSpec((tm, LANES), flat),
        pl.BlockSpec((MLA_HEADS, tm, LANES), head3),
        pl.BlockSpec((MLA_HEADS, tm, LANES), head3),
        pl.BlockSpec((tm, MLA_HEADS * MLA_V), flat),
    )
    return pl.pallas_call(
        _proj_even_kernel, grid=(b, nt), in_specs=in_specs, out_specs=out_specs, out_shape=out_shape,
        scratch_shapes=[pltpu.VMEM((SUBLANES, SSD_CONV_CH), F32)],
        compiler_params=_cparams("parallel", "arbitrary"), name="proj_even",
    )(h3, g, *wts, *tables)


def _cumsum_rows(x):
    n = x.shape[0]
    row = lax.broadcasted_iota(jnp.int32, x.shape, 0)
    d = 1
    while d < n:
        x = x + jnp.where(row >= d, pltpu.roll(x, d, 0), 0.0)
        d *= 2
    return x


SSD_BATCH_BLOCK = 4


def _ssd_chunk(c, xbc, dt_raw, zf, dtb, alog, dsk, ng, state_ref, y_scr):
    L = CHUNK
    heads_per_group = SSD_HEADS // SSD_GROUPS
    row = lax.broadcasted_iota(jnp.int32, (L, 1), 0)
    valid = (c * L + row) >= PAD
    act = xbc.astype(F32)
    xs = act[:, :SSD_D_INNER]
    bm = act[:, SSD_D_INNER:SSD_D_INNER + SSD_BC]
    cm = act[:, SSD_D_INNER + SSD_BC:]

    dtv = jnp.where(valid, _softplus(dt_raw + dtb), 0.0)
    a_cs = _cumsum_rows(dtv * (-jnp.exp(alog)))
    a_cs_t = a_cs.T
    dt_t = dtv.T
    tot = a_cs[L - 1:L, :]
    w_t = jnp.exp(a_cs_t[:, L - 1:L] - a_cs_t) * dt_t
    shifted_t = a_cs_t - jnp.log(dt_t)
    cd = jnp.exp(tot)
    causal = lax.broadcasted_iota(jnp.int32, (L, L), 0) >= lax.broadcasted_iota(jnp.int32, (L, L), 1)
    lo = lax.broadcasted_iota(jnp.int32, (L, LANES), 1) < SSD_HEAD_DIM

    for g in range(SSD_GROUPS):
        cc = cm[:, g * SSD_STATE:(g + 1) * SSD_STATE]
        bc = bm[:, g * SSD_STATE:(g + 1) * SSD_STATE]
        bc_t = bc.T
        cb = _dot_nt(cc.astype(BF16), bc.astype(BF16))
        for hp in range(heads_per_group // 2):
            pair = g * (heads_per_group // 2) + hp
            sl = slice(pair * LANES, (pair + 1) * LANES)
            xs_pair = xs[:, sl]
            st_pair = state_ref[:, sl]
            xs_b = xs_pair.astype(BF16)
            rhs = jnp.concatenate([xs_b, st_pair.astype(BF16)], axis=0)
            ys, sts, cds = [], [], []
            for hh in range(2):
                h = 2 * pair + hh
                a_col = jnp.broadcast_to(a_cs[:, h:h + 1], (L, L))
                m = cb * jnp.exp(jnp.where(causal, a_col - shifted_t[h:h + 1, :], -jnp.inf))
                lhs = jnp.concatenate([m, cc * jnp.exp(a_col)], axis=1).astype(BF16)
                ys.append(_dot(lhs, rhs))
                sts.append(_dot((bc_t * w_t[h:h + 1, :]).astype(BF16), xs_b))
                cds.append(cd[:, h:h + 1])
            y_scr[:, sl] = jnp.where(lo, ys[0], ys[1]) + dsk[:, sl] * xs_pair
            state_ref[:, sl] = jnp.where(lo, cds[0], cds[1]) * st_pair + jnp.where(lo, sts[0], sts[1])

    return _rms(y_scr[...] * (zf * _sigmoid(zf)), ng).astype(BF16)


def _ssd_kernel(xbc_ref, dt_ref, z_ref, dtb_ref, alog_ref, dsk_ref, ng_ref, y_ref, state_scr, y_scr):
    c = pl.program_id(1)

    @pl.when(c == 0)
    def _():
        state_scr[...] = jnp.zeros_like(state_scr)

    for bb in range(xbc_ref.shape[0]):
        y_ref[bb] = _ssd_chunk(c, xbc_ref[bb], dt_ref[bb], z_ref[bb].astype(F32), dtb_ref[...], alog_ref[...],
                               dsk_ref[...], ng_ref[...], state_scr.at[bb], y_scr.at[bb])


def _ssd(xbc3, dt3, z3, dtb, alog, dsk, ng):
    b, tp, _ = xbc3.shape
    nc = tp // CHUNK
    gb = _pick_tile(b, (SSD_BATCH_BLOCK, 1))
    row3 = lambda bi, ci: (bi, ci, 0)
    in_specs = [
        pl.BlockSpec((gb, CHUNK, SSD_CONV_CH), row3),
        pl.BlockSpec((gb, CHUNK, LANES), row3),
        pl.BlockSpec((gb, CHUNK, SSD_D_INNER), row3),
    ] + [_const_spec(a.shape) for a in (dtb, alog, dsk, ng)]
    return pl.pallas_call(
        _ssd_kernel, grid=(b // gb, nc), in_specs=in_specs,
        out_specs=pl.BlockSpec((gb, CHUNK, SSD_D_INNER), row3),
        out_shape=jax.ShapeDtypeStruct((b, tp, SSD_D_INNER), BF16),
        scratch_shapes=[pltpu.VMEM((gb, SSD_STATE, SSD_D_INNER), F32),
                        pltpu.VMEM((gb, CHUNK, SSD_D_INNER), F32)],
        compiler_params=_cparams("parallel", "arbitrary"), name="ssd",
    )(xbc3, dt3, z3, dtb, alog, dsk, ng)


ATTN_KV_BLOCK = 256
ATTN_V_ROWS = MLA_V + 16
PLAIN_UNROLLS = (8, 4, 2, 1)
ATTN_COL_SLAB = 256


def _attn_kernel(q_ref, k_ref, v_ref, o_ref, vt_scr, s_scr, bmax_scr, m_scr, acc_scr, *, tq, nq):
    kb = ATTN_KV_BLOCK
    qi = pl.program_id(2)
    q0 = qi * tq
    end = q0 + tq
    n_int = q0 // kb
    n_blocks = (end + kb - 1) // kb
    last_off = end - kb

    @pl.when(qi == 0)
    def _():
        ones = jnp.ones((ATTN_V_ROWS - MLA_V, CHUNK), BF16)
        for c in range(vt_scr.shape[0]):
            vt = v_ref[c * CHUNK:(c + 1) * CHUNK, :].astype(F32).T.astype(BF16)
            for hh in range(2):
                vt_scr[c, hh, :MLA_V, :] = vt[hh * MLA_V:(hh + 1) * MLA_V, :]
                vt_scr[c, hh, MLA_V:, :] = ones

    m_scr[...] = jnp.full_like(m_scr, NEG_BIG)
    acc_scr[...] = jnp.zeros_like(acc_scr)

    def offset(bk):
        return pl.multiple_of(jnp.minimum(bk * kb, last_off), CHUNK)

    levels = tuple(range(0, tq - kb + 1, kb))

    def col_slabs(cs):
        cuts = sorted({cs} | {c for c in range(tq, cs, -ATTN_COL_SLAB)})
        if len(cuts) > 2 and cuts[1] - cuts[0] < ATTN_COL_SLAB // 2:
            del cuts[1]
        return [slice(a, b) for a, b in zip(cuts[:-1], cuts[1:])]

    def qk(hh, bk, masked, cols):
        off = offset(bk)
        s = _dot_nt(k_ref[hh, pl.ds(off, kb), :], q_ref[hh, cols, :])
        if masked:
            key = lax.broadcasted_iota(jnp.int32, s.shape, 0)
            keep = (key - lax.broadcasted_iota(jnp.int32, s.shape, 1)) <= (q0 + cols.start - off)
            first_key = jnp.where(bk == 0, PAD, bk * kb - off)
            s = jnp.where(keep, s, NEG_BIG)
            s = jnp.where(key >= first_key, s, NEG_BIG)
        s_scr[hh, :, cols] = s
        bmax_scr[hh, :, cols] = jnp.broadcast_to(jnp.max(s, axis=0, keepdims=True), (SUBLANES, s.shape[1]))

    def softmax_pv(hh, bk, cols):
        m_prev = m_scr[hh, 0:1, cols]
        m_new = jnp.maximum(m_prev, bmax_scr[hh, 0:1, cols])
        alpha = jnp.exp2(m_prev - m_new)
        m_scr[hh, :, cols] = jnp.broadcast_to(m_new, (SUBLANES, m_new.shape[1]))
        p = jnp.exp2(s_scr[hh, :, cols] - m_new).astype(BF16)
        c0 = offset(bk) // CHUNK
        vt = jnp.concatenate([vt_scr[c0 + c, hh] for c in range(kb // CHUNK)], axis=1)
        acc_scr[hh, :, cols] = alpha * acc_scr[hh, :, cols] + _dot(vt, p)

    def step(bk, masked, cs_prev, cs):
        prev_slabs, cur_slabs = col_slabs(cs_prev), col_slabs(cs)
        for j in range(max(len(prev_slabs), len(cur_slabs))):
            if j < len(cur_slabs):
                qk(0, bk, masked, cur_slabs[j])
            if j < len(prev_slabs):
                softmax_pv(1, bk - 1, prev_slabs[j])
        for j, cols in enumerate(cur_slabs):
            qk(1, bk, masked, cols)
            softmax_pv(0, bk, cols)

    def plain_run(first, count, unroll):
        def trip(i, carry):
            for u in range(unroll):
                step(first + unroll * i + u, False, 0, 0)
            return carry

        trips = count // unroll
        lax.fori_loop(0, trips, trip, 0)
        return first + unroll * trips, count - unroll * trips

    def diagonal_run(residue, skip):
        n_diag = -(-(residue + tq) // kb)
        lvls = [min(max(min(i * kb - residue, tq - kb), 0) // kb, len(levels) - 1) * kb for i in range(n_diag)]
        for i in range(skip, n_diag):
            step(n_int + i, True, lvls[i - 1] if i > 0 else 0, lvls[i])
        for cols in col_slabs(lvls[-1]):
            softmax_pv(1, n_blocks - 1, cols)

    def first_block():
        for cols in col_slabs(0):
            qk(0, 0, True, cols)
        for cols in col_slabs(0):
            qk(1, 0, True, cols)
            softmax_pv(0, 0, cols)

    @pl.when(qi == 0)
    def _():
        first_block()
        diagonal_run(0, 1)

    @pl.when(qi > 0)
    def _():
        first_block()

    first, left = 1, jnp.maximum(n_int - 1, 0)
    for unroll in PLAIN_UNROLLS:
        first, left = plain_run(first, left, unroll)
    for residue in sorted({(i * tq) % kb for i in range(1, nq)}):
        @pl.when(jnp.logical_and(qi > 0, q0 % kb == residue))
        def _():
            diagonal_run(residue, 0)

    out_t = jnp.concatenate([acc_scr[hh, :MLA_V, :] / acc_scr[hh, MLA_V:MLA_V + 1, :] for hh in range(2)],
                            axis=0)
    qpos = q0 + lax.broadcasted_iota(jnp.int32, (1, tq), 1)
    out_t = jnp.where(qpos >= PAD, out_t, 0.0)
    for c in range(tq // CHUNK):
        o_ref[c * CHUNK:(c + 1) * CHUNK, :] = out_t[:, c * CHUNK:(c + 1) * CHUNK].T.astype(BF16)


def _attention(q, k, v, b, tp):
    tq = _pick_tile(tp, (1408, 384, 256))
    nq = tp // tq
    n = b * tp
    in_specs = [
        pl.BlockSpec((2, tq, LANES), lambda bi, hp, qi: (hp, bi * nq + qi, 0)),
        pl.BlockSpec((2, tp, LANES), lambda bi, hp, qi: (hp, bi, 0)),
        pl.BlockSpec((tp, LANES), lambda bi, hp, qi: (bi, hp)),
    ]
    return pl.pallas_call(
        functools.partial(_attn_kernel, tq=tq, nq=nq), grid=(b, MLA_HEADS // 2, nq), in_specs=in_specs,
        out_specs=pl.BlockSpec((tq, LANES), lambda bi, hp, qi: (bi * nq + qi, hp)),
        out_shape=jax.ShapeDtypeStruct((n, MLA_HEADS * MLA_V), BF16),
        scratch_shapes=[pltpu.VMEM((tp // CHUNK, 2, ATTN_V_ROWS, CHUNK), BF16),
                        pltpu.VMEM((2, ATTN_KV_BLOCK, tq), F32), pltpu.VMEM((2, SUBLANES, tq), F32),
                        pltpu.VMEM((2, SUBLANES, tq), F32), pltpu.VMEM((2, ATTN_V_ROWS, tq), F32)],
        compiler_params=_cparams("parallel", "parallel", "arbitrary"), name="attention",
    )(q, k, v)


POST_ROW_BLOCK = 256


def _post_kernel(*refs, n_in):
    h_ref = refs[0]
    a_refs = refs[1:1 + n_in]
    w_refs = refs[1 + n_in:1 + 2 * n_in]
    g_ref = refs[1 + 2 * n_in]
    o_ref = refs[2 + 2 * n_in]
    rows_per_block = min(POST_ROW_BLOCK, h_ref.shape[0])
    for r0 in range(0, h_ref.shape[0], rows_per_block):
        rows = slice(r0, r0 + rows_per_block)
        m = _dot(a_refs[0][rows, :], w_refs[0][...])
        for a_ref, w_ref in zip(a_refs[1:], w_refs[1:]):
            m = m + _dot(a_ref[rows, :], w_ref[...])
        o_ref[rows, :] = h_ref[rows, :] + _rms(m, g_ref[...])


def _post(h2, acts, wts, g):
    n, d = h2.shape
    tm = _pick_tile(n, (1024, 512, 384, 256, 128))
    rows = lambda i: (i, 0)
    in_specs = [pl.BlockSpec((tm, d), rows)]
    in_specs += [pl.BlockSpec((tm, a.shape[1]), rows) for a in acts]
    in_specs += [_const_spec(w.shape) for w in wts] + [_const_spec(g.shape)]
    return pl.pallas_call(
        functools.partial(_post_kernel, n_in=len(acts)), grid=(n // tm,), in_specs=in_specs,
        out_specs=pl.BlockSpec((tm, d), rows), out_shape=jax.ShapeDtypeStruct((n, d), F32),
        compiler_params=_cparams("parallel"), name="mixer_out",
    )(h2, *acts, *wts, g)


def _mlp_kernel(h_ref, g1_ref, wup_ref, wdn_ref, g2_ref, o_ref, *, ff_chunk):
    x = h_ref[...]
    hn = _rms(x, g1_ref[...]).astype(BF16)
    d_ff = wup_ref.shape[1]
    y = None
    for c0 in range(0, d_ff, ff_chunk):
        u = jnp.maximum(_dot(hn, wup_ref[:, c0:c0 + ff_chunk]), 0.0)
        part = _dot((u * u).astype(BF16), wdn_ref[c0:c0 + ff_chunk, :])
        y = part if y is None else y + part
    o_ref[...] = x + _rms(y, g2_ref[...])


def _mlp(h2, g1, wup, wdn, g2):
    n, d = h2.shape
    tm = _pick_tile(n, (512, 256, 128))
    rows = lambda i: (i, 0)
    in_specs = [pl.BlockSpec((tm, d), rows), _const_spec(g1.shape), _const_spec(wup.shape),
                _const_spec(wdn.shape), _const_spec(g2.shape)]
    return pl.pallas_call(
        functools.partial(_mlp_kernel, ff_chunk=1024), grid=(n // tm,), in_specs=in_specs,
        out_specs=pl.BlockSpec((tm, d), rows), out_shape=jax.ShapeDtypeStruct((n, d), F32),
        compiler_params=_cparams("parallel"), name="mlp",
    )(h2, g1, wup, wdn, g2)


def _mlp_last(h3, g1, wup, wdn, g2, lead):
    b, tp, d = h3.shape
    seq = tp - lead
    tm = _pick_tile(seq, (512, 256, 128))
    in_specs = [pl.BlockSpec((pl.Element(tm), pl.Element(d)), lambda bi, i: (pl.multiple_of(bi * tp + lead + i * tm, CHUNK), 0)),
                _const_spec(g1.shape), _const_spec(wup.shape), _const_spec(wdn.shape), _const_spec(g2.shape)]
    return pl.pallas_call(
        functools.partial(_mlp_kernel, ff_chunk=1024), grid=(b, seq // tm), in_specs=in_specs,
        out_specs=pl.BlockSpec((None, tm, d), lambda bi, i: (bi, i, 0)),
        out_shape=jax.ShapeDtypeStruct((b, seq, d), F32),
        compiler_params=_cparams("parallel", "parallel"), name="mlp_last",
    )(h3.reshape(b * tp, d), g1, wup, wdn, g2)


SEG_LEN = CHUNK // SUBLANES


def _interleave_matrix():
    p = np.zeros((CHUNK, CHUNK), np.float32)
    for j in range(SEG_LEN):
        for s in range(SUBLANES):
            p[SUBLANES * j + s, SEG_LEN * s + j] = 1.0
    return p


def _reorder_rows(perm, x):
    tiles = [_dot(perm, x[t0:t0 + CHUNK]).astype(BF16) for t0 in range(0, x.shape[0], CHUNK)]
    return tiles[0] if len(tiles) == 1 else jnp.concatenate(tiles, axis=0)


def _causal_conv_interleaved(pre, tail_scr, w_ref, b_ref, cols):
    k = w_ref.shape[0]
    halo = (k - 1) * SUBLANES
    width = pre.shape[1]
    first_sub = lax.broadcasted_iota(jnp.int32, (k - 1, SUBLANES, width), 1) == 0
    prev_last = tail_scr[:, cols]
    outs = []
    for t0 in range(0, pre.shape[0], CHUNK):
        tile = pre[t0:t0 + CHUNK]
        last = tile[CHUNK - halo:]
        fix = jnp.where(first_sub, pltpu.roll(prev_last.reshape(k - 1, SUBLANES, width), 1, 1),
                        pltpu.roll(last.reshape(k - 1, SUBLANES, width), 1, 1)).reshape(halo, width)
        out = b_ref[:, cols]
        for i in range(k):
            shift = k - 1 - i
            if shift == 0:
                xs = tile
            else:
                xs = jnp.concatenate([fix[halo - shift * SUBLANES:], tile[:CHUNK - shift * SUBLANES]], axis=0)
            out = out + xs * w_ref[i:i + 1, cols]
        outs.append(out)
        prev_last = last
    tail_scr[:, cols] = prev_last
    return jnp.concatenate(outs, axis=0)


def _proj_odd_kernel(h_ref, g_ref, wx_ref, wy_ref, cw_ref, cb_ref, perm_ref, x_ref, gate_ref, tail_scr):
    @pl.when(pl.program_id(0) == 0)
    def _():
        tail_scr[...] = jnp.zeros_like(tail_scr)

    hn = _reorder_rows(perm_ref[...], _rms(h_ref[...], g_ref[...]).astype(BF16))
    for c0 in range(0, LRU_WIDTH, LRU_CONV_COLS):
        cols = slice(c0, c0 + LRU_CONV_COLS)
        x_ref[:, cols] = _causal_conv_interleaved(_dot(hn, wx_ref[:, cols]), tail_scr, cw_ref, cb_ref,
                                                  cols).astype(BF16)
        y = _dot(hn, wy_ref[:, cols])
        gelu = 0.5 * y * (1.0 + jnp.tanh(np.sqrt(2.0 / np.pi).astype(np.float32) * (y + 0.044715 * (y * y * y))))
        gate_ref[:, cols] = gelu.astype(BF16)


def _proj_odd(h2, g, wx, wy, cw, cb, perm):
    n, d = h2.shape
    tm = _pick_tile(n, (1024, 512, 384, 256, 128))
    rows = lambda i: (i, 0)
    w = wx.shape[1]
    return pl.pallas_call(
        _proj_odd_kernel, grid=(n // tm,),
        in_specs=[pl.BlockSpec((tm, d), rows)] + [_const_spec(a.shape) for a in (g, wx, wy, cw, cb, perm)],
        out_specs=(pl.BlockSpec((tm, w), rows), pl.BlockSpec((tm, w), rows)),
        out_shape=(jax.ShapeDtypeStruct((n, w), BF16), jax.ShapeDtypeStruct((n, w), BF16)),
        scratch_shapes=[pltpu.VMEM(((cw.shape[0] - 1) * SUBLANES, w), F32)],
        compiler_params=_cparams("arbitrary"), name="proj_odd",
    )(h2, g, wx, wy, cw, cb, perm)


LRU_BATCH_BLOCK = 8


def _rglru_kernel(x_ref, gate_ref, wa_ref, ba_ref, wi_ref, bi_ref, lam_ref, unperm_ref, o_ref, h_scr):
    t = pl.program_id(1)

    @pl.when(t == 0)
    def _():
        h_scr[...] = jnp.zeros_like(h_scr)

    for bb in range(x_ref.shape[0]):
        _rglru_tile(t, x_ref.at[bb], gate_ref.at[bb], wa_ref, ba_ref, wi_ref, bi_ref, lam_ref, unperm_ref,
                    o_ref.at[bb], h_scr.at[bb])


def _rglru_tile(t, x_ref, gate_ref, wa_ref, ba_ref, wi_ref, bi_ref, lam_ref, unperm_ref, o_ref, h_scr):
    L = x_ref.shape[0]
    xr = x_ref[...].astype(F32)
    rs, gs = [], []
    for nb in range(LRU_BLOCKS):
        sl = slice(nb * LRU_BLOCK, (nb + 1) * LRU_BLOCK)
        xb = x_ref[:, sl]
        rs.append(_dot(xb, wa_ref[nb]))
        gs.append(_dot(xb, wi_ref[nb]))
    r = _sigmoid(jnp.concatenate(rs, axis=1) + ba_ref[...])
    ig = _sigmoid(jnp.concatenate(gs, axis=1) + bi_ref[...])
    log_a = (-LRU_C) * r * _softplus(-lam_ref[...])
    a = jnp.exp(log_a)
    row = lax.broadcasted_iota(jnp.int32, (L, 1), 0)
    valid = (t * L + SEG_LEN * (row % SUBLANES) + row // SUBLANES) >= PAD
    om = 1.0 - a * a
    root = jnp.where(om > 0.0, om * lax.rsqrt(om), 0.0)
    u = jnp.where(valid, root * (ig * xr), 0.0)

    slab = lambda v, j: v[j * SUBLANES:(j + 1) * SUBLANES, :]
    h_loc = [slab(u, 0)]
    prod = [slab(a, 0)]
    for j in range(1, SEG_LEN):
        h_loc.append(slab(a, j) * h_loc[-1] + slab(u, j))
        prod.append(slab(a, j) * prod[-1])
    carry_in = h_scr[0:1, :]
    carries = []
    for s in range(SUBLANES):
        carries.append(carry_in)
        carry_in = h_loc[-1][s:s + 1, :] + prod[-1][s:s + 1, :] * carry_in
    h_scr[...] = jnp.broadcast_to(carry_in, h_scr.shape)
    carry = jnp.concatenate(carries, axis=0)
    hs = jnp.concatenate([h_loc[j] + prod[j] * carry for j in range(SEG_LEN)], axis=0)
    o_ref[...] = _reorder_rows(unperm_ref[...], (hs * gate_ref[...].astype(F32)).astype(BF16))


def _rglru(x3, gate3, wa, ba, wi, bi, lam, unperm):
    b, tp, w = x3.shape
    tt = CHUNK
    gb = _pick_tile(b, (LRU_BATCH_BLOCK, 1))
    row3 = lambda bi_, ti: (bi_, ti, 0)
    in_specs = [pl.BlockSpec((gb, tt, w), row3), pl.BlockSpec((gb, tt, w), row3)]
    in_specs += [_const_spec(a.shape) for a in (wa, ba, wi, bi, lam, unperm)]
    return pl.pallas_call(
        _rglru_kernel, grid=(b // gb, tp // tt), in_specs=in_specs,
        out_specs=pl.BlockSpec((gb, tt, w), row3),
        out_shape=jax.ShapeDtypeStruct((b, tp, w), BF16),
        scratch_shapes=[pltpu.VMEM((gb, SUBLANES, w), F32)],
        compiler_params=_cparams("parallel", "arbitrary"), name="rglru",
    )(x3, gate3, wa, ba, wi, bi, lam, unperm)


def _rope_tables(tp):
    f32 = np.float32
    inv = f32(ROPE_BASE) ** (-np.arange(0, MLA_ROPE, 2, dtype=f32) / f32(MLA_ROPE))
    pos = np.maximum(np.arange(tp, dtype=f32) - f32(PAD), f32(0))
    ang = (pos[:, None] * inv[None, :]).astype(f32)
    cos, sin = np.cos(ang).astype(f32), np.sin(ang).astype(f32)
    zeros_lo = np.zeros((tp, MLA_NOPE), f32)
    zeros_hi = np.zeros((tp, LANES - MLA_NOPE - MLA_ROPE), f32)
    ck = np.concatenate([zeros_lo, cos, cos, zeros_hi], axis=1)
    sk = np.concatenate([zeros_lo, sin, sin, zeros_hi], axis=1)
    scale = f32((MLA_NOPE + MLA_ROPE) ** -0.5 * np.log2(np.e))
    cq = np.concatenate([np.ones((tp, MLA_NOPE), f32), cos, cos, zeros_hi], axis=1) * scale
    sq = sk * scale
    return tuple(jnp.asarray(t) for t in (cq, sq, ck, sk))


def _even_weights(w_in, w_q_up, w_kv_up):
    offs = np.cumsum((SSD_D_INNER, SSD_CONV_CH, SSD_HEADS, MLA_Q_RANK, MLA_KV_RANK, MLA_ROPE))
    wz = w_in[:, :offs[0]]
    wxbc = w_in[:, offs[0]:offs[1]]
    wdt = w_in[:, offs[1]:offs[2]]
    wcq = w_in[:, offs[2]:offs[3]]
    wckv = w_in[:, offs[3]:offs[4]]
    wkr = w_in[:, offs[4]:offs[5]]
    hi_pad = LANES - MLA_NOPE - MLA_ROPE
    wdt_p = jnp.pad(wdt, ((0, 0), (0, LANES - SSD_HEADS)))
    wkr_p = jnp.pad(wkr, ((0, 0), (MLA_NOPE, hi_pad)))
    wq = w_q_up.reshape(MLA_Q_RANK, MLA_HEADS, MLA_NOPE + MLA_ROPE)
    wq_p = jnp.pad(wq, ((0, 0), (0, 0), (0, hi_pad))).reshape(MLA_Q_RANK, MLA_HEADS * LANES)
    wkv = w_kv_up.reshape(MLA_KV_RANK, MLA_HEADS, MLA_NOPE + MLA_V)
    wk_p = jnp.pad(wkv[..., :MLA_NOPE], ((0, 0), (0, 0), (0, LANES - MLA_NOPE))).reshape(
        MLA_KV_RANK, MLA_HEADS * LANES)
    wv = wkv[..., MLA_NOPE:].reshape(MLA_KV_RANK, MLA_HEADS * MLA_V)
    bf = lambda a: a.astype(BF16)
    return dict(wz=bf(wz), wxbc=bf(wxbc), wdt=bf(wdt_p), wkr_p=bf(wkr_p), wcq=bf(wcq),
                wckv=bf(wckv), wq_p=bf(wq_p), wk_p=bf(wk_p), wv=bf(wv))


def _row(a):
    return a.reshape(1, -1).astype(F32)


def kernel(x, meta_tokens, mix_pre_g, mix_post_g, mlp_pre_g, mlp_post_g, w_up, w_down, w_in, ssd_conv_w,
           ssd_conv_b, ssd_dt_bias, ssd_a_log, ssd_d, ssd_norm_g, mla_q_norm_g, mla_w_q_up, mla_kv_norm_g,
           mla_w_kv_up, w_out_ab, rg_w_x, rg_w_y, rg_conv_w, rg_conv_b, rg_w_a, rg_b_a, rg_w_i, rg_b_i,
           rg_lambda, rg_w_out):
    b, seq, d = x.shape
    depth = mix_pre_g.shape[0]
    tp = PAD + N_META + seq
    n = b * tp
    meta = jnp.broadcast_to(meta_tokens[None].astype(x.dtype), (b, N_META, d))
    h = jnp.concatenate([jnp.zeros((b, PAD, d), x.dtype), meta, x], axis=1).reshape(n, d)
    tables = _rope_tables(tp)

    for layer in range(depth):
        if layer % 2 == 0:
            e = layer // 2
            w = _even_weights(w_in[e], mla_w_q_up[e], mla_w_kv_up[e])
            wts = (w["wz"], w["wxbc"], w["wdt"], w["wkr_p"], w["wcq"], _row(mla_q_norm_g[e]),
                   w["wckv"], _row(mla_kv_norm_g[e]), w["wq_p"], w["wk_p"], w["wv"],
                   ssd_conv_w[e].astype(F32), _row(ssd_conv_b[e]))
            z, xbc, dt, q, k, v = _proj_even(h.reshape(b, tp, d), _row(mix_pre_g[layer]), wts, tables)
            lane_pad = lambda a: jnp.pad(_row(a), ((0, 0), (0, LANES - SSD_HEADS)))
            y_ssd = _ssd(xbc.reshape(b, tp, SSD_CONV_CH), dt.reshape(b, tp, LANES), z.reshape(b, tp, SSD_D_INNER),
                         lane_pad(ssd_dt_bias[e]), lane_pad(ssd_a_log[e]),
                         _row(jnp.repeat(ssd_d[e], SSD_HEAD_DIM)), _row(ssd_norm_g[e]))
            y_att = _attention(q, k, v, b, tp)
            wo = w_out_ab[e].astype(BF16)
            h = _post(h, (y_ssd.reshape(n, SSD_D_INNER), y_att), (wo[:SSD_D_INNER], wo[SSD_D_INNER:]),
                      _row(mix_post_g[layer]))
        else:
            o = layer // 2
            perm = _interleave_matrix()
            xr, gate = _proj_odd(h, _row(mix_pre_g[layer]), rg_w_x[o].astype(BF16), rg_w_y[o].astype(BF16),
                                 rg_conv_w[o].astype(F32), _row(rg_conv_b[o]), jnp.asarray(perm, BF16))
            y = _rglru(xr.reshape(b, tp, LRU_WIDTH), gate.reshape(b, tp, LRU_WIDTH), rg_w_a[o].astype(BF16),
                       _row(rg_b_a[o]), rg_w_i[o].astype(BF16), _row(rg_b_i[o]), _row(rg_lambda[o]),
                       jnp.asarray(perm.T, BF16))
            h = _post(h, (y.reshape(n, LRU_WIDTH),), (rg_w_out[o].astype(BF16),), _row(mix_post_g[layer]))
        mlp_args = (_row(mlp_pre_g[layer]), w_up[layer].astype(BF16), w_down[layer].astype(BF16),
                    _row(mlp_post_g[layer]))
        if layer == depth - 1:
            return _mlp_last(h.reshape(b, tp, d), *mlp_args, PAD + N_META)
        h = _mlp(h, *mlp_args)
```

```python
import functools

import jax
import jax.numpy as jnp
import numpy as np
from jax import lax
from jax.experimental import pallas as pl
from jax.experimental.pallas import tpu as pltpu

F32 = jnp.float32
BF16 = jnp.bfloat16

N_META = 16
CHUNK = 128
PAD = CHUNK - N_META
EPS = 1e-6
SSD_HEADS = 16
SSD_HEAD_DIM = 64
SSD_D_INNER = SSD_HEADS * SSD_HEAD_DIM
SSD_GROUPS = 2
SSD_STATE = 128
SSD_BC = SSD_GROUPS * SSD_STATE
SSD_CONV_CH = SSD_D_INNER + 2 * SSD_BC
MLA_HEADS = 16
MLA_NOPE = 64
MLA_ROPE = 32
MLA_V = 64
MLA_Q_RANK = 384
MLA_KV_RANK = 256
ROPE_BASE = 10000.0
LRU_BLOCKS = 10
LRU_BLOCK = 128
LRU_WIDTH = LRU_BLOCKS * LRU_BLOCK
LRU_C = 8.0
LRU_CONV_COLS = 256
LANES = 128
SUBLANES = 8
VMEM_LIMIT_BYTES = 56 * 1024 * 1024
NEG_BIG = -1e30


def _cparams(*sem):
    return pltpu.CompilerParams(dimension_semantics=sem, vmem_limit_bytes=VMEM_LIMIT_BYTES)


def _pick_tile(n, candidates):
    for c in candidates:
        if n % c == 0:
            return c
    raise ValueError(f"no tile in {candidates} divides {n}")


def _const_spec(shape):
    nd = len(shape)
    return pl.BlockSpec(shape, lambda *_: (0,) * nd)


def _rms(x, g):
    ms = jnp.mean(x * x, axis=-1, keepdims=True)
    return x * lax.rsqrt(ms + EPS) * g


def _sigmoid(x):
    return 1.0 / (1.0 + jnp.exp(-x))


def _softplus(x):
    return jnp.maximum(x, 0.0) + jnp.log(1.0 + jnp.exp(-jnp.abs(x)))


def _dot(a, b):
    return jnp.dot(a, b, preferred_element_type=F32)


def _dot_nt(a, b):
    return lax.dot_general(a, b, (((1,), (1,)), ((), ())), preferred_element_type=F32)


def _shift_rows(cur, tail, k):
    if k == 0:
        return cur
    rolled = pltpu.roll(cur, k, 0)
    row = lax.broadcasted_iota(jnp.int32, tail.shape, 0)
    top = jnp.where(row < k, pltpu.roll(tail, k, 0), rolled[0:SUBLANES])
    return jnp.concatenate([top, rolled[SUBLANES:]], axis=0)


def _causal_conv(cur, tail_scr, w_ref, b_ref, cols):
    k = w_ref.shape[0]
    tail = tail_scr[:, cols]
    out = b_ref[:, cols]
    for i in range(k):
        out = out + _shift_rows(cur, tail, k - 1 - i) * w_ref[i:i + 1, cols]
    tail_scr[:, cols] = cur[cur.shape[0] - SUBLANES:]
    return out


CONV_COLS = 256
Z_COLS = 256


def _rotate_half(x):
    half = MLA_ROPE // 2
    lane = lax.broadcasted_iota(jnp.int32, x.shape, 1) % LANES
    from_right = pltpu.roll(x, x.shape[1] - half, 1)
    from_left = pltpu.roll(x, half, 1)
    return jnp.where(lane < MLA_NOPE + half, -from_right, from_left)


def _proj_even_kernel(h_ref, g_ref, wz_ref, wxbc_ref, wdt_ref, wkrp_ref, wcq_ref, qg_ref,
                      wckv_ref, kvg_ref, wqp_ref, wkp_ref, wv_ref, cw_ref, cb_ref,
                      cq_ref, sq_ref, ck_ref, sk_ref,
                      z_ref, xbc_ref, dt_ref, q_ref, k_ref, v_ref, tail_scr):
    @pl.when(pl.program_id(1) == 0)
    def _():
        tail_scr[...] = jnp.zeros_like(tail_scr)

    hn = _rms(h_ref[...], g_ref[...]).astype(BF16)

    def conv_slab(c0):
        cols = slice(c0, c0 + CONV_COLS)
        conv = _causal_conv(_dot(hn, wxbc_ref[:, cols]), tail_scr, cw_ref, cb_ref, cols)
        xbc_ref[:, cols] = (conv * _sigmoid(conv)).astype(BF16)

    def z_slab(c0):
        cols = slice(c0, c0 + Z_COLS)
        z_ref[:, cols] = _dot(hn, wz_ref[:, cols]).astype(BF16)

    cqn = _rms(_dot(hn, wcq_ref[...]), qg_ref[...]).astype(BF16)
    ckvn = _rms(_dot(hn, wckv_ref[...]), kvg_ref[...]).astype(BF16)
    kr_raw = _dot(hn, wkrp_ref[...])
    kr = kr_raw * ck_ref[...] + _rotate_half(kr_raw) * sk_ref[...]
    dt_ref[...] = _dot(hn, wdt_ref[...])
    cq_t = jnp.concatenate([cq_ref[...]] * 2, axis=1)
    sq_t = jnp.concatenate([sq_ref[...]] * 2, axis=1)
    kr2 = jnp.concatenate([kr] * 2, axis=1)

    def head_pair(hd):
        sl = slice(hd * LANES, (hd + 2) * LANES)
        q_raw = _dot(cqn, wqp_ref[:, sl])
        qh = (q_raw * cq_t + _rotate_half(q_raw) * sq_t).astype(BF16)
        kh = (_dot(ckvn, wkp_ref[:, sl]) + kr2).astype(BF16)
        for j in range(2):
            q_ref[hd + j] = qh[:, j * LANES:(j + 1) * LANES]
            k_ref[hd + j] = kh[:, j * LANES:(j + 1) * LANES]

    def v_proj(_):
        v_ref[...] = _dot(ckvn, wv_ref[...]).astype(BF16)

    convs = [(conv_slab, c0) for c0 in range(0, SSD_CONV_CH, CONV_COLS)]
    pairs = [(head_pair, hd) for hd in range(0, MLA_HEADS, 2)]
    heavy = []
    while convs or pairs:
        heavy += convs[:1] + pairs[:1]
        convs, pairs = convs[1:], pairs[1:]
    light = [(v_proj, None)] + [(z_slab, c0) for c0 in range(0, SSD_D_INNER, Z_COLS)]
    order = []
    for i, item in enumerate(heavy):
        order.append(item)
        if (i + 1) % 4 == 0 and len(light) > 1:
            order.append(light.pop(0))
    order += light
    for fn, arg in order:
        fn(arg)


def _proj_even(h3, g, wts, tables):
    b, tp, d = h3.shape
    tm = _pick_tile(tp, (384, 128))
    nt = tp // tm
    n = b * tp
    row3 = lambda bi, i: (bi, i, 0)
    flat = lambda bi, i: (bi * nt + i, 0)
    tab = lambda bi, i: (i, 0)
    in_specs = [pl.BlockSpec((None, tm, d), row3), _const_spec(g.shape)]
    in_specs += [_const_spec(w.shape) for w in wts]
    in_specs += [pl.BlockSpec((tm, LANES), tab) for _ in tables]
    out_shape = (
        jax.ShapeDtypeStruct((n, SSD_D_INNER), BF16),
        jax.ShapeDtypeStruct((n, SSD_CONV_CH), BF16),
        jax.ShapeDtypeStruct((n, LANES), F32),
        jax.ShapeDtypeStruct((MLA_HEADS, n, LANES), BF16),
        jax.ShapeDtypeStruct((MLA_HEADS, n, LANES), BF16),
        jax.ShapeDtypeStruct((n, MLA_HEADS * MLA_V), BF16),
    )
    head3 = lambda bi, i: (0, bi * nt + i, 0)
    out_specs = (
        pl.BlockSpec((tm, SSD_D_INNER), flat),
        pl.BlockSpec((tm, SSD_CONV_CH), flat),
        pl.BlockSpec((tm, LANES), flat),
        pl.BlockSpec((MLA_HEADS, tm, LANES), head3),
        pl.BlockSpec((MLA_HEADS, tm, LANES), head3),
        pl.BlockSpec((tm, MLA_HEADS * MLA_V), flat),
    )
    return pl.pallas_call(
        _proj_even_kernel, grid=(b, nt), in_specs=in_specs, out_specs=out_specs, out_shape=out_shape,
        scratch_shapes=[pltpu.VMEM((SUBLANES, SSD_CONV_CH), F32)],
        compiler_params=_cparams("parallel", "arbitrary"), name="proj_even",
    )(h3, g, *wts, *tables)


def _cumsum_rows(x):
    n = x.shape[0]
    row = lax.broadcasted_iota(jnp.int32, x.shape, 0)
    d = 1
    while d < n:
        x = x + jnp.where(row >= d, pltpu.roll(x, d, 0), 0.0)
        d *= 2
    return x


SSD_BATCH_BLOCK = 4


def _ssd_chunk(c, xbc, dt_raw, zf, dtb, alog, dsk, ng, state_ref, y_scr):
    L = CHUNK
    heads_per_group = SSD_HEADS // SSD_GROUPS
    row = lax.broadcasted_iota(jnp.int32, (L, 1), 0)
    valid = (c * L + row) >= PAD
    act = xbc.astype(F32)
    xs = act[:, :SSD_D_INNER]
    bm = act[:, SSD_D_INNER:SSD_D_INNER + SSD_BC]
    cm = act[:, SSD_D_INNER + SSD_BC:]

    dtv = jnp.where(valid, _softplus(dt_raw + dtb), 0.0)
    a_cs = _cumsum_rows(dtv * (-jnp.exp(alog)))
    a_cs_t = a_cs.T
    dt_t = dtv.T
    tot = a_cs[L - 1:L, :]
    w_t = jnp.exp(a_cs_t[:, L - 1:L] - a_cs_t) * dt_t
    shifted_t = a_cs_t - jnp.log(dt_t)
    cd = jnp.exp(tot)
    causal = lax.broadcasted_iota(jnp.int32, (L, L), 0) >= lax.broadcasted_iota(jnp.int32, (L, L), 1)
    lo = lax.broadcasted_iota(jnp.int32, (L, LANES), 1) < SSD_HEAD_DIM

    for g in range(SSD_GROUPS):
        cc = cm[:, g * SSD_STATE:(g + 1) * SSD_STATE]
        bc = bm[:, g * SSD_STATE:(g + 1) * SSD_STATE]
        bc_t = bc.T
        cb = _dot_nt(cc.astype(BF16), bc.astype(BF16))
        for hp in range(heads_per_group // 2):
            pair = g * (heads_per_group // 2) + hp
            sl = slice(pair * LANES, (pair + 1) * LANES)
            xs_pair = xs[:, sl]
            st_pair = state_ref[:, sl]
            xs_b = xs_pair.astype(BF16)
            rhs = jnp.concatenate([xs_b, st_pair.astype(BF16)], axis=0)
            ys, sts, cds = [], [], []
            for hh in range(2):
                h = 2 * pair + hh
                a_col = jnp.broadcast_to(a_cs[:, h:h + 1], (L, L))
                m = cb * jnp.exp(jnp.where(causal, a_col - shifted_t[h:h + 1, :], -jnp.inf))
                lhs = jnp.concatenate([m, cc * jnp.exp(a_col)], axis=1).astype(BF16)
                ys.append(_dot(lhs, rhs))
                sts.append(_dot((bc_t * w_t[h:h + 1, :]).astype(BF16), xs_b))
                cds.append(cd[:, h:h + 1])
            y_scr[:, sl] = jnp.where(lo, ys[0], ys[1]) + dsk[:, sl] * xs_pair
            state_ref[:, sl] = jnp.where(lo, cds[0], cds[1]) * st_pair + jnp.where(lo, sts[0], sts[1])

    return _rms(y_scr[...] * (zf * _sigmoid(zf)), ng).astype(BF16)


def _ssd_kernel(xbc_ref, dt_ref, z_ref, dtb_ref, alog_ref, dsk_ref, ng_ref, y_ref, state_scr, y_scr):
    c = pl.program_id(1)

    @pl.when(c == 0)
    def _():
        state_scr[...] = jnp.zeros_like(state_scr)

    for bb in range(xbc_ref.shape[0]):
        y_ref[bb] = _ssd_chunk(c, xbc_ref[bb], dt_ref[bb], z_ref[bb].astype(F32), dtb_ref[...], alog_ref[...],
                               dsk_ref[...], ng_ref[...], state_scr.at[bb], y_scr.at[bb])


def _ssd(xbc3, dt3, z3, dtb, alog, dsk, ng):
    b, tp, _ = xbc3.shape
    nc = tp // CHUNK
    gb = _pick_tile(b, (SSD_BATCH_BLOCK, 1))
    row3 = lambda bi, ci: (bi, ci, 0)
    in_specs = [
        pl.BlockSpec((gb, CHUNK, SSD_CONV_CH), row3),
        pl.BlockSpec((gb, CHUNK, LANES), row3),
        pl.BlockSpec((gb, CHUNK, SSD_D_INNER), row3),
    ] + [_const_spec(a.shape) for a in (dtb, alog, dsk, ng)]
    return pl.pallas_call(
        _ssd_kernel, grid=(b // gb, nc), in_specs=in_specs,
        out_specs=pl.BlockSpec((gb, CHUNK, SSD_D_INNER), row3),
        out_shape=jax.ShapeDtypeStruct((b, tp, SSD_D_INNER), BF16),
        scratch_shapes=[pltpu.VMEM((gb, SSD_STATE, SSD_D_INNER), F32),
                        pltpu.VMEM((gb, CHUNK, SSD_D_INNER), F32)],
        compiler_params=_cparams("parallel", "arbitrary"), name="ssd",
    )(xbc3, dt3, z3, dtb, alog, dsk, ng)


ATTN_KV_BLOCK = 256
ATTN_V_ROWS = MLA_V + 16
PLAIN_UNROLLS = (8, 4, 2, 1)
ATTN_COL_SLAB = 256


def _attn_kernel(q_ref, k_ref, v_ref, o_ref, vt_scr, s_scr, bmax_scr, m_scr, acc_scr, *, tq, nq):
    kb = ATTN_KV_BLOCK
    qi = pl.program_id(2)
    q0 = qi * tq
    end = q0 + tq
    n_int = q0 // kb
    n_blocks = (end + kb - 1) // kb
    last_off = end - kb

    @pl.when(qi == 0)
    def _():
        ones = jnp.ones((ATTN_V_ROWS - MLA_V, CHUNK), BF16)
        for c in range(vt_scr.shape[0]):
            vt = v_ref[c * CHUNK:(c + 1) * CHUNK, :].astype(F32).T.astype(BF16)
            for hh in range(2):
                vt_scr[c, hh, :MLA_V, :] = vt[hh * MLA_V:(hh + 1) * MLA_V, :]
                vt_scr[c, hh, MLA_V:, :] = ones

    m_scr[...] = jnp.full_like(m_scr, NEG_BIG)
    acc_scr[...] = jnp.zeros_like(acc_scr)

    def offset(bk):
        return pl.multiple_of(jnp.minimum(bk * kb, last_off), CHUNK)

    levels = tuple(range(0, tq - kb + 1, kb))

    def col_slabs(cs):
        cuts = sorted({cs} | {c for c in range(tq, cs, -ATTN_COL_SLAB)})
        if len(cuts) > 2 and cuts[1] - cuts[0] < ATTN_COL_SLAB // 2:
            del cuts[1]
        return [slice(a, b) for a, b in zip(cuts[:-1], cuts[1:])]

    def qk(hh, bk, masked, cols):
        off = offset(bk)
        s = _dot_nt(k_ref[hh, pl.ds(off, kb), :], q_ref[hh, cols, :])
        if masked:
            key = lax.broadcasted_iota(jnp.int32, s.shape, 0)
            keep = (key - lax.broadcasted_iota(jnp.int32, s.shape, 1)) <= (q0 + cols.start - off)
            first_key = jnp.where(bk == 0, PAD, bk * kb - off)
            s = jnp.where(keep, s, NEG_BIG)
            s = jnp.where(key >= first_key, s, NEG_BIG)
        s_scr[hh, :, cols] = s
        bmax_scr[hh, :, cols] = jnp.broadcast_to(jnp.max(s, axis=0, keepdims=True), (SUBLANES, s.shape[1]))

    def softmax_pv(hh, bk, cols):
        m_prev = m_scr[hh, 0:1, cols]
        m_new = jnp.maximum(m_prev, bmax_scr[hh, 0:1, cols])
        alpha = jnp.exp2(m_prev - m_new)
        m_scr[hh, :, cols] = jnp.broadcast_to(m_new, (SUBLANES, m_new.shape[1]))
        p = jnp.exp2(s_scr[hh, :, cols] - m_new).astype(BF16)
        c0 = offset(bk) // CHUNK
        vt = jnp.concatenate([vt_scr[c0 + c, hh] for c in range(kb // CHUNK)], axis=1)
        acc_scr[hh, :, cols] = alpha * acc_scr[hh, :, cols] + _dot(vt, p)

    def step(bk, masked, cs_prev, cs):
        prev_slabs, cur_slabs = col_slabs(cs_prev), col_slabs(cs)
        for j in range(max(len(prev_slabs), len(cur_slabs))):
            if j < len(cur_slabs):
                qk(0, bk, masked, cur_slabs[j])
            if j < len(prev_slabs):
                softmax_pv(1, bk - 1, prev_slabs[j])
        for j, cols in enumerate(cur_slabs):
            qk(1, bk, masked, cols)
            softmax_pv(0, bk, cols)

    def plain_run(first, count, unroll):
        def trip(i, carry):
            for u in range(unroll):
                step(first + unroll * i + u, False, 0, 0)
            return carry

        trips = count // unroll
        lax.fori_loop(0, trips, trip, 0)
        return first + unroll * trips, count - unroll * trips

    def diagonal_run(residue, skip):
        n_diag = -(-(residue + tq) // kb)
        lvls = [min(max(min(i * kb - residue, tq - kb), 0) // kb, len(levels) - 1) * kb for i in range(n_diag)]
        for i in range(skip, n_diag):
            step(n_int + i, True, lvls[i - 1] if i > 0 else 0, lvls[i])
        for cols in col_slabs(lvls[-1]):
            softmax_pv(1, n_blocks - 1, cols)

    def first_block():
        for cols in col_slabs(0):
            qk(0, 0, True, cols)
        for cols in col_slabs(0):
            qk(1, 0, True, cols)
            softmax_pv(0, 0, cols)

    @pl.when(qi == 0)
    def _():
        first_block()
        diagonal_run(0, 1)

    @pl.when(qi > 0)
    def _():
        first_block()

    first, left = 1, jnp.maximum(n_int - 1, 0)
    for unroll in PLAIN_UNROLLS:
        first, left = plain_run(first, left, unroll)
    for residue in sorted({(i * tq) % kb for i in range(1, nq)}):
        @pl.when(jnp.logical_and(qi > 0, q0 % kb == residue))
        def _():
            diagonal_run(residue, 0)

    out_t = jnp.concatenate([acc_scr[hh, :MLA_V, :] / acc_scr[hh, MLA_V:MLA_V + 1, :] for hh in range(2)],
                            axis=0)
    qpos = q0 + lax.broadcasted_iota(jnp.int32, (1, tq), 1)
    out_t = jnp.where(qpos >= PAD, out_t, 0.0)
    for c in range(tq // CHUNK):
        o_ref[c * CHUNK:(c + 1) * CHUNK, :] = out_t[:, c * CHUNK:(c + 1) * CHUNK].T.astype(BF16)


def _attention(q, k, v, b, tp):
    tq = _pick_tile(tp, (1408, 384, 256))
    nq = tp // tq
    n = b * tp
    in_specs = [
        pl.BlockSpec((2, tq, LANES), lambda bi, hp, qi: (hp, bi * nq + qi, 0)),
        pl.BlockSpec((2, tp, LANES), lambda bi, hp, qi: (hp, bi, 0)),
        pl.BlockSpec((tp, LANES), lambda bi, hp, qi: (bi, hp)),
    ]
    return pl.pallas_call(
        functools.partial(_attn_kernel, tq=tq, nq=nq), grid=(b, MLA_HEADS // 2, nq), in_specs=in_specs,
        out_specs=pl.BlockSpec((tq, LANES), lambda bi, hp, qi: (bi * nq + qi, hp)),
        out_shape=jax.ShapeDtypeStruct((n, MLA_HEADS * MLA_V), BF16),
        scratch_shapes=[pltpu.VMEM((tp // CHUNK, 2, ATTN_V_ROWS, CHUNK), BF16),
                        pltpu.VMEM((2, ATTN_KV_BLOCK, tq), F32), pltpu.VMEM((2, SUBLANES, tq), F32),
                        pltpu.VMEM((2, SUBLANES, tq), F32), pltpu.VMEM((2, ATTN_V_ROWS, tq), F32)],
        compiler_params=_cparams("parallel", "parallel", "arbitrary"), name="attention",
    )(q, k, v)


POST_ROW_BLOCK = 256


def _post_kernel(*refs, n_in):
    h_ref = refs[0]
    a_refs = refs[1:1 + n_in]
    w_refs = refs[1 + n_in:1 + 2 * n_in]
    g_ref = refs[1 + 2 * n_in]
    o_ref = refs[2 + 2 * n_in]
    rows_per_block = min(POST_ROW_BLOCK, h_ref.shape[0])
    for r0 in range(0, h_ref.shape[0], rows_per_block):
        rows = slice(r0, r0 + rows_per_block)
        m = _dot(a_refs[0][rows, :], w_refs[0][...])
        for a_ref, w_ref in zip(a_refs[1:], w_refs[1:]):
            m = m + _dot(a_ref[rows, :], w_ref[...])
        o_ref[rows, :] = h_ref[rows, :] + _rms(m, g_ref[...])


def _post(h2, acts, wts, g):
    n, d = h2.shape
    tm = _pick_tile(n, (1024, 512, 384, 256, 128))
    rows = lambda i: (i, 0)
    in_specs = [pl.BlockSpec((tm, d), rows)]
    in_specs += [pl.BlockSpec((tm, a.shape[1]), rows) for a in acts]
    in_specs += [_const_spec(w.shape) for w in wts] + [_const_spec(g.shape)]
    return pl.pallas_call(
        functools.partial(_post_kernel, n_in=len(acts)), grid=(n // tm,), in_specs=in_specs,
        out_specs=pl.BlockSpec((tm, d), rows), out_shape=jax.ShapeDtypeStruct((n, d), F32),
        compiler_params=_cparams("parallel"), name="mixer_out",
    )(h2, *acts, *wts, g)


def _mlp_kernel(h_ref, g1_ref, wup_ref, wdn_ref, g2_ref, o_ref, *, ff_chunk):
    x = h_ref[...]
    hn = _rms(x, g1_ref[...]).astype(BF16)
    d_ff = wup_ref.shape[1]
    y = None
    for c0 in range(0, d_ff, ff_chunk):
        u = jnp.maximum(_dot(hn, wup_ref[:, c0:c0 + ff_chunk]), 0.0)
        part = _dot((u * u).astype(BF16), wdn_ref[c0:c0 + ff_chunk, :])
        y = part if y is None else y + part
    o_ref[...] = x + _rms(y, g2_ref[...])


def _mlp(h2, g1, wup, wdn, g2):
    n, d = h2.shape
    tm = _pick_tile(n, (512, 256, 128))
    rows = lambda i: (i, 0)
    in_specs = [pl.BlockSpec((tm, d), rows), _const_spec(g1.shape), _const_spec(wup.shape),
                _const_spec(wdn.shape), _const_spec(g2.shape)]
    return pl.pallas_call(
        functools.partial(_mlp_kernel, ff_chunk=1024), grid=(n // tm,), in_specs=in_specs,
        out_specs=pl.BlockSpec((tm, d), rows), out_shape=jax.ShapeDtypeStruct((n, d), F32),
        compiler_params=_cparams("parallel"), name="mlp",
    )(h2, g1, wup, wdn, g2)


def _mix_mlp_kernel(*refs, n_in, ff_chunk):
    h_ref = refs[0]
    a_refs = refs[1:1 + n_in]
    w_refs = refs[1 + n_in:1 + 2 * n_in]
    gp_ref, g1_ref, wup_ref, wdn_ref, g2_ref, o_ref = refs[1 + 2 * n_in:]
    m = _dot(a_refs[0][...], w_refs[0][...])
    for a_ref, w_ref in zip(a_refs[1:], w_refs[1:]):
        m = m + _dot(a_ref[...], w_ref[...])
    x = h_ref[...] + _rms(m, gp_ref[...])
    hn = _rms(x, g1_ref[...]).astype(BF16)
    y = None
    for c0 in range(0, wup_ref.shape[1], ff_chunk):
        u = jnp.maximum(_dot(hn, wup_ref[:, c0:c0 + ff_chunk]), 0.0)
        part = _dot((u * u).astype(BF16), wdn_ref[c0:c0 + ff_chunk, :])
        y = part if y is None else y + part
    o_ref[...] = x + _rms(y, g2_ref[...])


def _resident_spec(shape):
    nd = len(shape)
    return pl.BlockSpec(shape, lambda *_: (0,) * nd, pipeline_mode=pl.Buffered(1))


def _mix_mlp(h2, acts, wts, gp, g1, wup, wdn, g2):
    n, d = h2.shape
    tm = _pick_tile(n, (512, 256, 128))
    rows = lambda i: (i, 0)
    in_specs = [pl.BlockSpec((tm, d), rows)]
    in_specs += [pl.BlockSpec((tm, a.shape[1]), rows) for a in acts]
    in_specs += [_resident_spec(w.shape) for w in (*wts, gp, g1, wup, wdn, g2)]
    return pl.pallas_call(
        functools.partial(_mix_mlp_kernel, n_in=len(acts), ff_chunk=1024), grid=(n // tm,), in_specs=in_specs,
        out_specs=pl.BlockSpec((tm, d), rows), out_shape=jax.ShapeDtypeStruct((n, d), F32),
        compiler_params=_cparams("parallel"), name="mix_mlp",
    )(h2, *acts, *wts, gp, g1, wup, wdn, g2)


def _mlp_last(h3, g1, wup, wdn, g2, lead):
    b, tp, d = h3.shape
    seq = tp - lead
    tm = _pick_tile(seq, (512, 256, 128))
    in_specs = [pl.BlockSpec((pl.Element(tm), pl.Element(d)), lambda bi, i: (pl.multiple_of(bi * tp + lead + i * tm, CHUNK), 0)),
                _const_spec(g1.shape), _const_spec(wup.shape), _const_spec(wdn.shape), _const_spec(g2.shape)]
    return pl.pallas_call(
        functools.partial(_mlp_kernel, ff_chunk=1024), grid=(b, seq // tm), in_specs=in_specs,
        out_specs=pl.BlockSpec((None, tm, d), lambda bi, i: (bi, i, 0)),
        out_shape=jax.ShapeDtypeStruct((b, seq, d), F32),
        compiler_params=_cparams("parallel", "parallel"), name="mlp_last",
    )(h3.reshape(b * tp, d), g1, wup, wdn, g2)


SEG_LEN = CHUNK // SUBLANES


def _interleave_matrix():
    p = np.zeros((CHUNK, CHUNK), np.float32)
    for j in range(SEG_LEN):
        for s in range(SUBLANES):
            p[SUBLANES * j + s, SEG_LEN * s + j] = 1.0
    return p


def _reorder_rows(perm, x):
    tiles = [_dot(perm, x[t0:t0 + CHUNK]).astype(BF16) for t0 in range(0, x.shape[0], CHUNK)]
    return tiles[0] if len(tiles) == 1 else jnp.concatenate(tiles, axis=0)


def _causal_conv_interleaved(pre, tail_scr, w_ref, b_ref, cols):
    k = w_ref.shape[0]
    halo = (k - 1) * SUBLANES
    width = pre.shape[1]
    first_sub = lax.broadcasted_iota(jnp.int32, (k - 1, SUBLANES, width), 1) == 0
    prev_last = tail_scr[:, cols]
    outs = []
    for t0 in range(0, pre.shape[0], CHUNK):
        tile = pre[t0:t0 + CHUNK]
        last = tile[CHUNK - halo:]
        fix = jnp.where(first_sub, pltpu.roll(prev_last.reshape(k - 1, SUBLANES, width), 1, 1),
                        pltpu.roll(last.reshape(k - 1, SUBLANES, width), 1, 1)).reshape(halo, width)
        out = b_ref[:, cols]
        for i in range(k):
            shift = k - 1 - i
            if shift == 0:
                xs = tile
            else:
                xs = jnp.concatenate([fix[halo - shift * SUBLANES:], tile[:CHUNK - shift * SUBLANES]], axis=0)
            out = out + xs * w_ref[i:i + 1, cols]
        outs.append(out)
        prev_last = last
    tail_scr[:, cols] = prev_last
    return jnp.concatenate(outs, axis=0)


def _proj_odd_kernel(h_ref, g_ref, wx_ref, wy_ref, cw_ref, cb_ref, perm_ref, x_ref, gate_ref, tail_scr):
    @pl.when(pl.program_id(0) == 0)
    def _():
        tail_scr[...] = jnp.zeros_like(tail_scr)

    hn = _reorder_rows(perm_ref[...], _rms(h_ref[...], g_ref[...]).astype(BF16))
    for c0 in range(0, LRU_WIDTH, LRU_CONV_COLS):
        cols = slice(c0, c0 + LRU_CONV_COLS)
        x_ref[:, cols] = _causal_conv_interleaved(_dot(hn, wx_ref[:, cols]), tail_scr, cw_ref, cb_ref,
                                                  cols).astype(BF16)
        y = _dot(hn, wy_ref[:, cols])
        gelu = 0.5 * y * (1.0 + jnp.tanh(np.sqrt(2.0 / np.pi).astype(np.float32) * (y + 0.044715 * (y * y * y))))
        gate_ref[:, cols] = gelu.astype(BF16)


def _proj_odd(h2, g, wx, wy, cw, cb, perm):
    n, d = h2.shape
    tm = _pick_tile(n, (1024, 512, 384, 256, 128))
    rows = lambda i: (i, 0)
    w = wx.shape[1]
    return pl.pallas_call(
        _proj_odd_kernel, grid=(n // tm,),
        in_specs=[pl.BlockSpec((tm, d), rows)] + [_const_spec(a.shape) for a in (g, wx, wy, cw, cb, perm)],
        out_specs=(pl.BlockSpec((tm, w), rows), pl.BlockSpec((tm, w), rows)),
        out_shape=(jax.ShapeDtypeStruct((n, w), BF16), jax.ShapeDtypeStruct((n, w), BF16)),
        scratch_shapes=[pltpu.VMEM(((cw.shape[0] - 1) * SUBLANES, w), F32)],
        compiler_params=_cparams("arbitrary"), name="proj_odd",
    )(h2, g, wx, wy, cw, cb, perm)


LRU_BATCH_BLOCK = 8


def _rglru_kernel(x_ref, gate_ref, wa_ref, ba_ref, wi_ref, bi_ref, lam_ref, unperm_ref, o_ref, h_scr):
    t = pl.program_id(1)

    @pl.when(t == 0)
    def _():
        h_scr[...] = jnp.zeros_like(h_scr)

    for bb in range(x_ref.shape[0]):
        _rglru_tile(t, x_ref.at[bb], gate_ref.at[bb], wa_ref, ba_ref, wi_ref, bi_ref, lam_ref, unperm_ref,
                    o_ref.at[bb], h_scr.at[bb])


def _rglru_tile(t, x_ref, gate_ref, wa_ref, ba_ref, wi_ref, bi_ref, lam_ref, unperm_ref, o_ref, h_scr):
    L = x_ref.shape[0]
    xr = x_ref[...].astype(F32)
    rs, gs = [], []
    for nb in range(LRU_BLOCKS):
        sl = slice(nb * LRU_BLOCK, (nb + 1) * LRU_BLOCK)
        xb = x_ref[:, sl]
        rs.append(_dot(xb, wa_ref[nb]))
        gs.append(_dot(xb, wi_ref[nb]))
    r = _sigmoid(jnp.concatenate(rs, axis=1) + ba_ref[...])
    ig = _sigmoid(jnp.concatenate(gs, axis=1) + bi_ref[...])
    log_a = (-LRU_C) * r * _softplus(-lam_ref[...])
    a = jnp.exp(log_a)
    row = lax.broadcasted_iota(jnp.int32, (L, 1), 0)
    valid = (t * L + SEG_LEN * (row % SUBLANES) + row // SUBLANES) >= PAD
    om = 1.0 - a * a
    root = jnp.where(om > 0.0, om * lax.rsqrt(om), 0.0)
    u = jnp.where(valid, root * (ig * xr), 0.0)

    slab = lambda v, j: v[j * SUBLANES:(j + 1) * SUBLANES, :]
    h_loc = [slab(u, 0)]
    prod = [slab(a, 0)]
    for j in range(1, SEG_LEN):
        h_loc.append(slab(a, j) * h_loc[-1] + slab(u, j))
        prod.append(slab(a, j) * prod[-1])
    carry_in = h_scr[0:1, :]
    carries = []
    for s in range(SUBLANES):
        carries.append(carry_in)
        carry_in = h_loc[-1][s:s + 1, :] + prod[-1][s:s + 1, :] * carry_in
    h_scr[...] = jnp.broadcast_to(carry_in, h_scr.shape)
    carry = jnp.concatenate(carries, axis=0)
    hs = jnp.concatenate([h_loc[j] + prod[j] * carry for j in range(SEG_LEN)], axis=0)
    o_ref[...] = _reorder_rows(unperm_ref[...], (hs * gate_ref[...].astype(F32)).astype(BF16))


def _rglru(x3, gate3, wa, ba, wi, bi, lam, unperm):
    b, tp, w = x3.shape
    tt = CHUNK
    gb = _pick_tile(b, (LRU_BATCH_BLOCK, 1))
    row3 = lambda bi_, ti: (bi_, ti, 0)
    in_specs = [pl.BlockSpec((gb, tt, w), row3), pl.BlockSpec((gb, tt, w), row3)]
    in_specs += [_const_spec(a.shape) for a in (wa, ba, wi, bi, lam, unperm)]
    return pl.pallas_call(
        _rglru_kernel, grid=(b // gb, tp // tt), in_specs=in_specs,
        out_specs=pl.BlockSpec((gb, tt, w), row3),
        out_shape=jax.ShapeDtypeStruct((b, tp, w), BF16),
        scratch_shapes=[pltpu.VMEM((gb, SUBLANES, w), F32)],
        compiler_params=_cparams("parallel", "arbitrary"), name="rglru",
    )(x3, gate3, wa, ba, wi, bi, lam, unperm)


def _rope_tables(tp):
    f32 = np.float32
    inv = f32(ROPE_BASE) ** (-np.arange(0, MLA_ROPE, 2, dtype=f32) / f32(MLA_ROPE))
    pos = np.maximum(np.arange(tp, dtype=f32) - f32(PAD), f32(0))
    ang = (pos[:, None] * inv[None, :]).astype(f32)
    cos, sin = np.cos(ang).astype(f32), np.sin(ang).astype(f32)
    zeros_lo = np.zeros((tp, MLA_NOPE), f32)
    zeros_hi = np.zeros((tp, LANES - MLA_NOPE - MLA_ROPE), f32)
    ck = np.concatenate([zeros_lo, cos, cos, zeros_hi], axis=1)
    sk = np.concatenate([zeros_lo, sin, sin, zeros_hi], axis=1)
    scale = f32((MLA_NOPE + MLA_ROPE) ** -0.5 * np.log2(np.e))
    cq = np.concatenate([np.ones((tp, MLA_NOPE), f32), cos, cos, zeros_hi], axis=1) * scale
    sq = sk * scale
    return tuple(jnp.asarray(t) for t in (cq, sq, ck, sk))


def _even_weights(w_in, w_q_up, w_kv_up):
    offs = np.cumsum((SSD_D_INNER, SSD_CONV_CH, SSD_HEADS, MLA_Q_RANK, MLA_KV_RANK, MLA_ROPE))
    wz = w_in[:, :offs[0]]
    wxbc = w_in[:, offs[0]:offs[1]]
    wdt = w_in[:, offs[1]:offs[2]]
    wcq = w_in[:, offs[2]:offs[3]]
    wckv = w_in[:, offs[3]:offs[4]]
    wkr = w_in[:, offs[4]:offs[5]]
    hi_pad = LANES - MLA_NOPE - MLA_ROPE
    wdt_p = jnp.pad(wdt, ((0, 0), (0, LANES - SSD_HEADS)))
    wkr_p = jnp.pad(wkr, ((0, 0), (MLA_NOPE, hi_pad)))
    wq = w_q_up.reshape(MLA_Q_RANK, MLA_HEADS, MLA_NOPE + MLA_ROPE)
    wq_p = jnp.pad(wq, ((0, 0), (0, 0), (0, hi_pad))).reshape(MLA_Q_RANK, MLA_HEADS * LANES)
    wkv = w_kv_up.reshape(MLA_KV_RANK, MLA_HEADS, MLA_NOPE + MLA_V)
    wk_p = jnp.pad(wkv[..., :MLA_NOPE], ((0, 0), (0, 0), (0, LANES - MLA_NOPE))).reshape(
        MLA_KV_RANK, MLA_HEADS * LANES)
    wv = wkv[..., MLA_NOPE:].reshape(MLA_KV_RANK, MLA_HEADS * MLA_V)
    bf = lambda a: a.astype(BF16)
    return dict(wz=bf(wz), wxbc=bf(wxbc), wdt=bf(wdt_p), wkr_p=bf(wkr_p), wcq=bf(wcq),
                wckv=bf(wckv), wq_p=bf(wq_p), wk_p=bf(wk_p), wv=bf(wv))


def _row(a):
    return a.reshape(1, -1).astype(F32)


def kernel(x, meta_tokens, mix_pre_g, mix_post_g, mlp_pre_g, mlp_post_g, w_up, w_down, w_in, ssd_conv_w,
           ssd_conv_b, ssd_dt_bias, ssd_a_log, ssd_d, ssd_norm_g, mla_q_norm_g, mla_w_q_up, mla_kv_norm_g,
           mla_w_kv_up, w_out_ab, rg_w_x, rg_w_y, rg_conv_w, rg_conv_b, rg_w_a, rg_b_a, rg_w_i, rg_b_i,
           rg_lambda, rg_w_out):
    b, seq, d = x.shape
    depth = mix_pre_g.shape[0]
    tp = PAD + N_META + seq
    n = b * tp
    meta = jnp.broadcast_to(meta_tokens[None].astype(x.dtype), (b, N_META, d))
    h = jnp.concatenate([jnp.zeros((b, PAD, d), x.dtype), meta, x], axis=1).reshape(n, d)
    tables = _rope_tables(tp)

    for layer in range(depth):
        if layer % 2 == 0:
            e = layer // 2
            w = _even_weights(w_in[e], mla_w_q_up[e], mla_w_kv_up[e])
            wts = (w["wz"], w["wxbc"], w["wdt"], w["wkr_p"], w["wcq"], _row(mla_q_norm_g[e]),
                   w["wckv"], _row(mla_kv_norm_g[e]), w["wq_p"], w["wk_p"], w["wv"],
                   ssd_conv_w[e].astype(F32), _row(ssd_conv_b[e]))
            z, xbc, dt, q, k, v = _proj_even(h.reshape(b, tp, d), _row(mix_pre_g[layer]), wts, tables)
            lane_pad = lambda a: jnp.pad(_row(a), ((0, 0), (0, LANES - SSD_HEADS)))
            y_ssd = _ssd(xbc.reshape(b, tp, SSD_CONV_CH), dt.reshape(b, tp, LANES), z.reshape(b, tp, SSD_D_INNER),
                         lane_pad(ssd_dt_bias[e]), lane_pad(ssd_a_log[e]),
                         _row(jnp.repeat(ssd_d[e], SSD_HEAD_DIM)), _row(ssd_norm_g[e]))
            y_att = _attention(q, k, v, b, tp)
            wo = w_out_ab[e].astype(BF16)
            acts, wts = (y_ssd.reshape(n, SSD_D_INNER), y_att), (wo[:SSD_D_INNER], wo[SSD_D_INNER:])
        else:
            o = layer // 2
            perm = _interleave_matrix()
            xr, gate = _proj_odd(h, _row(mix_pre_g[layer]), rg_w_x[o].astype(BF16), rg_w_y[o].astype(BF16),
                                 rg_conv_w[o].astype(F32), _row(rg_conv_b[o]), jnp.asarray(perm, BF16))
            y = _rglru(xr.reshape(b, tp, LRU_WIDTH), gate.reshape(b, tp, LRU_WIDTH), rg_w_a[o].astype(BF16),
                       _row(rg_b_a[o]), rg_w_i[o].astype(BF16), _row(rg_b_i[o]), _row(rg_lambda[o]),
                       jnp.asarray(perm.T, BF16))
            acts, wts = (y.reshape(n, LRU_WIDTH),), (rg_w_out[o].astype(BF16),)
        mlp_args = (_row(mlp_pre_g[layer]), w_up[layer].astype(BF16), w_down[layer].astype(BF16),
                    _row(mlp_post_g[layer]))
        if layer == depth - 1:
            h = _post(h, acts, wts, _row(mix_post_g[layer]))
            return _mlp_last(h.reshape(b, tp, d), *mlp_args, PAD + N_META)
        h = _mix_mlp(h, acts, wts, _row(mix_post_g[layer]), *mlp_args)
```
